```python
import jax, jax.numpy as jnp
from jax import lax
import numpy as np

D_MODEL = 4096
BATCH = 1
SEQ = 8192
DEPTH = 2

CTX_LEN = 256
GRID_W = 64
HEAD_DIM = 128
GROUP_WIDTH = D_MODEL // 4
MIX_WIDTH = 4 * GROUP_WIDTH
EPS = 1e-6
ROPE_THETA = 10000.0
Q_BLOCK = 128

MLA_HEADS = GROUP_WIDTH // HEAD_DIM
MLA_Q_RANK = 768
MLA_KV_RANK = 512
MLA_NOPE = 128
MLA_ROPE = 64
MLA_V = 128
MLA_SCALE = (MLA_NOPE + MLA_ROPE) ** -0.5

GLA_HEADS = 4
GLA_DK = GROUP_WIDTH // 2 // GLA_HEADS
GLA_DV = GROUP_WIDTH // GLA_HEADS
GLA_GATE_RANK = 16
GLA_GATE_TAU = 16.0
GLA_CHUNK = 64

FNET_GROUPS = 8
FNET_GROUP_DIM = GROUP_WIDTH // FNET_GROUPS

SWA_HEADS = GROUP_WIDTH // HEAD_DIM
SWA_KV_HEADS = 2
SWA_GROUP = SWA_HEADS // SWA_KV_HEADS
SWA_WINDOW = 128
SWA_BLOCK = 128
SWA_SCALE = HEAD_DIM ** -0.5

D_FF = 11008
CONV_W = 3

IN_COLS = (MLA_Q_RANK, MLA_KV_RANK, MLA_ROPE,
           GLA_HEADS * GLA_DK, GLA_HEADS * GLA_DK, GLA_HEADS * GLA_DV, GROUP_WIDTH, 2 * GLA_GATE_RANK,
           GROUP_WIDTH,
           SWA_HEADS * HEAD_DIM, SWA_KV_HEADS * HEAD_DIM, SWA_KV_HEADS * HEAD_DIM)
IN_WIDTH = sum(IN_COLS)

kernel_name = 'hybrid_parallel_head_dit_block'


def rmsnorm(x, g):
    xf = x.astype(jnp.float32)
    y = xf * lax.rsqrt(jnp.mean(xf * xf, axis=-1, keepdims=True) + EPS)
    return (y * g.astype(jnp.float32)).astype(x.dtype)


def modulate(x, shift, scale):
    return x * (1 + scale) + shift


def split_cols(p):
    out, i = [], 0
    for s in IN_COLS:
        out.append(p[..., i:i + s])
        i += s
    return out


def rope_1d(x, pos):
    half = x.shape[-1] // 2
    inv = ROPE_THETA ** (-jnp.arange(half, dtype=jnp.float32) / half)
    ang = pos.astype(jnp.float32)[:, None] * inv[None, :]
    cos = jnp.cos(ang)[:, None, :].astype(x.dtype)
    sin = jnp.sin(ang)[:, None, :].astype(x.dtype)
    x1, x2 = x[..., :half], x[..., half:]
    return jnp.concatenate([x1 * cos - x2 * sin, x1 * sin + x2 * cos], axis=-1)


def rope_2d(x, rows, cols):
    d = x.shape[-1] // 2
    return jnp.concatenate([rope_1d(x[..., :d], rows), rope_1d(x[..., d:], cols)], axis=-1)


def attend_blocks(q, k, v, scale):
    B, N, H, d = q.shape
    nb = N // Q_BLOCK
    qb = q.reshape(B, nb, Q_BLOCK, H, d).transpose(1, 0, 2, 3, 4)

    def one(qi):
        s = jnp.einsum('bqhd,bkhd->bhqk', qi, k).astype(jnp.float32) * scale
        p = jax.nn.softmax(s, axis=-1).astype(v.dtype)
        return jnp.einsum('bhqk,bkhv->bqhv', p, v)

    o = lax.map(one, qb)
    return o.transpose(1, 0, 2, 3, 4).reshape(B, N, H * v.shape[-1])


def mla_q(cq, g_qa, w_uq, rows, cols):
    B, N, _ = cq.shape
    q = (rmsnorm(cq, g_qa) @ w_uq).reshape(B, N, MLA_HEADS, MLA_NOPE + MLA_ROPE)
    q_nope, q_rope = q[..., :MLA_NOPE], q[..., MLA_NOPE:]
    if rows is not None:
        q_rope = rope_2d(q_rope, rows, cols)
    return jnp.concatenate([q_nope, q_rope], axis=-1)


def mla_kv(ckv, kr, g_kva, w_ukv, rows, cols):
    B, N, _ = ckv.shape
    kv = (rmsnorm(ckv, g_kva) @ w_ukv).reshape(B, N, MLA_HEADS, MLA_NOPE + MLA_V)
    k_nope, v = kv[..., :MLA_NOPE], kv[..., MLA_NOPE:]
    k_rope = kr[:, :, None, :]
    if rows is not None:
        k_rope = rope_2d(k_rope, rows, cols)
    k = jnp.concatenate([k_nope, jnp.broadcast_to(k_rope, (B, N, MLA_HEADS, MLA_ROPE))], axis=-1)
    return k, v


def gla_inputs(gq, gk, gv, glr, w_gate_f, b_gate_f, w_gate_b, b_gate_b):
    B, N, _ = gq.shape
    f32 = jnp.float32
    q = gq.reshape(B, N, GLA_HEADS, GLA_DK).astype(f32) * GLA_DK ** -0.5
    k = gk.reshape(B, N, GLA_HEADS, GLA_DK).astype(f32)
    v = gv.reshape(B, N, GLA_HEADS, GLA_DV).astype(f32)
    lr_f, lr_b = glr[..., :GLA_GATE_RANK], glr[..., GLA_GATE_RANK:]
    la_f = jax.nn.log_sigmoid((lr_f @ w_gate_f + b_gate_f).astype(f32)) / GLA_GATE_TAU
    la_b = jax.nn.log_sigmoid((lr_b @ w_gate_b + b_gate_b).astype(f32)) / GLA_GATE_TAU
    return q, k, v, la_f.reshape(B, N, GLA_HEADS, GLA_DK), la_b.reshape(B, N, GLA_HEADS, GLA_DK)


def gla_chunk_scan(q, k, v, log_a, s0):
    B, N, H, DK = q.shape
    DV = v.shape[-1]
    nc = N // GLA_CHUNK

    def to_chunks(t):
        return t.reshape(B, nc, GLA_CHUNK, H, t.shape[-1]).transpose(1, 0, 2, 3, 4)

    causal = jnp.tril(jnp.ones((GLA_CHUNK, GLA_CHUNK), dtype=bool))[None, :, :, None, None]

    def step(s, inp):
        qc, kc, vc, ac = inp
        b = jnp.cumsum(ac, axis=1)
        inter = jnp.einsum('bthk,bhkv->bthv', qc * jnp.exp(b), s)
        decay = jnp.exp(jnp.where(causal, b[:, :, None] - b[:, None, :], -jnp.inf))
        scores = jnp.einsum('bthk,bshk,btshk->bhts', qc, kc, decay)
        intra = jnp.einsum('bhts,bshv->bthv', scores, vc)
        b_last = b[:, -1]
        s_new = jnp.exp(b_last)[..., None] * s + jnp.einsum('bshk,bshv->bhkv', kc * jnp.exp(b_last[:, None] - b), vc)
        return s_new, inter + intra

    s_fin, o = lax.scan(step, s0, (to_chunks(q), to_chunks(k), to_chunks(v), to_chunks(log_a)))
    return o.transpose(1, 0, 2, 3, 4).reshape(B, N, H, DV), s_fin


def gla_bidir(q, k, v, la_f, la_b, s0_f, s0_b):
    fl = lambda t: jnp.flip(t, axis=1)
    o_f, s_f = gla_chunk_scan(q, k, v, la_f, s0_f)
    o_b, s_b = gla_chunk_scan(fl(q), fl(k), fl(v), fl(la_b), s0_b)
    return o_f + fl(o_b), s_f, s_b


def gla_final_state(k, v, log_a):
    b = jnp.cumsum(log_a, axis=1)
    return jnp.einsum('bshk,bshv->bhkv', k * jnp.exp(b[:, -1:] - b), v)


def gla_out(o, r, g_gla):
    B, N = o.shape[:2]
    o = rmsnorm(o, g_gla).reshape(B, N, GLA_HEADS * GLA_DV)
    return (o * jax.nn.silu(r.astype(jnp.float32))).astype(r.dtype)


def fourier_mix(u):
    B, N, _ = u.shape
    uf = u.reshape(B, N, FNET_GROUPS, FNET_GROUP_DIM).astype(jnp.float32)
    y = jnp.fft.fft2(uf, axes=(1, 3), norm='ortho').real
    return y.reshape(B, N, GROUP_WIDTH).astype(u.dtype)


def swa_qkv(sq, sk, sv, rows, cols):
    B, N, _ = sq.shape
    q = sq.reshape(B, N, SWA_HEADS, HEAD_DIM)
    k = sk.reshape(B, N, SWA_KV_HEADS, HEAD_DIM)
    v = sv.reshape(B, N, SWA_KV_HEADS, HEAD_DIM)
    if rows is not None:
        q = rope_2d(q, rows, cols)
        k = rope_2d(k, rows, cols)
    return q, k, v


def sink_softmax(s, sink):
    sink_col = jnp.broadcast_to(sink.astype(jnp.float32), s.shape[:-1] + (1,))
    p = jax.nn.softmax(jnp.concatenate([sink_col, s], axis=-1), axis=-1)
    return p[..., 1:]


def swa_latent(q, k, v, k_ctx, v_ctx, sink):
    B, N = q.shape[:2]
    nb = N // SWA_BLOCK
    span = SWA_BLOCK + 2 * SWA_WINDOW
    pad = ((0, 0), (SWA_WINDOW, SWA_WINDOW), (0, 0), (0, 0))
    kp, vp = jnp.pad(k, pad), jnp.pad(v, pad)
    idx = jnp.arange(nb)[:, None] * SWA_BLOCK + jnp.arange(span)[None, :]
    kw, vw = kp[:, idx], vp[:, idx]
    qb = q.reshape(B, nb, SWA_BLOCK, SWA_KV_HEADS, SWA_GROUP, HEAD_DIM)
    qpos = jnp.arange(N).reshape(nb, SWA_BLOCK)
    kpos = idx - SWA_WINDOW
    mask = (jnp.abs(qpos[:, :, None] - kpos[:, None, :]) <= SWA_WINDOW) & (kpos[:, None, :] >= 0) & (kpos[:, None, :] < N)
    s_win = jnp.einsum('bnqhgd,bnkhd->bnhgqk', qb, kw).astype(jnp.float32) * SWA_SCALE
    s_win = jnp.where(mask[None, :, None, None], s_win, -jnp.inf)
    s_ctx = jnp.einsum('bnqhgd,bchd->bnhgqc', qb, k_ctx).astype(jnp.float32) * SWA_SCALE
    p = sink_softmax(jnp.concatenate([s_ctx, s_win], axis=-1), sink.reshape(1, 1, SWA_KV_HEADS, SWA_GROUP, 1, 1))
    c_len = k_ctx.shape[1]
    p_ctx, p_win = p[..., :c_len].astype(v.dtype), p[..., c_len:].astype(v.dtype)
    o = jnp.einsum('bnhgqc,bchd->bnqhgd', p_ctx, v_ctx) + jnp.einsum('bnhgqk,bnkhd->bnqhgd', p_win, vw)
    return o.reshape(B, N, SWA_HEADS * HEAD_DIM)


def swa_context(q, k, v, sink):
    B, C = q.shape[:2]
    qg = q.reshape(B, C, SWA_KV_HEADS, SWA_GROUP, HEAD_DIM)
    s = jnp.einsum('bqhgd,bkhd->bhgqk', qg, k).astype(jnp.float32) * SWA_SCALE
    p = sink_softmax(s, sink.reshape(1, SWA_KV_HEADS, SWA_GROUP, 1, 1)).astype(v.dtype)
    return jnp.einsum('bhgqk,bkhd->bqhgd', p, v).reshape(B, C, SWA_HEADS * HEAD_DIM)


def conv_ffn(h, w_up, conv_w, conv_b, w_down):
    u = h @ w_up
    up = jnp.pad(u, ((0, 0), (1, 1), (0, 0)))
    u = up[:, :-2] * conv_w[0] + up[:, 1:-1] * conv_w[1] + up[:, 2:] * conv_w[2] + conv_b
    g, a = u[..., :D_FF], u[..., D_FF:]
    return (jax.nn.silu(g) * a) @ w_down


def layer(x, xc, c_act, cc_act, rows, cols, last,
          w_ada, b_ada, g_pre_mix, g_post_mix, g_pre_ffn, g_post_ffn, w_in,
          g_qa, w_uq, g_kva, w_ukv, w_gate_f, b_gate_f, w_gate_b, b_gate_b, g_gla,
          swa_sink, w_out, w_up, conv_w, conv_b, w_down):
    B = x.shape[0]
    mod = (c_act @ w_ada + b_ada)[:, None, :]
    mod_c = cc_act @ w_ada + b_ada
    sh_m, sc_m, gt_m, sh_f, sc_f, gt_f = jnp.split(mod, 6, axis=-1)
    csh_m, csc_m, cgt_m, csh_f, csc_f, cgt_f = jnp.split(mod_c, 6, axis=-1)

    h = modulate(rmsnorm(x, g_pre_mix), sh_m, sc_m)
    hc = modulate(rmsnorm(xc, g_pre_mix), csh_m, csc_m)
    cq, ckv, kr, gq, gk, gv, gr, glr, fu, sq, sk, sv = split_cols(h @ w_in)
    cq_c, ckv_c, kr_c, gq_c, gk_c, gv_c, gr_c, glr_c, fu_c, sq_c, sk_c, sv_c = split_cols(hc @ w_in)

    k_a_c, v_a_c = mla_kv(ckv_c, kr_c, g_kva, w_ukv, None, None)
    q_a = mla_q(cq, g_qa, w_uq, rows, cols)
    k_a, v_a = mla_kv(ckv, kr, g_kva, w_ukv, rows, cols)
    o_a = attend_blocks(q_a, jnp.concatenate([k_a_c, k_a], axis=1), jnp.concatenate([v_a_c, v_a], axis=1), MLA_SCALE)

    qg_c, kg_c, vg_c, la_f_c, la_b_c = gla_inputs(gq_c, gk_c, gv_c, glr_c, w_gate_f, b_gate_f, w_gate_b, b_gate_b)
    if last:
        s_f = gla_final_state(kg_c, vg_c, la_f_c)
        s_b = gla_final_state(jnp.flip(kg_c, 1), jnp.flip(vg_c, 1), jnp.flip(la_b_c, 1))
    else:
        zeros = jnp.zeros((B, GLA_HEADS, GLA_DK, GLA_DV), jnp.float32)
        o_g_c, s_f, s_b = gla_bidir(qg_c, kg_c, vg_c, la_f_c, la_b_c, zeros, zeros)
    qg, kg, vg, la_f, la_b = gla_inputs(gq, gk, gv, glr, w_gate_f, b_gate_f, w_gate_b, b_gate_b)
    o_g, _, _ = gla_bidir(qg, kg, vg, la_f, la_b, s_f, s_b)
    o_b = gla_out(o_g, gr, g_gla)

    o_c = fourier_mix(fu)

    _, k_d_c, v_d_c = swa_qkv(sq_c, sk_c, sv_c, None, None)
    q_d, k_d, v_d = swa_qkv(sq, sk, sv, rows, cols)
    o_d = swa_latent(q_d, k_d, v_d, k_d_c, v_d_c, swa_sink)

    y = jnp.concatenate([o_a, o_b, o_c, o_d], axis=-1) @ w_out
    x = x + gt_m * rmsnorm(y, g_post_mix)
    h2 = modulate(rmsnorm(x, g_pre_ffn), sh_f, sc_f)
    x = x + gt_f * rmsnorm(conv_ffn(h2, w_up, conv_w, conv_b, w_down), g_post_ffn)
    if last:
        return x, None

    o_a_c = attend_blocks(mla_q(cq_c, g_qa, w_uq, None, None), k_a_c, v_a_c, MLA_SCALE)
    o_b_c = gla_out(o_g_c, gr_c, g_gla)
    o_c_c = fourier_mix(fu_c)
    q_d_c = sq_c.reshape(B, sq_c.shape[1], SWA_HEADS, HEAD_DIM)
    o_d_c = swa_context(q_d_c, k_d_c, v_d_c, swa_sink)
    yc = jnp.concatenate([o_a_c, o_b_c, o_c_c, o_d_c], axis=-1) @ w_out
    xc = xc + cgt_m * rmsnorm(yc, g_post_mix)
    hc2 = modulate(rmsnorm(xc, g_pre_ffn), csh_f, csc_f)
    xc = xc + cgt_f * rmsnorm(conv_ffn(hc2, w_up, conv_w, conv_b, w_down), g_post_ffn)
    return x, xc


def setup_inputs(seed: int = 0) -> dict:
    key = jax.random.key(seed)
    ks = jax.random.split(key, 32)
    f32 = jnp.float32
    nrm = lambda k, shape, s: jax.random.normal(k, shape, f32) * s
    gain = lambda k, n: 1.0 + nrm(k, (DEPTH, n), 0.02)
    center = (jnp.arange(CONV_W) == CONV_W // 2).astype(f32)[None, :, None]
    return {
        'x': nrm(ks[0], (BATCH, SEQ, D_MODEL), 1.0),
        'c': nrm(ks[1], (BATCH, D_MODEL), 1.0),
        'ctx': nrm(ks[2], (BATCH, CTX_LEN, D_MODEL), 1.0),
        'c_ctx': nrm(ks[3], (D_MODEL,), 1.0),
        'w_ada': nrm(ks[4], (DEPTH, D_MODEL, 6 * D_MODEL), D_MODEL ** -0.5),
        'b_ada': nrm(ks[5], (DEPTH, 6 * D_MODEL), 0.02),
        'g_pre_mix': gain(ks[6], D_MODEL),
        'g_post_mix': gain(ks[7], D_MODEL),
        'g_pre_ffn': gain(ks[8], D_MODEL),
        'g_post_ffn': gain(ks[9], D_MODEL),
        'w_in': nrm(ks[10], (DEPTH, D_MODEL, IN_WIDTH), D_MODEL ** -0.5),
        'g_qa': gain(ks[11], MLA_Q_RANK),
        'w_uq': nrm(ks[12], (DEPTH, MLA_Q_RANK, MLA_HEADS * (MLA_NOPE + MLA_ROPE)), MLA_Q_RANK ** -0.5),
        'g_kva': gain(ks[13], MLA_KV_RANK),
        'w_ukv': nrm(ks[14], (DEPTH, MLA_KV_RANK, MLA_HEADS * (MLA_NOPE + MLA_V)), MLA_KV_RANK ** -0.5),
        'w_gate_f': nrm(ks[15], (DEPTH, GLA_GATE_RANK, GLA_HEADS * GLA_DK), GLA_GATE_RANK ** -0.5),
        'b_gate_f': nrm(ks[16], (DEPTH, GLA_HEADS * GLA_DK), 0.1),
        'w_gate_b': nrm(ks[17], (DEPTH, GLA_GATE_RANK, GLA_HEADS * GLA_DK), GLA_GATE_RANK ** -0.5),
        'b_gate_b': nrm(ks[18], (DEPTH, GLA_HEADS * GLA_DK), 0.1),
        'g_gla': gain(ks[19], GLA_DV),
        'swa_sink': nrm(ks[20], (DEPTH, SWA_HEADS), 0.5),
        'w_out': nrm(ks[21], (DEPTH, MIX_WIDTH, D_MODEL), MIX_WIDTH ** -0.5),
        'w_up': nrm(ks[22], (DEPTH, D_MODEL, 2 * D_FF), D_MODEL ** -0.5),
        'conv_w': center + nrm(ks[23], (DEPTH, CONV_W, 2 * D_FF), 0.3),
        'conv_b': nrm(ks[24], (DEPTH, 2 * D_FF), 0.02),
        'w_down': nrm(ks[25], (DEPTH, D_FF, D_MODEL), D_FF ** -0.5),
    }


def reference(x, c, ctx, c_ctx, w_ada, b_ada, g_pre_mix, g_post_mix, g_pre_ffn, g_post_ffn, w_in,
              g_qa, w_uq, g_kva, w_ukv, w_gate_f, b_gate_f, w_gate_b, b_gate_b, g_gla,
              swa_sink, w_out, w_up, conv_w, conv_b, w_down):
    n_tok = x.shape[1]
    ROWS = n_tok // GRID_W
    rows = jnp.repeat(jnp.arange(ROWS, dtype=jnp.int32), GRID_W)
    cols = jnp.tile(jnp.arange(GRID_W, dtype=jnp.int32), ROWS)
    c_act = jax.nn.silu(c)
    cc_act = jax.nn.silu(c_ctx)
    xc = ctx
    for l in range(DEPTH):
        x, xc = layer(x, xc, c_act, cc_act, rows, cols, l == DEPTH - 1,
                      w_ada[l], b_ada[l], g_pre_mix[l], g_post_mix[l], g_pre_ffn[l], g_post_ffn[l], w_in[l],
                      g_qa[l], w_uq[l], g_kva[l], w_ukv[l], w_gate_f[l], b_gate_f[l], w_gate_b[l], b_gate_b[l], g_gla[l],
                      swa_sink[l], w_out[l], w_up[l], conv_w[l], conv_b[l], w_down[l])
    return x
```

```python
import functools

import numpy as np
import jax
import jax.numpy as jnp
from jax import lax
from jax.experimental import pallas as pl
from jax.experimental.pallas import tpu as pltpu

F32 = jnp.float32
BF16 = jnp.bfloat16

GRID_W = 64
EPS = 1e-6
ROPE_THETA = 10000.0
MLA_HEADS = 8
MLA_Q_RANK = 768
MLA_KV_RANK = 512
MLA_NOPE = 128
MLA_ROPE = 64
MLA_SCALE = (MLA_NOPE + MLA_ROPE) ** -0.5
GLA_HEADS = 4
GLA_DK = 128
GLA_DV = 256
GLA_GATE_RANK = 16
GLA_GATE_TAU = 16.0
GLA_CHUNK = 128
FNET_GROUPS = 8
FFT_N2 = 64
SWA_HEADS = 8
SWA_KV_HEADS = 2
SWA_GROUP = 4
SWA_BLOCK = 128
SWA_QBLOCK = 512
FLASH_SUBTILES = 4
SWA_SCALE = 128 ** -0.5
LANES = 128
ROW_TILE = 256

P_FU, P_GV, P_GR, P_CQ, P_SK, P_SQ = 0, 1024, 2048, 3072, 3840, 4096
P_CKV, P_GQ, P_GK, P_SV, P_KR, P_GLR = 5120, 5632, 6144, 6656, 6912, 7040
P_WIDTH = 7168

_NT = (((1,), (1,)), ((), ()))
_MIB = 1024 * 1024


def _params(sem, vmem_mib=40):
    return pltpu.CompilerParams(dimension_semantics=sem, vmem_limit_bytes=vmem_mib * _MIB)


def _silu(v):
    return v / (1.0 + jnp.exp(-v))


def _rms(v, g):
    return v * lax.rsqrt(jnp.mean(v * v, axis=-1, keepdims=True) + EPS) * g


def _pick(is_ctx, ref):
    return jnp.where(is_ctx, ref[1:2, :], ref[0:1, :])


def _ada_kernel(c_ref, w_ref, b_ref, o_ref):
    a = _silu(c_ref[...]).astype(BF16)
    o_ref[...] = jnp.dot(a, w_ref[...].astype(BF16), preferred_element_type=F32) + b_ref[...]


def _ada(cc, w_ada, b_ada):
    L, D, N6 = w_ada.shape
    tn = 512
    return pl.pallas_call(
        _ada_kernel,
        grid=(L, N6 // tn),
        in_specs=[pl.BlockSpec((8, D), lambda l, j: (0, 0)),
                  pl.BlockSpec((None, D, tn), lambda l, j: (l, 0, j)),
                  pl.BlockSpec((None, 1, tn), lambda l, j: (l, 0, j))],
        out_specs=pl.BlockSpec((None, 8, tn), lambda l, j: (l, 0, j)),
        out_shape=jax.ShapeDtypeStruct((L, 8, N6), F32),
        compiler_params=_params(("arbitrary", "arbitrary")),
    )(cc, w_ada, b_ada.reshape(L, 1, N6))


def _rms_mod_kernel(n_lat_tiles, x_ref, c_ref, g_ref, sh_ref, sc_ref, h_ref):
    is_ctx = pl.program_id(0) >= n_lat_tiles
    xv = jnp.where(is_ctx, c_ref[...], x_ref[...])
    y = _rms(xv, g_ref[...])
    h_ref[...] = (y * (1.0 + _pick(is_ctx, sc_ref)) + _pick(is_ctx, sh_ref)).astype(BF16)


def _rms_mod(x2d, ctx2d, g, mod, k_shift):
    N, D = x2d.shape
    C = ctx2d.shape[0]
    tm = ROW_TILE
    nl, nc = N // tm, C // tm
    return pl.pallas_call(
        functools.partial(_rms_mod_kernel, nl),
        grid=(nl + nc,),
        in_specs=[pl.BlockSpec((tm, D), lambda i: (jnp.minimum(i, nl - 1), 0)),
                  pl.BlockSpec((tm, D), lambda i: (jnp.maximum(i - nl, 0), 0)),
                  pl.BlockSpec((1, D), lambda i: (0, 0)),
                  pl.BlockSpec((8, D), lambda i: (0, k_shift)),
                  pl.BlockSpec((8, D), lambda i: (0, k_shift + 1))],
        out_specs=pl.BlockSpec((tm, D), lambda i: (i, 0)),
        out_shape=jax.ShapeDtypeStruct((N + C, D), BF16),
        compiler_params=_params(("arbitrary",)),
    )(x2d, ctx2d, g.reshape(1, D), mod, mod)


def _resid_kernel(n_lat_tiles, split_x, with_h, *refs):
    refs = list(refs)
    x_ref = refs.pop(0)
    c_ref = refs.pop(0) if split_x else None
    y_ref, gpost_ref, gate_ref = refs.pop(0), refs.pop(0), refs.pop(0)
    if with_h:
        gpre_ref, sh_ref, sc_ref = refs.pop(0), refs.pop(0), refs.pop(0)
    x1_ref = refs.pop(0)
    is_ctx = pl.program_id(0) >= n_lat_tiles
    xv = jnp.where(is_ctx, c_ref[...], x_ref[...]) if split_x else x_ref[...]
    x1 = xv + _pick(is_ctx, gate_ref) * _rms(y_ref[...], gpost_ref[...])
    x1_ref[...] = x1
    if with_h:
        h_ref = refs.pop(0)
        n = _rms(x1, gpre_ref[...])
        h_ref[...] = (n * (1.0 + _pick(is_ctx, sc_ref)) + _pick(is_ctx, sh_ref)).astype(BF16)


def _resid(xs, y, g_post, mod_gate, k_gate, n_lat, n_rows, pre=None):
    D = y.shape[1]
    tm = ROW_TILE
    nl = n_lat // tm
    nt = n_rows // tm
    split_x = len(xs) == 2
    row = lambda i: (i, 0)
    vec = lambda i: (0, 0)
    if split_x:
        in_specs = [pl.BlockSpec((tm, D), lambda i: (jnp.minimum(i, nl - 1), 0)),
                    pl.BlockSpec((tm, D), lambda i: (jnp.maximum(i - nl, 0), 0))]
    else:
        in_specs = [pl.BlockSpec((tm, D), row)]
    in_specs += [pl.BlockSpec((tm, D), row), pl.BlockSpec((1, D), vec),
                 pl.BlockSpec((8, D), lambda i: (0, k_gate))]
    args = list(xs) + [y, g_post.reshape(1, D), mod_gate]
    out_specs = [pl.BlockSpec((tm, D), row)]
    out_shape = [jax.ShapeDtypeStruct((n_rows, D), F32)]
    if pre is not None:
        g_pre, mod_pre, k_shift = pre
        in_specs += [pl.BlockSpec((1, D), vec),
                     pl.BlockSpec((8, D), lambda i: (0, k_shift)),
                     pl.BlockSpec((8, D), lambda i: (0, k_shift + 1))]
        args += [g_pre.reshape(1, D), mod_pre, mod_pre]
        out_specs.append(pl.BlockSpec((tm, D), row))
        out_shape.append(jax.ShapeDtypeStruct((n_rows, D), BF16))
    return pl.pallas_call(
        functools.partial(_resid_kernel, nl, split_x, pre is not None),
        grid=(nt,), in_specs=in_specs, out_specs=out_specs, out_shape=out_shape,
        compiler_params=_params(("arbitrary",)),
    )(*args)


def _mm_kernel(n_in, *refs):
    w_ref, o_ref = refs[n_in], refs[n_in + 1]
    kc = w_ref.shape[0] // n_in
    acc = None
    for k in range(n_in):
        d = jnp.dot(refs[k][...].astype(BF16), w_ref[kc * k:kc * (k + 1), :], preferred_element_type=F32)
        acc = d if acc is None else acc + d
    o_ref[...] = acc.astype(o_ref.dtype)


def _matmul(a_list, w3, layer, tm, tn, out_dtype):
    M = a_list[0].shape[0]
    _, K, Nn = w3.shape
    kc = K // len(a_list)
    return pl.pallas_call(
        functools.partial(_mm_kernel, len(a_list)),
        grid=(M // tm, Nn // tn),
        in_specs=[pl.BlockSpec((tm, kc), lambda i, j: (i, 0)) for _ in a_list]
        + [pl.BlockSpec((None, K, tn), lambda i, j: (layer, 0, j))],
        out_specs=pl.BlockSpec((tm, tn), lambda i, j: (i, j)),
        out_shape=jax.ShapeDtypeStruct((M, Nn), out_dtype),
        compiler_params=_params(("arbitrary", "arbitrary"), 48),
    )(*a_list, w3)


def _rope_tables(n_lat, n_ctx, n_real, half):
    t = jnp.arange(n_lat, dtype=jnp.int32)
    rows = (t // GRID_W).astype(F32)[:, None]
    cols = (t % GRID_W).astype(F32)[:, None]
    lane = np.arange(LANES)
    grp = (lane // (2 * half)) % 2
    jj = lane % (2 * half)
    first = jj < half
    real = lane < n_real
    inv = ROPE_THETA ** (-jnp.asarray(jj % half, F32) / half)
    ang = jnp.where(jnp.asarray(grp == 0)[None, :], rows, cols) * inv[None, :]
    cos, sin = jnp.cos(ang), jnp.sin(ang)
    real_j, first_j = jnp.asarray(real)[None, :], jnp.asarray(first)[None, :]
    cos_t = jnp.where(real_j, cos, 1.0)
    sa_t = jnp.where(real_j & first_j, -sin, 0.0)
    sb_t = jnp.where(real_j & (~first_j), sin, 0.0)
    pad = lambda a, v: jnp.concatenate([a, jnp.full((n_ctx, LANES), v, F32)], axis=0)
    return pad(cos_t, 1.0), pad(sa_t, 0.0), pad(sb_t, 0.0)


def _rope(v, cos, sa, sb, half):
    return v * cos + pltpu.roll(v, LANES - half, 1) * sa + pltpu.roll(v, half, 1) * sb


def _mla_q_kernel(cq_ref, g_ref, w_ref, cos_ref, sa_ref, sb_ref, o_ref):
    n = _rms(cq_ref[...], g_ref[...]).astype(BF16)
    a = jnp.dot(n, w_ref[...], preferred_element_type=F32)
    cos, sa, sb = cos_ref[...], sa_ref[...], sb_ref[...]
    for h in range(MLA_HEADS):
        lo = 2 * LANES * h
        o_ref[:, lo:lo + LANES] = (a[:, lo:lo + LANES] * MLA_SCALE).astype(BF16)
        r = _rope(a[:, lo + LANES:lo + 2 * LANES], cos, sa, sb, MLA_ROPE // 4)
        o_ref[:, lo + LANES:lo + 2 * LANES] = (r * MLA_SCALE).astype(BF16)


def _mla_q(p, g_qa, w_uq_ext, tabs):
    T = p.shape[0]
    tm = ROW_TILE
    tab = pl.BlockSpec((tm, LANES), lambda i: (i, 0))
    return pl.pallas_call(
        _mla_q_kernel,
        grid=(T // tm,),
        in_specs=[pl.BlockSpec((tm, MLA_Q_RANK), lambda i: (i, P_CQ // MLA_Q_RANK)),
                  pl.BlockSpec((1, MLA_Q_RANK), lambda i: (0, 0)),
                  pl.BlockSpec((MLA_Q_RANK, 2 * LANES * MLA_HEADS), lambda i: (0, 0)),
                  tab, tab, tab],
        out_specs=pl.BlockSpec((tm, 2 * LANES * MLA_HEADS), lambda i: (i, 0)),
        out_shape=jax.ShapeDtypeStruct((T, 2 * LANES * MLA_HEADS), BF16),
        compiler_params=_params(("arbitrary",)),
    )(p, g_qa.reshape(1, -1), w_uq_ext, *tabs)


def _mla_kv_kernel(ckv_ref, kr_ref, g_ref, w_ref, cos_ref, sa_ref, sb_ref, k_ref, v_ref):
    n = _rms(ckv_ref[...], g_ref[...]).astype(BF16)
    a = jnp.dot(n, w_ref[...], preferred_element_type=F32)
    kr = _rope(kr_ref[...], cos_ref[...], sa_ref[...], sb_ref[...], MLA_ROPE // 4).astype(BF16)
    ones = jnp.ones(kr.shape, BF16)
    for h in range(MLA_HEADS):
        lo = 2 * LANES * h
        k_ref[:, lo:lo + LANES] = a[:, lo:lo + LANES].astype(BF16)
        k_ref[:, lo + LANES:lo + 2 * LANES] = kr
        v_ref[:, lo:lo + LANES] = a[:, lo + LANES:lo + 2 * LANES].astype(BF16)
        v_ref[:, lo + LANES:lo + 2 * LANES] = ones


def _mla_kv(p, g_kva, w_ukv, tabs):
    T = p.shape[0]
    tm = ROW_TILE
    tab = pl.BlockSpec((tm, LANES), lambda i: (i, 0))
    wide = 2 * LANES * MLA_HEADS
    return pl.pallas_call(
        _mla_kv_kernel,
        grid=(T // tm,),
        in_specs=[pl.BlockSpec((tm, MLA_KV_RANK), lambda i: (i, P_CKV // MLA_KV_RANK)),
                  pl.BlockSpec((tm, LANES), lambda i: (i, P_KR // LANES)),
                  pl.BlockSpec((1, MLA_KV_RANK), lambda i: (0, 0)),
                  pl.BlockSpec((MLA_KV_RANK, wide), lambda i: (0, 0)),
                  tab, tab, tab],
        out_specs=[pl.BlockSpec((tm, wide), lambda i: (i, 0)),
                   pl.BlockSpec((tm, wide), lambda i: (i, 0))],
        out_shape=[jax.ShapeDtypeStruct((T, wide), BF16),
                   jax.ShapeDtypeStruct((T, wide), BF16)],
        compiler_params=_params(("arbitrary",)),
    )(p, p, g_kva.reshape(1, -1), w_ukv, *tabs)


def _flash_kernel(q_ref, k_ref, v_ref, *rest):
    o_ref, m_sc, acc_sc = rest[-3:]
    kk = pl.program_id(2)

    @pl.when(kk == 0)
    def _():
        m_sc[...] = jnp.full(m_sc.shape, -1e30, F32)
        acc_sc[...] = jnp.zeros(acc_sc.shape, F32)

    rc = q_ref.shape[0] // FLASH_SUBTILES
    subs = [slice(r * rc, (r + 1) * rc) for r in range(FLASH_SUBTILES)]
    scores = [lax.dot_general(q_ref[rows, :], k_ref[...], _NT, preferred_element_type=F32) for rows in subs]
    for rows, s in zip(subs, scores):
        m_prev = m_sc[rows, :]
        m_new = jnp.maximum(m_prev, jnp.max(s, axis=1, keepdims=True))
        alpha = jnp.exp(m_prev - m_new)
        p = jnp.exp((s - m_new).astype(BF16))
        acc_sc[rows, :] = alpha * acc_sc[rows, :] + jnp.dot(p, v_ref[...], preferred_element_type=F32)
        m_sc[rows, :] = m_new

    @pl.when(kk == pl.num_programs(2) - 1)
    def _():
        acc = acc_sc[...]
        o_ref[...] = (acc[:, :LANES] / acc[:, LANES:LANES + 1]).astype(o_ref.dtype)


def _flash(q, k, v, out_prev, n_rows, q_blk0, n_qblk, k_blk0, n_kblk, tq, tk):
    dq = 2 * LANES
    in_specs = [pl.BlockSpec((tq, dq), lambda h, i, kk: (i + q_blk0, h)),
                pl.BlockSpec((tk, dq), lambda h, i, kk: (kk + k_blk0, h)),
                pl.BlockSpec((tk, dq), lambda h, i, kk: (kk + k_blk0, h))]
    args = [q, k, v]
    aliases = {}
    if out_prev is not None:
        in_specs.append(pl.BlockSpec(memory_space=pl.ANY))
        args.append(out_prev)
        aliases = {3: 0}
    return pl.pallas_call(
        _flash_kernel,
        grid=(MLA_HEADS, n_qblk, n_kblk),
        in_specs=in_specs,
        out_specs=pl.BlockSpec((tq, LANES), lambda h, i, kk: (i + q_blk0, h)),
        out_shape=jax.ShapeDtypeStruct((n_rows, MLA_HEADS * LANES), BF16),
        scratch_shapes=[pltpu.VMEM((tq, 1), F32), pltpu.VMEM((tq, dq), F32)],
        input_output_aliases=aliases,
        compiler_params=_params(("arbitrary", "arbitrary", "arbitrary"), 48),
    )(*args)


def _gla_direction(q, k, v, glr, wg, bg, tri, last_row, st_ref, o_ref):
    z = jnp.dot(glr.astype(BF16), wg, preferred_element_type=F32) + bg
    la = (jnp.minimum(z, 0.0) - jnp.log(1.0 + jnp.exp(-jnp.abs(z)))) * (1.0 / GLA_GATE_TAU)
    ones = jnp.where(tri, 1.0, 0.0).astype(BF16)
    la_hi = la.astype(BF16)
    la_lo = (la - la_hi.astype(F32)).astype(BF16)
    b = (jnp.dot(ones, la_hi, preferred_element_type=F32)
         + jnp.dot(ones, la_lo, preferred_element_type=F32))
    b_tot = b[last_row:last_row + 1, :]
    qt = q * (GLA_DK ** -0.5) * jnp.exp(b)
    kt = k * jnp.exp(-b)
    kh = k * jnp.exp(b_tot - b)
    dec = jnp.exp(b_tot)
    for h in range(GLA_HEADS):
        ks = slice(GLA_DK * h, GLA_DK * (h + 1))
        vs = slice(GLA_DV * h, GLA_DV * (h + 1))
        qh = qt[:, ks].astype(BF16)
        vh = v[:, vs]
        st = st_ref[h]
        inter = lax.dot_general(qh, st.astype(BF16), _NT, preferred_element_type=F32)
        sc = lax.dot_general(qh, kt[:, ks].astype(BF16), _NT, preferred_element_type=F32)
        sc = jnp.where(tri, sc, 0.0)
        intra = jnp.dot(sc.astype(BF16), vh.astype(BF16), preferred_element_type=F32)
        o_ref[:, vs] = inter + intra
        st_ref[h] = st * dec[:, ks] + jnp.dot(vh.T.astype(BF16), kh[:, ks].astype(BF16),
                                               preferred_element_type=F32)


def _gla_kernel(qf, kf, vf, gf, qb, kb, vb, gb, wf, bf, wb, bb, of_ref, ob_ref, stf, stb):
    @pl.when(pl.program_id(0) == 0)
    def _():
        stf[...] = jnp.zeros(stf.shape, F32)
        stb[...] = jnp.zeros(stb.shape, F32)

    cs = qf.shape[0]
    t_i = lax.broadcasted_iota(jnp.int32, (cs, cs), 0)
    s_i = lax.broadcasted_iota(jnp.int32, (cs, cs), 1)
    _gla_direction(qf[...], kf[...], vf[...], gf[...], wf[...], bf[...], s_i <= t_i, cs - 1, stf, of_ref)
    _gla_direction(qb[...], kb[...], vb[...], gb[...], wb[...], bb[...], s_i >= t_i, 0, stb, ob_ref)


def _gla(p, n_lat, wf_ext, bf, wb_ext, bb):
    T = p.shape[0]
    cs = GLA_CHUNK
    nch = T // cs
    nlc = n_lat // cs
    fwd = lambda c: (c + nlc) % nch
    bwd = lambda c: nch - 1 - c
    dk, dv = GLA_HEADS * GLA_DK, GLA_HEADS * GLA_DV

    def specs(idx):
        return [pl.BlockSpec((cs, dk), lambda c: (idx(c), P_GQ // dk)),
                pl.BlockSpec((cs, dk), lambda c: (idx(c), P_GK // dk)),
                pl.BlockSpec((cs, dv), lambda c: (idx(c), P_GV // dv)),
                pl.BlockSpec((cs, LANES), lambda c: (idx(c), P_GLR // LANES))]

    wspec = pl.BlockSpec((LANES, dk), lambda c: (0, 0))
    bspec = pl.BlockSpec((1, dk), lambda c: (0, 0))
    return pl.pallas_call(
        _gla_kernel,
        grid=(nch,),
        in_specs=specs(fwd) + specs(bwd) + [wspec, bspec, wspec, bspec],
        out_specs=[pl.BlockSpec((cs, dv), lambda c: (fwd(c), 0)),
                   pl.BlockSpec((cs, dv), lambda c: (bwd(c), 0))],
        out_shape=[jax.ShapeDtypeStruct((T, dv), F32), jax.ShapeDtypeStruct((T, dv), F32)],
        scratch_shapes=[pltpu.VMEM((GLA_HEADS, GLA_DV, GLA_DK), F32),
                        pltpu.VMEM((GLA_HEADS, GLA_DV, GLA_DK), F32)],
        compiler_params=_params(("arbitrary",)),
    )(p, p, p, p, p, p, p, p, wf_ext, bf.reshape(1, dk), wb_ext, bb.reshape(1, dk))


def _gla_out_kernel(of_ref, ob_ref, gr_ref, g_ref, o_ref):
    o = of_ref[...] + ob_ref[...]
    gr = gr_ref[...]
    for h in range(GLA_HEADS):
        vs = slice(GLA_DV * h, GLA_DV * (h + 1))
        o_ref[:, vs] = (_rms(o[:, vs], g_ref[...]) * _silu(gr[:, vs])).astype(BF16)


def _gla_out(o_f, o_b, p, g_gla):
    T, dv = o_f.shape
    tm = ROW_TILE
    row = pl.BlockSpec((tm, dv), lambda i: (i, 0))
    return pl.pallas_call(
        _gla_out_kernel,
        grid=(T // tm,),
        in_specs=[row, row, pl.BlockSpec((tm, dv), lambda i: (i, P_GR // dv)),
                  pl.BlockSpec((1, GLA_DV), lambda i: (0, 0))],
        out_specs=row,
        out_shape=jax.ShapeDtypeStruct((T, dv), BF16),
        compiler_params=_params(("arbitrary",)),
    )(o_f, o_b, p, g_gla.reshape(1, GLA_DV))


def _dft_cos_sin(n):
    k = np.arange(n)
    ang = 2.0 * np.pi * ((k[:, None] * k[None, :]) % n) / n
    return np.cos(ang), np.sin(ang)


def _channel_dft(xb, wd):
    zr, zi = [], []
    for g in range(FNET_GROUPS):
        z = jnp.dot(xb[:, LANES * g:LANES * (g + 1)], wd, preferred_element_type=F32)
        zr.append(z[:, :LANES])
        zi.append(z[:, LANES:])
    return jnp.concatenate(zr, axis=1), jnp.concatenate(zi, axis=1)


def _fft_a_kernel(n1, x_ref, wd_ref, m_ref, bre_ref, bim_ref):
    for n2 in range(FFT_N2):
        xs = x_ref[pl.ds(n2, n1, stride=FFT_N2), :].astype(BF16)
        z = jnp.dot(xs, wd_ref[...], preferred_element_type=F32)
        zs = jnp.concatenate([z[:, :LANES], z[:, LANES:]], axis=0).astype(BF16)
        b = jnp.dot(m_ref[n2], zs, preferred_element_type=F32)
        bre_ref[n2 * n1:(n2 + 1) * n1, :] = b[:n1]
        bim_ref[n2 * n1:(n2 + 1) * n1, :] = b[n1:]


def _fft_b_kernel(n1, scale, c_ref, s_ref, bre_ref, bim_ref, o_ref):
    grp = 8
    for t in range(n1 // grp):
        rows = [pl.ds(grp * t + r, FFT_N2, stride=n1) for r in range(grp)]
        br = jnp.concatenate([bre_ref[rw, :] for rw in rows], axis=1).astype(BF16)
        bi = jnp.concatenate([bim_ref[rw, :] for rw in rows], axis=1).astype(BF16)
        o = (jnp.dot(c_ref[...], br, preferred_element_type=F32)
             + jnp.dot(s_ref[...], bi, preferred_element_type=F32)) * scale
        for r in range(grp):
            o_ref[rows[r], :] = o[:, LANES * r:LANES * (r + 1)]


def _fft_ctx_kernel(scale, x_ref, wd_ref, cs_ref, prev_ref, o_ref):
    del prev_ref
    zr, zi = _channel_dft(x_ref[...].astype(BF16), wd_ref[...])
    z = jnp.concatenate([zr, zi], axis=0).astype(BF16)
    o_ref[...] = jnp.dot(cs_ref[...], z, preferred_element_type=F32) * scale


def _fourier(p, n_lat, n_ctx):
    T = p.shape[0]
    gw = FNET_GROUPS * LANES
    n1 = n_lat // FFT_N2
    cd, sd = _dft_cos_sin(LANES)
    as_bf16 = lambda a: jnp.asarray(a, F32).astype(BF16)
    wd = as_bf16(np.concatenate([cd, -sd], axis=1))
    k1 = jnp.arange(n1, dtype=jnp.int32)[None, :, None]
    pos = (FFT_N2 * jnp.arange(n1, dtype=jnp.int32)[None, None, :]
           + jnp.arange(FFT_N2, dtype=jnp.int32)[:, None, None])
    ang = (2.0 * np.pi / n_lat) * ((k1 * pos) % n_lat).astype(F32)
    gc, gs = jnp.cos(ang), jnp.sin(ang)
    m = jnp.concatenate([jnp.concatenate([gc, gs], axis=2),
                         jnp.concatenate([-gs, gc], axis=2)], axis=1).astype(BF16)
    col = lambda g: (0, g)
    bre, bim = pl.pallas_call(
        functools.partial(_fft_a_kernel, n1),
        grid=(FNET_GROUPS,),
        in_specs=[pl.BlockSpec((n_lat, LANES), lambda g: (0, P_FU // LANES + g)),
                  pl.BlockSpec((LANES, 2 * LANES), lambda g: (0, 0)),
                  pl.BlockSpec((FFT_N2, 2 * n1, 2 * n1), lambda g: (0, 0, 0))],
        out_specs=[pl.BlockSpec((n_lat, LANES), col)] * 2,
        out_shape=[jax.ShapeDtypeStruct((n_lat, gw), F32)] * 2,
        compiler_params=_params(("arbitrary",), 48),
    )(p, wd, m)

    c2, s2 = _dft_cos_sin(FFT_N2)
    mspec = pl.BlockSpec((FFT_N2, FFT_N2), lambda g: (0, 0))
    y = pl.pallas_call(
        functools.partial(_fft_b_kernel, n1, float((n_lat * LANES) ** -0.5)),
        grid=(FNET_GROUPS,),
        in_specs=[mspec, mspec, pl.BlockSpec((n_lat, LANES), col), pl.BlockSpec((n_lat, LANES), col)],
        out_specs=pl.BlockSpec((n_lat, LANES), col),
        out_shape=jax.ShapeDtypeStruct((T, gw), F32),
        compiler_params=_params(("arbitrary",)),
    )(as_bf16(c2), as_bf16(s2), bre, bim)

    cc, sc = _dft_cos_sin(n_ctx)
    return pl.pallas_call(
        functools.partial(_fft_ctx_kernel, float((n_ctx * LANES) ** -0.5)),
        grid=(1,),
        in_specs=[pl.BlockSpec((n_ctx, gw), lambda j: (n_lat // n_ctx, P_FU // gw)),
                  pl.BlockSpec((LANES, 2 * LANES), lambda j: (0, 0)),
                  pl.BlockSpec((n_ctx, 2 * n_ctx), lambda j: (0, 0)),
                  pl.BlockSpec(memory_space=pl.ANY)],
        out_specs=pl.BlockSpec((n_ctx, gw), lambda j: (n_lat // n_ctx, 0)),
        out_shape=jax.ShapeDtypeStruct((T, gw), F32),
        input_output_aliases={3: 0},
        compiler_params=_params(("arbitrary",)),
    )(p, wd, as_bf16(np.concatenate([cc, sc], axis=1)), y)


def _swa_prep_kernel(q_ref, k_ref, v_ref, cos_ref, sa_ref, sb_ref, qo_ref, ko_ref, vo_ref):
    cos, sa, sb = cos_ref[...], sa_ref[...], sb_ref[...]
    for h in range(SWA_HEADS):
        sl = slice(LANES * h, LANES * (h + 1))
        qo_ref[:, sl] = (_rope(q_ref[:, sl], cos, sa, sb, 32) * SWA_SCALE).astype(BF16)
    for h in range(SWA_KV_HEADS):
        sl = slice(LANES * h, LANES * (h + 1))
        ko_ref[:, sl] = _rope(k_ref[:, sl], cos, sa, sb, 32).astype(BF16)
    vo_ref[...] = v_ref[...].astype(BF16)


def _swa_prep(p, tabs):
    T = p.shape[0]
    tm = ROW_TILE
    qw, kw = SWA_HEADS * LANES, SWA_KV_HEADS * LANES
    tab = pl.BlockSpec((tm, LANES), lambda i: (i, 0))
    return pl.pallas_call(
        _swa_prep_kernel,
        grid=(T // tm,),
        in_specs=[pl.BlockSpec((tm, qw), lambda i: (i, P_SQ // qw)),
                  pl.BlockSpec((tm, kw), lambda i: (i, P_SK // kw)),
                  pl.BlockSpec((tm, kw), lambda i: (i, P_SV // kw)),
                  tab, tab, tab],
        out_specs=[pl.BlockSpec((tm, qw), lambda i: (i, 0)),
                   pl.BlockSpec((tm, kw), lambda i: (i, 0)),
                   pl.BlockSpec((tm, kw), lambda i: (i, 0))],
        out_shape=[jax.ShapeDtypeStruct((T, qw), BF16),
                   jax.ShapeDtypeStruct((T, kw), BF16),
                   jax.ShapeDtypeStruct((T, kw), BF16)],
        compiler_params=_params(("arbitrary",)),
    )(p, p, p, *tabs)


def _sink_attend(q, kcat, vcat, mask, sink):
    s = lax.dot_general(q, kcat, _NT, preferred_element_type=F32)
    if mask is not None:
        s = jnp.where(mask, s, -1e30)
    m = jnp.maximum(jnp.max(s, axis=1, keepdims=True), sink)
    pr = jnp.exp((s - m).astype(BF16))
    den = jnp.sum(pr.astype(F32), axis=1, keepdims=True) + jnp.exp(sink - m)
    return jnp.dot(pr, vcat, preferred_element_type=F32) / den


def _swa_lat_kernel(q_ref, kc_ref, kp_ref, k0_ref, kn_ref, vc_ref, vp_ref, v0_ref, vn_ref,
                    sink_ref, o_ref):
    kvh, nb = pl.program_id(0), pl.program_id(1)
    n_ctx = kc_ref.shape[0]
    qb = q_ref.shape[0]
    win = SWA_BLOCK
    kcat = jnp.concatenate([kc_ref[...], kp_ref[...], k0_ref[...], kn_ref[...]], axis=0)
    vcat = jnp.concatenate([vc_ref[...], vp_ref[...], v0_ref[...], vn_ref[...]], axis=0)
    n_keys = n_ctx + qb + 2 * win
    i = lax.broadcasted_iota(jnp.int32, (qb, n_keys), 0)
    j = lax.broadcasted_iota(jnp.int32, (qb, n_keys), 1)
    off = j - (n_ctx + win)
    in_band = jnp.abs(i - off) <= win
    exists = ((off >= 0) | (nb > 0)) & ((off < qb) | (nb < pl.num_programs(1) - 1))
    mask = (j < n_ctx) | (in_band & exists)
    for g in range(SWA_GROUP):
        sl = slice(LANES * g, LANES * (g + 1))
        sink = sink_ref[pl.ds(kvh * SWA_GROUP + g, 1), 0:1]
        o_ref[:, sl] = _sink_attend(q_ref[:, sl], kcat, vcat, mask, sink).astype(BF16)


def _swa_ctx_kernel(q_ref, k_ref, v_ref, sink_ref, prev_ref, o_ref):
    del prev_ref
    kvh = pl.program_id(0)
    for g in range(SWA_GROUP):
        sl = slice(LANES * g, LANES * (g + 1))
        sink = sink_ref[pl.ds(kvh * SWA_GROUP + g, 1), 0:1]
        o_ref[:, sl] = _sink_attend(q_ref[:, sl], k_ref[...], v_ref[...], None, sink).astype(BF16)


def _swa(qs, ks, vs, sink, n_lat, n_ctx):
    T = qs.shape[0]
    win = SWA_BLOCK
    qb = SWA_QBLOCK
    nb = n_lat // qb
    per = qb // win
    gq = SWA_GROUP * LANES
    sink2d = jnp.broadcast_to(sink.astype(F32)[:, None], (SWA_HEADS, LANES))
    cblk = n_lat // n_ctx
    ctx_spec = pl.BlockSpec((n_ctx, LANES), lambda h, b: (cblk, h))
    prev_spec = pl.BlockSpec((win, LANES), lambda h, b: (jnp.maximum(b * per - 1, 0), h))
    cur_spec = pl.BlockSpec((qb, LANES), lambda h, b: (b, h))
    next_spec = pl.BlockSpec((win, LANES), lambda h, b: (jnp.minimum((b + 1) * per, nb * per - 1), h))
    sink_spec = pl.BlockSpec((SWA_HEADS, LANES), lambda h, b: (0, 0))
    o_lat = pl.pallas_call(
        _swa_lat_kernel,
        grid=(SWA_KV_HEADS, nb),
        in_specs=[pl.BlockSpec((qb, gq), lambda h, b: (b, h)),
                  ctx_spec, prev_spec, cur_spec, next_spec,
                  ctx_spec, prev_spec, cur_spec, next_spec, sink_spec],
        out_specs=pl.BlockSpec((qb, gq), lambda h, b: (b, h)),
        out_shape=jax.ShapeDtypeStruct((T, SWA_HEADS * LANES), BF16),
        compiler_params=_params(("arbitrary", "arbitrary")),
    )(qs, ks, ks, ks, ks, vs, vs, vs, vs, sink2d)
    return pl.pallas_call(
        _swa_ctx_kernel,
        grid=(SWA_KV_HEADS,),
        in_specs=[pl.BlockSpec((n_ctx, gq), lambda h: (cblk, h)),
                  pl.BlockSpec((n_ctx, LANES), lambda h: (cblk, h)),
                  pl.BlockSpec((n_ctx, LANES), lambda h: (cblk, h)),
                  pl.BlockSpec((SWA_HEADS, LANES), lambda h: (0, 0)),
                  pl.BlockSpec(memory_space=pl.ANY)],
        out_specs=pl.BlockSpec((n_ctx, gq), lambda h: (cblk, h)),
        out_shape=jax.ShapeDtypeStruct((T, SWA_HEADS * LANES), BF16),
        input_output_aliases={4: 0},
        compiler_params=_params(("arbitrary",)),
    )(qs, ks, vs, sink2d, o_lat)


FFN_HALO = 16


def _ffn_kernel(tm, n_lat, n_tot, h_ref, hp_ref, hn_ref, wg_ref, wa_ref, cwg_ref, cwa_ref,
                cbg_ref, cba_ref, wd_ref, wdl_ref, o_ref, hbuf, ug, ua, act):
    i, j = pl.program_id(0), pl.program_id(1)

    @pl.when(j == 0)
    def _():
        hbuf[0:FFN_HALO, :] = hp_ref[...]
        hbuf[FFN_HALO:FFN_HALO + tm, :] = h_ref[...]
        hbuf[FFN_HALO + tm:, :] = hn_ref[...]
        o_ref[...] = jnp.zeros(o_ref.shape, F32)
        act[...] = jnp.zeros(act.shape, BF16)

    ug[...] = jnp.dot(hbuf[...], wg_ref[...], preferred_element_type=F32)
    ua[...] = jnp.dot(hbuf[...], wa_ref[...], preferred_element_type=F32)
    o_ref[...] += jnp.dot(act[...], wd_ref[...], preferred_element_type=F32)

    r = i * tm + lax.broadcasted_iota(jnp.int32, (tm, 1), 0)
    has_prev = (r != 0) & (r != n_lat)
    has_next = (r != n_lat - 1) & (r != n_tot - 1)

    def conv(u_ref, cw_ref, cb_ref):
        um = jnp.where(has_prev, u_ref[pl.ds(FFN_HALO - 1, tm), :], 0.0)
        u0 = u_ref[pl.ds(FFN_HALO, tm), :]
        up = jnp.where(has_next, u_ref[pl.ds(FFN_HALO + 1, tm), :], 0.0)
        return um * cw_ref[0:1, :] + u0 * cw_ref[1:2, :] + up * cw_ref[2:3, :] + cb_ref[...]

    g = conv(ug, cwg_ref, cbg_ref)
    a = conv(ua, cwa_ref, cba_ref)
    act[...] = (_silu(g) * a).astype(BF16)

    @pl.when(j == pl.num_programs(1) - 1)
    def _():
        o_ref[...] += jnp.dot(act[...], wdl_ref[...], preferred_element_type=F32)


def _ffn(h, n_lat, w_up3, conv_w, conv_b, w_down3, layer, tm, tn):
    T, D = h.shape
    dff = w_down3.shape[1]
    nj = dff // tn
    hb = tm // FFN_HALO
    last_hb = T // FFN_HALO - 1
    cb2 = conv_b.reshape(1, 2 * dff)
    return pl.pallas_call(
        functools.partial(_ffn_kernel, tm, n_lat, T),
        grid=(T // tm, nj),
        in_specs=[pl.BlockSpec((tm, D), lambda i, j: (i, 0), pipeline_mode=pl.Buffered(1)),
                  pl.BlockSpec((FFN_HALO, D), lambda i, j: (jnp.maximum(i * hb - 1, 0), 0)),
                  pl.BlockSpec((FFN_HALO, D), lambda i, j: (jnp.minimum((i + 1) * hb, last_hb), 0)),
                  pl.BlockSpec((None, D, tn), lambda i, j: (layer, 0, j)),
                  pl.BlockSpec((None, D, tn), lambda i, j: (layer, 0, j + nj)),
                  pl.BlockSpec((3, tn), lambda i, j: (0, j)),
                  pl.BlockSpec((3, tn), lambda i, j: (0, j + nj)),
                  pl.BlockSpec((1, tn), lambda i, j: (0, j)),
                  pl.BlockSpec((1, tn), lambda i, j: (0, j + nj)),
                  pl.BlockSpec((None, tn, D), lambda i, j: (layer, jnp.maximum(j - 1, 0), 0)),
                  pl.BlockSpec((None, tn, D), lambda i, j: (layer, nj - 1, 0))],
        out_specs=pl.BlockSpec((tm, D), lambda i, j: (i, 0), pipeline_mode=pl.Buffered(1)),
        out_shape=jax.ShapeDtypeStruct((T, D), F32),
        scratch_shapes=[pltpu.VMEM((tm + 2 * FFN_HALO, D), BF16),
                        pltpu.VMEM((tm + 2 * FFN_HALO, tn), F32),
                        pltpu.VMEM((tm + 2 * FFN_HALO, tn), F32),
                        pltpu.VMEM((tm, tn), BF16)],
        compiler_params=_params(("arbitrary", "arbitrary"), 56),
    )(h, h, h, w_up3, w_up3, conv_w, conv_w, cb2, cb2, w_down3, w_down3)


def _pad_cols(w, width):
    return jnp.pad(w, ((0, 0), (0, width - w.shape[1])))


def _w_in_ext(w_in):
    cq, ckv, kr, gq, gk, gv, gr, glr, fu, sq, sk, sv = jnp.split(
        w_in, np.cumsum([768, 512, 64, 512, 512, 1024, 1024, 32, 1024, 1024, 256]).tolist(), axis=2)
    pad = lambda w: jnp.pad(w, ((0, 0), (0, 0), (0, LANES - w.shape[2])))
    parts = [fu, gv, gr, cq, sk, sq, ckv, gq, gk, sv, pad(kr), pad(glr)]
    return jnp.concatenate(parts, axis=2).astype(BF16)


def _tile_for(total, parts, mult=16):
    t = total // parts
    assert t * parts == total and t % mult == 0, (total, parts)
    return t


def _key_tile(total):
    best = LANES
    for t in range(LANES, 1536 + 1, LANES):
        if total % t == 0:
            best = t
    return best


def kernel(x, c, ctx, c_ctx, w_ada, b_ada, g_pre_mix, g_post_mix, g_pre_ffn, g_post_ffn, w_in, g_qa, w_uq,
           g_kva, w_ukv, w_gate_f, b_gate_f, w_gate_b, b_gate_b, g_gla, swa_sink, w_out, w_up, conv_w, conv_b,
           w_down):
    B, N, D = x.shape
    C = ctx.shape[1]
    T = N + C
    L = w_ada.shape[0]
    dff = w_down.shape[1]
    assert B == 1 and D == 4096 and N % (GRID_W * 16) == 0 and C % ROW_TILE == 0 and N % C == 0
    x2d, ctx2d = x[0], ctx[0]

    cc = jnp.zeros((8, D), F32).at[0].set(c[0]).at[1].set(c_ctx)
    mod = _ada(cc, w_ada, b_ada)
    mla_tabs = _rope_tables(N, C, MLA_ROPE, MLA_ROPE // 4)
    swa_tabs = _rope_tables(N, C, LANES, 32)

    tm_mm = _tile_for(T, 8)
    tm_ffn = _tile_for(T, 8)
    ffn_tn = 256 if dff % 256 == 0 else LANES

    w_in_b = _w_in_ext(w_in)
    w_out_b, w_up_b, w_down_b = w_out.astype(BF16), w_up.astype(BF16), w_down.astype(BF16)
    tq, tk = _tile_for(N, 4), _key_tile(T)

    h = _rms_mod(x2d, ctx2d, g_pre_mix[0], mod[0], 0)
    xs = (x2d, ctx2d)
    for l in range(L):
        last = l == L - 1
        p = _matmul([h], w_in_b, l, tm_mm, 512, F32)

        w_uq_ext = jnp.pad(w_uq[l].reshape(MLA_Q_RANK, MLA_HEADS, MLA_NOPE + MLA_ROPE),
                           ((0, 0), (0, 0), (0, 2 * LANES - MLA_NOPE - MLA_ROPE))
                           ).reshape(MLA_Q_RANK, 2 * LANES * MLA_HEADS).astype(BF16)
        q_a = _mla_q(p, g_qa[l], w_uq_ext, mla_tabs)
        k_a, v_a = _mla_kv(p, g_kva[l], w_ukv[l].astype(BF16), mla_tabs)
        o_a = _flash(q_a, k_a, v_a, None, T, 0, N // tq, 0, T // tk, tq, tk)
        o_a = _flash(q_a, k_a, v_a, o_a, T, N // C, 1, N // C, 1, C, C)

        wf_ext = jnp.zeros((LANES, GLA_HEADS * GLA_DK), F32).at[:GLA_GATE_RANK].set(w_gate_f[l]).astype(BF16)
        wb_ext = jnp.zeros((LANES, GLA_HEADS * GLA_DK), F32).at[GLA_GATE_RANK:2 * GLA_GATE_RANK].set(
            w_gate_b[l]).astype(BF16)
        o_f, o_bk = _gla(p, N, wf_ext, b_gate_f[l], wb_ext, b_gate_b[l])
        o_b = _gla_out(o_f, o_bk, p, g_gla[l])

        o_c = _fourier(p, N, C)

        qs, ks, vs = _swa_prep(p, swa_tabs)
        o_d = _swa(qs, ks, vs, swa_sink[l], N, C)

        y = _matmul([o_a, o_b, o_c, o_d], w_out_b, l, tm_mm, 512, F32)
        x1, h2 = _resid(xs, y, g_post_mix[l], mod[l], 2, N, T, pre=(g_pre_ffn[l], mod[l], 3))
        f = _ffn(h2, N, w_up_b, conv_w[l], conv_b[l], w_down_b, l, tm_ffn, ffn_tn)
        if last:
            (x_out,) = _resid((x1,), f, g_post_ffn[l], mod[l], 5, N, N)
            return x_out.reshape(1, N, D)
        x2, h = _resid((x1,), f, g_post_ffn[l], mod[l], 5, N, T, pre=(g_pre_mix[l + 1], mod[l + 1], 0))
        xs = (x2,)
```

```python
import functools

import numpy as np
import jax
import jax.numpy as jnp
from jax import lax
from jax.experimental import pallas as pl
from jax.experimental.pallas import tpu as pltpu

F32 = jnp.float32
BF16 = jnp.bfloat16

GRID_W = 64
EPS = 1e-6
ROPE_THETA = 10000.0
MLA_HEADS = 8
MLA_Q_RANK = 768
MLA_KV_RANK = 512
MLA_NOPE = 128
MLA_ROPE = 64
MLA_SCALE = (MLA_NOPE + MLA_ROPE) ** -0.5
GLA_HEADS = 4
GLA_DK = 128
GLA_DV = 256
GLA_GATE_RANK = 16
GLA_GATE_TAU = 16.0
GLA_CHUNK = 128
FNET_GROUPS = 8
FFT_N2 = 64
SWA_HEADS = 8
SWA_KV_HEADS = 2
SWA_GROUP = 4
SWA_BLOCK = 128
SWA_QBLOCK = 512
FLASH_SUBTILES = 4
SWA_SCALE = 128 ** -0.5
LANES = 128
ROW_TILE = 256

P_FU, P_GV, P_GR, P_CQ, P_SK, P_SQ = 0, 1024, 2048, 3072, 3840, 4096
P_CKV, P_GQ, P_GK, P_SV, P_KR, P_GLR = 5120, 5632, 6144, 6656, 6912, 7040
P_WIDTH = 7168

_NT = (((1,), (1,)), ((), ()))
_MIB = 1024 * 1024


def _params(sem, vmem_mib=40):
    return pltpu.CompilerParams(dimension_semantics=sem, vmem_limit_bytes=vmem_mib * _MIB)


def _silu(v):
    return v / (1.0 + jnp.exp(-v))


def _rms(v, g):
    return v * lax.rsqrt(jnp.mean(v * v, axis=-1, keepdims=True) + EPS) * g


def _pick(is_ctx, ref):
    return jnp.where(is_ctx, ref[1:2, :], ref[0:1, :])


def _ada_kernel(c_ref, w_ref, b_ref, o_ref):
    a = _silu(c_ref[...]).astype(BF16)
    o_ref[...] = jnp.dot(a, w_ref[...].astype(BF16), preferred_element_type=F32) + b_ref[...]


def _ada(cc, w_ada, b_ada):
    L, D, N6 = w_ada.shape
    tn = 512
    return pl.pallas_call(
        _ada_kernel,
        grid=(L, N6 // tn),
        in_specs=[pl.BlockSpec((8, D), lambda l, j: (0, 0)),
                  pl.BlockSpec((None, D, tn), lambda l, j: (l, 0, j)),
                  pl.BlockSpec((None, 1, tn), lambda l, j: (l, 0, j))],
        out_specs=pl.BlockSpec((None, 8, tn), lambda l, j: (l, 0, j)),
        out_shape=jax.ShapeDtypeStruct((L, 8, N6), F32),
        compiler_params=_params(("arbitrary", "arbitrary")),
    )(cc, w_ada, b_ada.reshape(L, 1, N6))


def _rms_mod_kernel(n_lat_tiles, x_ref, c_ref, g_ref, sh_ref, sc_ref, h_ref):
    is_ctx = pl.program_id(0) >= n_lat_tiles
    xv = jnp.where(is_ctx, c_ref[...], x_ref[...])
    y = _rms(xv, g_ref[...])
    h_ref[...] = (y * (1.0 + _pick(is_ctx, sc_ref)) + _pick(is_ctx, sh_ref)).astype(BF16)


def _rms_mod(x2d, ctx2d, g, mod, k_shift):
    N, D = x2d.shape
    C = ctx2d.shape[0]
    tm = ROW_TILE
    nl, nc = N // tm, C // tm
    return pl.pallas_call(
        functools.partial(_rms_mod_kernel, nl),
        grid=(nl + nc,),
        in_specs=[pl.BlockSpec((tm, D), lambda i: (jnp.minimum(i, nl - 1), 0)),
                  pl.BlockSpec((tm, D), lambda i: (jnp.maximum(i - nl, 0), 0)),
                  pl.BlockSpec((1, D), lambda i: (0, 0)),
                  pl.BlockSpec((8, D), lambda i: (0, k_shift)),
                  pl.BlockSpec((8, D), lambda i: (0, k_shift + 1))],
        out_specs=pl.BlockSpec((tm, D), lambda i: (i, 0)),
        out_shape=jax.ShapeDtypeStruct((N + C, D), BF16),
        compiler_params=_params(("arbitrary",)),
    )(x2d, ctx2d, g.reshape(1, D), mod, mod)


def _resid_kernel(n_lat_tiles, split_x, with_h, *refs):
    refs = list(refs)
    x_ref = refs.pop(0)
    c_ref = refs.pop(0) if split_x else None
    y_ref, gpost_ref, gate_ref = refs.pop(0), refs.pop(0), refs.pop(0)
    if with_h:
        gpre_ref, sh_ref, sc_ref = refs.pop(0), refs.pop(0), refs.pop(0)
    x1_ref = refs.pop(0)
    is_ctx = pl.program_id(0) >= n_lat_tiles
    xv = jnp.where(is_ctx, c_ref[...], x_ref[...]) if split_x else x_ref[...]
    x1 = xv + _pick(is_ctx, gate_ref) * _rms(y_ref[...], gpost_ref[...])
    x1_ref[...] = x1
    if with_h:
        h_ref = refs.pop(0)
        n = _rms(x1, gpre_ref[...])
        h_ref[...] = (n * (1.0 + _pick(is_ctx, sc_ref)) + _pick(is_ctx, sh_ref)).astype(BF16)


def _resid(xs, y, g_post, mod_gate, k_gate, n_lat, n_rows, pre=None):
    D = y.shape[1]
    tm = ROW_TILE
    nl = n_lat // tm
    nt = n_rows // tm
    split_x = len(xs) == 2
    row = lambda i: (i, 0)
    vec = lambda i: (0, 0)
    if split_x:
        in_specs = [pl.BlockSpec((tm, D), lambda i: (jnp.minimum(i, nl - 1), 0)),
                    pl.BlockSpec((tm, D), lambda i: (jnp.maximum(i - nl, 0), 0))]
    else:
        in_specs = [pl.BlockSpec((tm, D), row)]
    in_specs += [pl.BlockSpec((tm, D), row), pl.BlockSpec((1, D), vec),
                 pl.BlockSpec((8, D), lambda i: (0, k_gate))]
    args = list(xs) + [y, g_post.reshape(1, D), mod_gate]
    out_specs = [pl.BlockSpec((tm, D), row)]
    out_shape = [jax.ShapeDtypeStruct((n_rows, D), F32)]
    if pre is not None:
        g_pre, mod_pre, k_shift = pre
        in_specs += [pl.BlockSpec((1, D), vec),
                     pl.BlockSpec((8, D), lambda i: (0, k_shift)),
                     pl.BlockSpec((8, D), lambda i: (0, k_shift + 1))]
        args += [g_pre.reshape(1, D), mod_pre, mod_pre]
        out_specs.append(pl.BlockSpec((tm, D), row))
        out_shape.append(jax.ShapeDtypeStruct((n_rows, D), BF16))
    return pl.pallas_call(
        functools.partial(_resid_kernel, nl, split_x, pre is not None),
        grid=(nt,), in_specs=in_specs, out_specs=out_specs, out_shape=out_shape,
        compiler_params=_params(("arbitrary",)),
    )(*args)


def _mm_kernel(n_in, *refs):
    w_ref, o_ref = refs[n_in], refs[n_in + 1]
    kc = w_ref.shape[0] // n_in
    acc = None
    for k in range(n_in):
        d = jnp.dot(refs[k][...].astype(BF16), w_ref[kc * k:kc * (k + 1), :], preferred_element_type=F32)
        acc = d if acc is None else acc + d
    o_ref[...] = acc.astype(o_ref.dtype)


def _matmul(a_list, w3, layer, tm, tn, out_dtype):
    M = a_list[0].shape[0]
    _, K, Nn = w3.shape
    kc = K // len(a_list)
    return pl.pallas_call(
        functools.partial(_mm_kernel, len(a_list)),
        grid=(M // tm, Nn // tn),
        in_specs=[pl.BlockSpec((tm, kc), lambda i, j: (i, 0)) for _ in a_list]
        + [pl.BlockSpec((None, K, tn), lambda i, j: (layer, 0, j))],
        out_specs=pl.BlockSpec((tm, tn), lambda i, j: (i, j)),
        out_shape=jax.ShapeDtypeStruct((M, Nn), out_dtype),
        compiler_params=_params(("arbitrary", "arbitrary"), 48),
    )(*a_list, w3)


def _rope_tables(n_lat, n_ctx, n_real, half):
    t = jnp.arange(n_lat, dtype=jnp.int32)
    rows = (t // GRID_W).astype(F32)[:, None]
    cols = (t % GRID_W).astype(F32)[:, None]
    lane = np.arange(LANES)
    grp = (lane // (2 * half)) % 2
    jj = lane % (2 * half)
    first = jj < half
    real = lane < n_real
    inv = ROPE_THETA ** (-jnp.asarray(jj % half, F32) / half)
    ang = jnp.where(jnp.asarray(grp == 0)[None, :], rows, cols) * inv[None, :]
    cos, sin = jnp.cos(ang), jnp.sin(ang)
    real_j, first_j = jnp.asarray(real)[None, :], jnp.asarray(first)[None, :]
    cos_t = jnp.where(real_j, cos, 1.0)
    sa_t = jnp.where(real_j & first_j, -sin, 0.0)
    sb_t = jnp.where(real_j & (~first_j), sin, 0.0)
    pad = lambda a, v: jnp.concatenate([a, jnp.full((n_ctx, LANES), v, F32)], axis=0)
    return pad(cos_t, 1.0), pad(sa_t, 0.0), pad(sb_t, 0.0)


def _rope(v, cos, sa, sb, half):
    return v * cos + pltpu.roll(v, LANES - half, 1) * sa + pltpu.roll(v, half, 1) * sb


def _mla_q_kernel(cq_ref, g_ref, w_ref, cos_ref, sa_ref, sb_ref, o_ref):
    n = _rms(cq_ref[...], g_ref[...]).astype(BF16)
    a = jnp.dot(n, w_ref[...], preferred_element_type=F32)
    cos, sa, sb = cos_ref[...], sa_ref[...], sb_ref[...]
    for h in range(MLA_HEADS):
        lo = 2 * LANES * h
        o_ref[:, lo:lo + LANES] = (a[:, lo:lo + LANES] * MLA_SCALE).astype(BF16)
        r = _rope(a[:, lo + LANES:lo + 2 * LANES], cos, sa, sb, MLA_ROPE // 4)
        o_ref[:, lo + LANES:lo + 2 * LANES] = (r * MLA_SCALE).astype(BF16)


def _prep_tile(T):
    return 3 * ROW_TILE if T % (3 * ROW_TILE) == 0 else ROW_TILE


def _mla_q(p, g_qa, w_uq_ext, tabs):
    T = p.shape[0]
    tm = _prep_tile(T)
    tab = pl.BlockSpec((tm, LANES), lambda i: (i, 0))
    return pl.pallas_call(
        _mla_q_kernel,
        grid=(T // tm,),
        in_specs=[pl.BlockSpec((tm, MLA_Q_RANK), lambda i: (i, P_CQ // MLA_Q_RANK)),
                  pl.BlockSpec((1, MLA_Q_RANK), lambda i: (0, 0)),
                  pl.BlockSpec((MLA_Q_RANK, 2 * LANES * MLA_HEADS), lambda i: (0, 0)),
                  tab, tab, tab],
        out_specs=pl.BlockSpec((tm, 2 * LANES * MLA_HEADS), lambda i: (i, 0)),
        out_shape=jax.ShapeDtypeStruct((T, 2 * LANES * MLA_HEADS), BF16),
        compiler_params=_params(("arbitrary",)),
    )(p, g_qa.reshape(1, -1), w_uq_ext, *tabs)


def _mla_kv_kernel(ckv_ref, kr_ref, g_ref, w_ref, cos_ref, sa_ref, sb_ref, k_ref, v_ref):
    n = _rms(ckv_ref[...], g_ref[...]).astype(BF16)
    a = jnp.dot(n, w_ref[...], preferred_element_type=F32)
    kr = _rope(kr_ref[...], cos_ref[...], sa_ref[...], sb_ref[...], MLA_ROPE // 4).astype(BF16)
    ones = jnp.ones(kr.shape, BF16)
    for h in range(MLA_HEADS):
        lo = 2 * LANES * h
        k_ref[:, lo:lo + LANES] = a[:, lo:lo + LANES].astype(BF16)
        k_ref[:, lo + LANES:lo + 2 * LANES] = kr
        v_ref[:, lo:lo + LANES] = a[:, lo + LANES:lo + 2 * LANES].astype(BF16)
        v_ref[:, lo + LANES:lo + 2 * LANES] = ones


def _mla_kv(p, g_kva, w_ukv, tabs):
    T = p.shape[0]
    tm = _prep_tile(T)
    tab = pl.BlockSpec((tm, LANES), lambda i: (i, 0))
    wide = 2 * LANES * MLA_HEADS
    return pl.pallas_call(
        _mla_kv_kernel,
        grid=(T // tm,),
        in_specs=[pl.BlockSpec((tm, MLA_KV_RANK), lambda i: (i, P_CKV // MLA_KV_RANK)),
                  pl.BlockSpec((tm, LANES), lambda i: (i, P_KR // LANES)),
                  pl.BlockSpec((1, MLA_KV_RANK), lambda i: (0, 0)),
                  pl.BlockSpec((MLA_KV_RANK, wide), lambda i: (0, 0)),
                  tab, tab, tab],
        out_specs=[pl.BlockSpec((tm, wide), lambda i: (i, 0)),
                   pl.BlockSpec((tm, wide), lambda i: (i, 0))],
        out_shape=[jax.ShapeDtypeStruct((T, wide), BF16),
                   jax.ShapeDtypeStruct((T, wide), BF16)],
        compiler_params=_params(("arbitrary",)),
    )(p, p, g_kva.reshape(1, -1), w_ukv, *tabs)


def _flash_kernel(q_ref, k_ref, v_ref, *rest):
    o_ref, m_sc, acc_sc = rest[-3:]
    kk = pl.program_id(2)

    @pl.when(kk == 0)
    def _():
        m_sc[...] = jnp.full(m_sc.shape, -1e30, F32)
        acc_sc[...] = jnp.zeros(acc_sc.shape, F32)

    rc = q_ref.shape[0] // FLASH_SUBTILES
    subs = [slice(r * rc, (r + 1) * rc) for r in range(FLASH_SUBTILES)]
    scores = [lax.dot_general(q_ref[rows, :], k_ref[...], _NT, preferred_element_type=F32) for rows in subs]
    for rows, s in zip(subs, scores):
        m_prev = m_sc[rows, :]
        m_new = jnp.maximum(m_prev, jnp.max(s, axis=1, keepdims=True))
        alpha = jnp.exp(m_prev - m_new)
        p = jnp.exp((s - m_new).astype(BF16))
        acc_sc[rows, :] = alpha * acc_sc[rows, :] + jnp.dot(p, v_ref[...], preferred_element_type=F32)
        m_sc[rows, :] = m_new

    @pl.when(kk == pl.num_programs(2) - 1)
    def _():
        acc = acc_sc[...]
        o_ref[...] = (acc[:, :LANES] / acc[:, LANES:LANES + 1]).astype(o_ref.dtype)


def _flash(q, k, v, out_prev, n_rows, q_blk0, n_qblk, k_blk0, n_kblk, tq, tk):
    dq = 2 * LANES
    in_specs = [pl.BlockSpec((tq, dq), lambda h, i, kk: (i + q_blk0, h)),
                pl.BlockSpec((tk, dq), lambda h, i, kk: (kk + k_blk0, h)),
                pl.BlockSpec((tk, dq), lambda h, i, kk: (kk + k_blk0, h))]
    args = [q, k, v]
    aliases = {}
    if out_prev is not None:
        in_specs.append(pl.BlockSpec(memory_space=pl.ANY))
        args.append(out_prev)
        aliases = {3: 0}
    return pl.pallas_call(
        _flash_kernel,
        grid=(MLA_HEADS, n_qblk, n_kblk),
        in_specs=in_specs,
        out_specs=pl.BlockSpec((tq, LANES), lambda h, i, kk: (i + q_blk0, h)),
        out_shape=jax.ShapeDtypeStruct((n_rows, MLA_HEADS * LANES), BF16),
        scratch_shapes=[pltpu.VMEM((tq, 1), F32), pltpu.VMEM((tq, dq), F32)],
        input_output_aliases=aliases,
        compiler_params=_params(("arbitrary", "arbitrary", "arbitrary"), 48),
    )(*args)


def _gla_direction(q, k, v, glr, wg, bg, tri, last_row, st_ref, o_ref):
    z = jnp.dot(glr.astype(BF16), wg, preferred_element_type=F32) + bg
    la = (jnp.minimum(z, 0.0) - jnp.log(1.0 + jnp.exp(-jnp.abs(z)))) * (1.0 / GLA_GATE_TAU)
    ones = jnp.where(tri, 1.0, 0.0).astype(BF16)
    la_hi = la.astype(BF16)
    la_lo = (la - la_hi.astype(F32)).astype(BF16)
    b = (jnp.dot(ones, la_hi, preferred_element_type=F32)
         + jnp.dot(ones, la_lo, preferred_element_type=F32))
    b_tot = b[last_row:last_row + 1, :]
    qt = q * (GLA_DK ** -0.5) * jnp.exp(b)
    kt = k * jnp.exp(-b)
    kh = k * jnp.exp(b_tot - b)
    dec = jnp.exp(b_tot)
    for h in range(GLA_HEADS):
        ks = slice(GLA_DK * h, GLA_DK * (h + 1))
        vs = slice(GLA_DV * h, GLA_DV * (h + 1))
        qh = qt[:, ks].astype(BF16)
        vh = v[:, vs]
        st = st_ref[h]
        inter = lax.dot_general(qh, st.astype(BF16), _NT, preferred_element_type=F32)
        sc = lax.dot_general(qh, kt[:, ks].astype(BF16), _NT, preferred_element_type=F32)
        sc = jnp.where(tri, sc, 0.0)
        intra = jnp.dot(sc.astype(BF16), vh.astype(BF16), preferred_element_type=F32)
        o_ref[:, vs] = inter + intra
        st_ref[h] = st * dec[:, ks] + jnp.dot(vh.T.astype(BF16), kh[:, ks].astype(BF16),
                                               preferred_element_type=F32)


def _gla_kernel(qf, kf, vf, gf, qb, kb, vb, gb, wf, bf, wb, bb, of_ref, ob_ref, stf, stb):
    @pl.when(pl.program_id(0) == 0)
    def _():
        stf[...] = jnp.zeros(stf.shape, F32)
        stb[...] = jnp.zeros(stb.shape, F32)

    cs = qf.shape[0]
    t_i = lax.broadcasted_iota(jnp.int32, (cs, cs), 0)
    s_i = lax.broadcasted_iota(jnp.int32, (cs, cs), 1)
    _gla_direction(qf[...], kf[...], vf[...], gf[...], wf[...], bf[...], s_i <= t_i, cs - 1, stf, of_ref)
    _gla_direction(qb[...], kb[...], vb[...], gb[...], wb[...], bb[...], s_i >= t_i, 0, stb, ob_ref)


def _gla(p, n_lat, wf_ext, bf, wb_ext, bb):
    T = p.shape[0]
    cs = GLA_CHUNK
    nch = T // cs
    nlc = n_lat // cs
    fwd = lambda c: (c + nlc) % nch
    bwd = lambda c: nch - 1 - c
    dk, dv = GLA_HEADS * GLA_DK, GLA_HEADS * GLA_DV

    def specs(idx):
        return [pl.BlockSpec((cs, dk), lambda c: (idx(c), P_GQ // dk)),
                pl.BlockSpec((cs, dk), lambda c: (idx(c), P_GK // dk)),
                pl.BlockSpec((cs, dv), lambda c: (idx(c), P_GV // dv)),
                pl.BlockSpec((cs, LANES), lambda c: (idx(c), P_GLR // LANES))]

    wspec = pl.BlockSpec((LANES, dk), lambda c: (0, 0))
    bspec = pl.BlockSpec((1, dk), lambda c: (0, 0))
    return pl.pallas_call(
        _gla_kernel,
        grid=(nch,),
        in_specs=specs(fwd) + specs(bwd) + [wspec, bspec, wspec, bspec],
        out_specs=[pl.BlockSpec((cs, dv), lambda c: (fwd(c), 0)),
                   pl.BlockSpec((cs, dv), lambda c: (bwd(c), 0))],
        out_shape=[jax.ShapeDtypeStruct((T, dv), F32), jax.ShapeDtypeStruct((T, dv), F32)],
        scratch_shapes=[pltpu.VMEM((GLA_HEADS, GLA_DV, GLA_DK), F32),
                        pltpu.VMEM((GLA_HEADS, GLA_DV, GLA_DK), F32)],
        compiler_params=_params(("arbitrary",)),
    )(p, p, p, p, p, p, p, p, wf_ext, bf.reshape(1, dk), wb_ext, bb.reshape(1, dk))


def _gla_out_kernel(of_ref, ob_ref, gr_ref, g_ref, o_ref):
    o = of_ref[...] + ob_ref[...]
    gr = gr_ref[...]
    for h in range(GLA_HEADS):
        vs = slice(GLA_DV * h, GLA_DV * (h + 1))
        o_ref[:, vs] = (_rms(o[:, vs], g_ref[...]) * _silu(gr[:, vs])).astype(BF16)


def _gla_out(o_f, o_b, p, g_gla):
    T, dv = o_f.shape
    tm = _prep_tile(T)
    row = pl.BlockSpec((tm, dv), lambda i: (i, 0))
    return pl.pallas_call(
        _gla_out_kernel,
        grid=(T // tm,),
        in_specs=[row, row, pl.BlockSpec((tm, dv), lambda i: (i, P_GR // dv)),
                  pl.BlockSpec((1, GLA_DV), lambda i: (0, 0))],
        out_specs=row,
        out_shape=jax.ShapeDtypeStruct((T, dv), BF16),
        compiler_params=_params(("arbitrary",)),
    )(o_f, o_b, p, g_gla.reshape(1, GLA_DV))


def _dft_cos_sin(n):
    k = np.arange(n)
    ang = 2.0 * np.pi * ((k[:, None] * k[None, :]) % n) / n
    return np.cos(ang), np.sin(ang)


def _channel_dft(xb, wd):
    zr, zi = [], []
    for g in range(FNET_GROUPS):
        z = jnp.dot(xb[:, LANES * g:LANES * (g + 1)], wd, preferred_element_type=F32)
        zr.append(z[:, :LANES])
        zi.append(z[:, LANES:])
    return jnp.concatenate(zr, axis=1), jnp.concatenate(zi, axis=1)


def _fft_a_kernel(n1, x_ref, wd_ref, m_ref, bre_ref, bim_ref):
    for n2 in range(FFT_N2):
        xs = x_ref[pl.ds(n2, n1, stride=FFT_N2), :].astype(BF16)
        z = jnp.dot(xs, wd_ref[...], preferred_element_type=F32)
        zs = jnp.concatenate([z[:, :LANES], z[:, LANES:]], axis=0).astype(BF16)
        b = jnp.dot(m_ref[n2], zs, preferred_element_type=F32)
        bre_ref[n2 * n1:(n2 + 1) * n1, :] = b[:n1]
        bim_ref[n2 * n1:(n2 + 1) * n1, :] = b[n1:]


def _fft_b_kernel(n1, scale, c_ref, s_ref, bre_ref, bim_ref, o_ref):
    grp = 8
    for t in range(n1 // grp):
        rows = [pl.ds(grp * t + r, FFT_N2, stride=n1) for r in range(grp)]
        br = jnp.concatenate([bre_ref[rw, :] for rw in rows], axis=1).astype(BF16)
        bi = jnp.concatenate([bim_ref[rw, :] for rw in rows], axis=1).astype(BF16)
        o = (jnp.dot(c_ref[...], br, preferred_element_type=F32)
             + jnp.dot(s_ref[...], bi, preferred_element_type=F32)) * scale
        for r in range(grp):
            o_ref[rows[r], :] = o[:, LANES * r:LANES * (r + 1)]


def _fft_ctx_kernel(scale, x_ref, wd_ref, cs_ref, prev_ref, o_ref):
    del prev_ref
    zr, zi = _channel_dft(x_ref[...].astype(BF16), wd_ref[...])
    z = jnp.concatenate([zr, zi], axis=0).astype(BF16)
    o_ref[...] = jnp.dot(cs_ref[...], z, preferred_element_type=F32) * scale


def _fourier(p, n_lat, n_ctx):
    T = p.shape[0]
    gw = FNET_GROUPS * LANES
    n1 = n_lat // FFT_N2
    cd, sd = _dft_cos_sin(LANES)
    as_bf16 = lambda a: jnp.asarray(a, F32).astype(BF16)
    wd = as_bf16(np.concatenate([cd, -sd], axis=1))
    k1 = jnp.arange(n1, dtype=jnp.int32)[None, :, None]
    pos = (FFT_N2 * jnp.arange(n1, dtype=jnp.int32)[None, None, :]
           + jnp.arange(FFT_N2, dtype=jnp.int32)[:, None, None])
    ang = (2.0 * np.pi / n_lat) * ((k1 * pos) % n_lat).astype(F32)
    gc, gs = jnp.cos(ang), jnp.sin(ang)
    m = jnp.concatenate([jnp.concatenate([gc, gs], axis=2),
                         jnp.concatenate([-gs, gc], axis=2)], axis=1).astype(BF16)
    col = lambda g: (0, g)
    bre, bim = pl.pallas_call(
        functools.partial(_fft_a_kernel, n1),
        grid=(FNET_GROUPS,),
        in_specs=[pl.BlockSpec((n_lat, LANES), lambda g: (0, P_FU // LANES + g)),
                  pl.BlockSpec((LANES, 2 * LANES), lambda g: (0, 0)),
                  pl.BlockSpec((FFT_N2, 2 * n1, 2 * n1), lambda g: (0, 0, 0))],
        out_specs=[pl.BlockSpec((n_lat, LANES), col)] * 2,
        out_shape=[jax.ShapeDtypeStruct((n_lat, gw), F32)] * 2,
        compiler_params=_params(("arbitrary",), 48),
    )(p, wd, m)

    c2, s2 = _dft_cos_sin(FFT_N2)
    mspec = pl.BlockSpec((FFT_N2, FFT_N2), lambda g: (0, 0))
    y = pl.pallas_call(
        functools.partial(_fft_b_kernel, n1, float((n_lat * LANES) ** -0.5)),
        grid=(FNET_GROUPS,),
        in_specs=[mspec, mspec, pl.BlockSpec((n_lat, LANES), col), pl.BlockSpec((n_lat, LANES), col)],
        out_specs=pl.BlockSpec((n_lat, LANES), col),
        out_shape=jax.ShapeDtypeStruct((T, gw), F32),
        compiler_params=_params(("arbitrary",)),
    )(as_bf16(c2), as_bf16(s2), bre, bim)

    cc, sc = _dft_cos_sin(n_ctx)
    return pl.pallas_call(
        functools.partial(_fft_ctx_kernel, float((n_ctx * LANES) ** -0.5)),
        grid=(1,),
        in_specs=[pl.BlockSpec((n_ctx, gw), lambda j: (n_lat // n_ctx, P_FU // gw)),
                  pl.BlockSpec((LANES, 2 * LANES), lambda j: (0, 0)),
                  pl.BlockSpec((n_ctx, 2 * n_ctx), lambda j: (0, 0)),
                  pl.BlockSpec(memory_space=pl.ANY)],
        out_specs=pl.BlockSpec((n_ctx, gw), lambda j: (n_lat // n_ctx, 0)),
        out_shape=jax.ShapeDtypeStruct((T, gw), F32),
        input_output_aliases={3: 0},
        compiler_params=_params(("arbitrary",)),
    )(p, wd, as_bf16(np.concatenate([cc, sc], axis=1)), y)


def _swa_prep_kernel(q_ref, k_ref, v_ref, cos_ref, sa_ref, sb_ref, qo_ref, ko_ref, vo_ref):
    cos, sa, sb = cos_ref[...], sa_ref[...], sb_ref[...]
    for h in range(SWA_HEADS):
        sl = slice(LANES * h, LANES * (h + 1))
        qo_ref[:, sl] = (_rope(q_ref[:, sl], cos, sa, sb, 32) * SWA_SCALE).astype(BF16)
    for h in range(SWA_KV_HEADS):
        sl = slice(LANES * h, LANES * (h + 1))
        ko_ref[:, sl] = _rope(k_ref[:, sl], cos, sa, sb, 32).astype(BF16)
    vo_ref[...] = v_ref[...].astype(BF16)


def _swa_prep(p, tabs):
    T = p.shape[0]
    tm = _prep_tile(T)
    qw, kw = SWA_HEADS * LANES, SWA_KV_HEADS * LANES
    tab = pl.BlockSpec((tm, LANES), lambda i: (i, 0))
    return pl.pallas_call(
        _swa_prep_kernel,
        grid=(T // tm,),
        in_specs=[pl.BlockSpec((tm, qw), lambda i: (i, P_SQ // qw)),
                  pl.BlockSpec((tm, kw), lambda i: (i, P_SK // kw)),
                  pl.BlockSpec((tm, kw), lambda i: (i, P_SV // kw)),
                  tab, tab, tab],
        out_specs=[pl.BlockSpec((tm, qw), lambda i: (i, 0)),
                   pl.BlockSpec((tm, kw), lambda i: (i, 0)),
                   pl.BlockSpec((tm, kw), lambda i: (i, 0))],
        out_shape=[jax.ShapeDtypeStruct((T, qw), BF16),
                   jax.ShapeDtypeStruct((T, kw), BF16),
                   jax.ShapeDtypeStruct((T, kw), BF16)],
        compiler_params=_params(("arbitrary",)),
    )(p, p, p, *tabs)


def _sink_attend(q, kcat, vcat, mask, sink):
    s = lax.dot_general(q, kcat, _NT, preferred_element_type=F32)
    if mask is not None:
        s = jnp.where(mask, s, -1e30)
    m = jnp.maximum(jnp.max(s, axis=1, keepdims=True), sink)
    pr = jnp.exp((s - m).astype(BF16))
    den = jnp.sum(pr.astype(F32), axis=1, keepdims=True) + jnp.exp(sink - m)
    return jnp.dot(pr, vcat, preferred_element_type=F32) / den


def _swa_lat_kernel(q_ref, kc_ref, kp_ref, k0_ref, kn_ref, vc_ref, vp_ref, v0_ref, vn_ref,
                    sink_ref, o_ref):
    kvh, nb = pl.program_id(0), pl.program_id(1)
    n_ctx = kc_ref.shape[0]
    qb = q_ref.shape[0]
    win = SWA_BLOCK
    kcat = jnp.concatenate([kc_ref[...], kp_ref[...], k0_ref[...], kn_ref[...]], axis=0)
    vcat = jnp.concatenate([vc_ref[...], vp_ref[...], v0_ref[...], vn_ref[...]], axis=0)
    n_keys = n_ctx + qb + 2 * win
    i = lax.broadcasted_iota(jnp.int32, (qb, n_keys), 0)
    j = lax.broadcasted_iota(jnp.int32, (qb, n_keys), 1)
    off = j - (n_ctx + win)
    in_band = jnp.abs(i - off) <= win
    exists = ((off >= 0) | (nb > 0)) & ((off < qb) | (nb < pl.num_programs(1) - 1))
    mask = (j < n_ctx) | (in_band & exists)
    for g in range(SWA_GROUP):
        sl = slice(LANES * g, LANES * (g + 1))
        sink = sink_ref[pl.ds(kvh * SWA_GROUP + g, 1), 0:1]
        o_ref[:, sl] = _sink_attend(q_ref[:, sl], kcat, vcat, mask, sink).astype(BF16)


def _swa_ctx_kernel(q_ref, k_ref, v_ref, sink_ref, prev_ref, o_ref):
    del prev_ref
    kvh = pl.program_id(0)
    for g in range(SWA_GROUP):
        sl = slice(LANES * g, LANES * (g + 1))
        sink = sink_ref[pl.ds(kvh * SWA_GROUP + g, 1), 0:1]
        o_ref[:, sl] = _sink_attend(q_ref[:, sl], k_ref[...], v_ref[...], None, sink).astype(BF16)


def _swa(qs, ks, vs, sink, n_lat, n_ctx):
    T = qs.shape[0]
    win = SWA_BLOCK
    qb = SWA_QBLOCK
    nb = n_lat // qb
    per = qb // win
    gq = SWA_GROUP * LANES
    sink2d = jnp.broadcast_to(sink.astype(F32)[:, None], (SWA_HEADS, LANES))
    cblk = n_lat // n_ctx
    ctx_spec = pl.BlockSpec((n_ctx, LANES), lambda h, b: (cblk, h))
    prev_spec = pl.BlockSpec((win, LANES), lambda h, b: (jnp.maximum(b * per - 1, 0), h))
    cur_spec = pl.BlockSpec((qb, LANES), lambda h, b: (b, h))
    next_spec = pl.BlockSpec((win, LANES), lambda h, b: (jnp.minimum((b + 1) * per, nb * per - 1), h))
    sink_spec = pl.BlockSpec((SWA_HEADS, LANES), lambda h, b: (0, 0))
    o_lat = pl.pallas_call(
        _swa_lat_kernel,
        grid=(SWA_KV_HEADS, nb),
        in_specs=[pl.BlockSpec((qb, gq), lambda h, b: (b, h)),
                  ctx_spec, prev_spec, cur_spec, next_spec,
                  ctx_spec, prev_spec, cur_spec, next_spec, sink_spec],
        out_specs=pl.BlockSpec((qb, gq), lambda h, b: (b, h)),
        out_shape=jax.ShapeDtypeStruct((T, SWA_HEADS * LANES), BF16),
        compiler_params=_params(("arbitrary", "arbitrary")),
    )(qs, ks, ks, ks, ks, vs, vs, vs, vs, sink2d)
    return pl.pallas_call(
        _swa_ctx_kernel,
        grid=(SWA_KV_HEADS,),
        in_specs=[pl.BlockSpec((n_ctx, gq), lambda h: (cblk, h)),
                  pl.BlockSpec((n_ctx, LANES), lambda h: (cblk, h)),
                  pl.BlockSpec((n_ctx, LANES), lambda h: (cblk, h)),
                  pl.BlockSpec((SWA_HEADS, LANES), lambda h: (0, 0)),
                  pl.BlockSpec(memory_space=pl.ANY)],
        out_specs=pl.BlockSpec((n_ctx, gq), lambda h: (cblk, h)),
        out_shape=jax.ShapeDtypeStruct((T, SWA_HEADS * LANES), BF16),
        input_output_aliases={4: 0},
        compiler_params=_params(("arbitrary",)),
    )(qs, ks, vs, sink2d, o_lat)


FFN_HALO = 16
FFN_CONV_CHUNKS = 16
FFN_CHUNKS_EARLY = 6
FFN_DOWN_COLS = 256


def _ffn_kernel(tm, nj, n_lat, n_tot, h_ref, hp_ref, hn_ref, wg0_ref, wa0_ref, wg_ref, wa_ref, cwg_ref, cwa_ref,
                cbg_ref, cba_ref, wd_ref, wdl_ref, o_ref, hbuf, ug0, ua0, ug1, ua1, act0, act1):
    i, j = pl.program_id(0), pl.program_id(1)

    @pl.when(j == 0)
    def _():
        hbuf[0:FFN_HALO, :] = hp_ref[...]
        hbuf[FFN_HALO:FFN_HALO + tm, :] = h_ref[...]
        hbuf[FFN_HALO + tm:, :] = hn_ref[...]
        o_ref[...] = jnp.zeros(o_ref.shape, F32)
        act1[...] = jnp.zeros(act1.shape, BF16)
        ug0[...] = jnp.dot(hbuf[...], wg0_ref[...], preferred_element_type=F32)
        ua0[...] = jnp.dot(hbuf[...], wa0_ref[...], preferred_element_type=F32)

    rc = tm // FFN_CONV_CHUNKS
    n_col = o_ref.shape[1] // FFN_DOWN_COLS

    def conv(u_ref, cw_ref, cb_ref, row0, has_prev, has_next):
        um = jnp.where(has_prev, u_ref[pl.ds(FFN_HALO - 1 + row0, rc), :], 0.0)
        u0 = u_ref[pl.ds(FFN_HALO + row0, rc), :]
        up = jnp.where(has_next, u_ref[pl.ds(FFN_HALO + 1 + row0, rc), :], 0.0)
        return um * cw_ref[0:1, :] + u0 * cw_ref[1:2, :] + up * cw_ref[2:3, :] + cb_ref[...]

    def step(ug_cur, ua_cur, act_cur, ug_nxt, ua_nxt, act_prev):
        def gate_chunk(k):
            row0 = rc * k
            r = i * tm + row0 + lax.broadcasted_iota(jnp.int32, (rc, 1), 0)
            has_prev = (r != 0) & (r != n_lat)
            has_next = (r != n_lat - 1) & (r != n_tot - 1)
            g = conv(ug_cur, cwg_ref, cbg_ref, row0, has_prev, has_next)
            a = conv(ua_cur, cwa_ref, cba_ref, row0, has_prev, has_next)
            act_cur[row0:row0 + rc, :] = (_silu(g) * a).astype(BF16)

        ug_nxt[...] = jnp.dot(hbuf[...], wg_ref[...], preferred_element_type=F32)
        for k in range(FFN_CHUNKS_EARLY):
            gate_chunk(k)
        ua_nxt[...] = jnp.dot(hbuf[...], wa_ref[...], preferred_element_type=F32)
        late = FFN_CONV_CHUNKS - FFN_CHUNKS_EARLY
        for c in range(n_col):
            cols = slice(FFN_DOWN_COLS * c, FFN_DOWN_COLS * (c + 1))
            o_ref[:, cols] += jnp.dot(act_prev[...], wd_ref[:, cols], preferred_element_type=F32)
            if c * late // n_col != (c + 1) * late // n_col:
                gate_chunk(FFN_CHUNKS_EARLY + c * late // n_col)

    pl.when(j % 2 == 0)(lambda: step(ug0, ua0, act0, ug1, ua1, act1))
    pl.when(j % 2 == 1)(lambda: step(ug1, ua1, act1, ug0, ua0, act0))

    @pl.when(j == nj - 1)
    def _():
        last = act0 if (nj - 1) % 2 == 0 else act1
        o_ref[...] += jnp.dot(last[...], wdl_ref[...], preferred_element_type=F32)


def _ffn(h, n_lat, w_up3, conv_w, conv_b, w_down3, layer, tm, tn):
    T, D = h.shape
    dff = w_down3.shape[1]
    nj = dff // tn
    hb = tm // FFN_HALO
    last_hb = T // FFN_HALO - 1
    cb2 = conv_b.reshape(1, 2 * dff)
    ahead = lambda j: jnp.minimum(j + 1, nj - 1)
    once = pl.Buffered(1)
    return pl.pallas_call(
        functools.partial(_ffn_kernel, tm, nj, n_lat, T),
        grid=(T // tm, nj),
        in_specs=[pl.BlockSpec((tm, D), lambda i, j: (i, 0), pipeline_mode=once),
                  pl.BlockSpec((FFN_HALO, D), lambda i, j: (jnp.maximum(i * hb - 1, 0), 0)),
                  pl.BlockSpec((FFN_HALO, D), lambda i, j: (jnp.minimum((i + 1) * hb, last_hb), 0)),
                  pl.BlockSpec((None, D, tn), lambda i, j: (layer, 0, 0), pipeline_mode=once),
                  pl.BlockSpec((None, D, tn), lambda i, j: (layer, 0, nj), pipeline_mode=once),
                  pl.BlockSpec((None, D, tn), lambda i, j: (layer, 0, ahead(j))),
                  pl.BlockSpec((None, D, tn), lambda i, j: (layer, 0, ahead(j) + nj)),
                  pl.BlockSpec((3, tn), lambda i, j: (0, j)),
                  pl.BlockSpec((3, tn), lambda i, j: (0, j + nj)),
                  pl.BlockSpec((1, tn), lambda i, j: (0, j)),
                  pl.BlockSpec((1, tn), lambda i, j: (0, j + nj)),
                  pl.BlockSpec((None, tn, D), lambda i, j: (layer, jnp.maximum(j - 1, 0), 0)),
                  pl.BlockSpec((None, tn, D), lambda i, j: (layer, nj - 1, 0), pipeline_mode=once)],
        out_specs=pl.BlockSpec((tm, D), lambda i, j: (i, 0), pipeline_mode=once),
        out_shape=jax.ShapeDtypeStruct((T, D), F32),
        scratch_shapes=[pltpu.VMEM((tm + 2 * FFN_HALO, D), BF16)]
        + [pltpu.VMEM((tm + 2 * FFN_HALO, tn), F32)] * 4 + [pltpu.VMEM((tm, tn), BF16)] * 2,
        compiler_params=_params(("arbitrary", "arbitrary"), 56),
    )(h, h, h, w_up3, w_up3, w_up3, w_up3, conv_w, conv_w, cb2, cb2, w_down3, w_down3)


def _pad_cols(w, width):
    return jnp.pad(w, ((0, 0), (0, width - w.shape[1])))


def _w_in_ext(w_in):
    cq, ckv, kr, gq, gk, gv, gr, glr, fu, sq, sk, sv = jnp.split(
        w_in, np.cumsum([768, 512, 64, 512, 512, 1024, 1024, 32, 1024, 1024, 256]).tolist(), axis=2)
    pad = lambda w: jnp.pad(w, ((0, 0), (0, 0), (0, LANES - w.shape[2])))
    parts = [fu, gv, gr, cq, sk, sq, ckv, gq, gk, sv, pad(kr), pad(glr)]
    return jnp.concatenate(parts, axis=2).astype(BF16)


def _tile_for(total, parts, mult=16):
    t = total // parts
    assert t * parts == total and t % mult == 0, (total, parts)
    return t


def _key_tile(total):
    best = LANES
    for t in range(LANES, 1536 + 1, LANES):
        if total % t == 0:
            best = t
    return best


def kernel(x, c, ctx, c_ctx, w_ada, b_ada, g_pre_mix, g_post_mix, g_pre_ffn, g_post_ffn, w_in, g_qa, w_uq,
           g_kva, w_ukv, w_gate_f, b_gate_f, w_gate_b, b_gate_b, g_gla, swa_sink, w_out, w_up, conv_w, conv_b,
           w_down):
    B, N, D = x.shape
    C = ctx.shape[1]
    T = N + C
    L = w_ada.shape[0]
    dff = w_down.shape[1]
    assert B == 1 and D == 4096 and N % (GRID_W * 16) == 0 and C % ROW_TILE == 0 and N % C == 0
    x2d, ctx2d = x[0], ctx[0]

    cc = jnp.zeros((8, D), F32).at[0].set(c[0]).at[1].set(c_ctx)
    mod = _ada(cc, w_ada, b_ada)
    mla_tabs = _rope_tables(N, C, MLA_ROPE, MLA_ROPE // 4)
    swa_tabs = _rope_tables(N, C, LANES, 32)

    tm_mm = _tile_for(T, 8)
    tm_ffn = _tile_for(T, 11, 256) if T % (11 * 256) == 0 else _tile_for(T, 5, 256)
    ffn_tn = 256 if dff % 256 == 0 else LANES

    w_in_b = _w_in_ext(w_in)
    w_out_b, w_up_b, w_down_b = w_out.astype(BF16), w_up.astype(BF16), w_down.astype(BF16)
    tq, tk = _tile_for(N, 4), _key_tile(T)

    h = _rms_mod(x2d, ctx2d, g_pre_mix[0], mod[0], 0)
    xs = (x2d, ctx2d)
    for l in range(L):
        last = l == L - 1
        p = _matmul([h], w_in_b, l, tm_mm, 512, F32)

        w_uq_ext = jnp.pad(w_uq[l].reshape(MLA_Q_RANK, MLA_HEADS, MLA_NOPE + MLA_ROPE),
                           ((0, 0), (0, 0), (0, 2 * LANES - MLA_NOPE - MLA_ROPE))
                           ).reshape(MLA_Q_RANK, 2 * LANES * MLA_HEADS).astype(BF16)
        q_a = _mla_q(p, g_qa[l], w_uq_ext, mla_tabs)
        k_a, v_a = _mla_kv(p, g_kva[l], w_ukv[l].astype(BF16), mla_tabs)
        o_a = _flash(q_a, k_a, v_a, None, T, 0, N // tq, 0, T // tk, tq, tk)
        o_a = _flash(q_a, k_a, v_a, o_a, T, N // C, 1, N // C, 1, C, C)

        wf_ext = jnp.zeros((LANES, GLA_HEADS * GLA_DK), F32).at[:GLA_GATE_RANK].set(w_gate_f[l]).astype(BF16)
        wb_ext = jnp.zeros((LANES, GLA_HEADS * GLA_DK), F32).at[GLA_GATE_RANK:2 * GLA_GATE_RANK].set(
            w_gate_b[l]).astype(BF16)
        o_f, o_bk = _gla(p, N, wf_ext, b_gate_f[l], wb_ext, b_gate_b[l])
        o_b = _gla_out(o_f, o_bk, p, g_gla[l])

        o_c = _fourier(p, N, C)

        qs, ks, vs = _swa_prep(p, swa_tabs)
        o_d = _swa(qs, ks, vs, swa_sink[l], N, C)

        y = _matmul([o_a, o_b, o_c, o_d], w_out_b, l, tm_mm, 512, F32)
        x1, h2 = _resid(xs, y, g_post_mix[l], mod[l], 2, N, T, pre=(g_pre_ffn[l], mod[l], 3))
        f = _ffn(h2, N, w_up_b, conv_w[l], conv_b[l], w_down_b, l, tm_ffn, ffn_tn)
        if last:
            (x_out,) = _resid((x1,), f, g_post_ffn[l], mod[l], 5, N, N)
            return x_out.reshape(1, N, D)
        x2, h = _resid((x1,), f, g_post_ffn[l], mod[l], 5, N, T, pre=(g_pre_mix[l + 1], mod[l + 1], 0))
        xs = (x2,)
```

```python
import functools

import numpy as np
import jax
import jax.numpy as jnp
from jax import lax
from jax.experimental import pallas as pl
from jax.experimental.pallas import tpu as pltpu

F32 = jnp.float32
BF16 = jnp.bfloat16

GRID_W = 64
EPS = 1e-6
ROPE_THETA = 10000.0
MLA_HEADS = 8
MLA_Q_RANK = 768
MLA_KV_RANK = 512
MLA_NOPE = 128
MLA_ROPE = 64
MLA_SCALE = (MLA_NOPE + MLA_ROPE) ** -0.5
GLA_HEADS = 4
GLA_DK = 128
GLA_DV = 256
GLA_GATE_RANK = 16
GLA_GATE_TAU = 16.0
GLA_CHUNK = 128
FNET_GROUPS = 8
FFT_N2 = 64
SWA_HEADS = 8
SWA_KV_HEADS = 2
SWA_GROUP = 4
SWA_BLOCK = 128
SWA_QBLOCK = 512
FLASH_SUBTILES = 4
SWA_SCALE = 128 ** -0.5
LANES = 128
ROW_TILE = 256

P_FU, P_GV, P_GR, P_CQ, P_SK, P_SQ = 0, 1024, 2048, 3072, 3840, 4096
P_CKV, P_GQ, P_GK, P_SV, P_KR, P_GLR = 5120, 5632, 6144, 6656, 6912, 7040
P_WIDTH = 7168

_NT = (((1,), (1,)), ((), ()))
_MIB = 1024 * 1024


def _params(sem, vmem_mib=40):
    return pltpu.CompilerParams(dimension_semantics=sem, vmem_limit_bytes=vmem_mib * _MIB)


def _silu(v):
    return v / (1.0 + jnp.exp(-v))


def _rms(v, g):
    return v * lax.rsqrt(jnp.mean(v * v, axis=-1, keepdims=True) + EPS) * g


def _pick(is_ctx, ref):
    return jnp.where(is_ctx, ref[1:2, :], ref[0:1, :])


def _ada_kernel(c_ref, w_ref, b_ref, o_ref):
    a = _silu(c_ref[...]).astype(BF16)
    o_ref[...] = jnp.dot(a, w_ref[...].astype(BF16), preferred_element_type=F32) + b_ref[...]


def _ada(cc, w_ada, b_ada):
    L, D, N6 = w_ada.shape
    tn = 512
    return pl.pallas_call(
        _ada_kernel,
        grid=(L, N6 // tn),
        in_specs=[pl.BlockSpec((8, D), lambda l, j: (0, 0)),
                  pl.BlockSpec((None, D, tn), lambda l, j: (l, 0, j)),
                  pl.BlockSpec((None, 1, tn), lambda l, j: (l, 0, j))],
        out_specs=pl.BlockSpec((None, 8, tn), lambda l, j: (l, 0, j)),
        out_shape=jax.ShapeDtypeStruct((L, 8, N6), F32),
        compiler_params=_params(("arbitrary", "arbitrary")),
    )(cc, w_ada, b_ada.reshape(L, 1, N6))


def _rms_mod_kernel(n_lat_tiles, x_ref, c_ref, g_ref, sh_ref, sc_ref, h_ref):
    is_ctx = pl.program_id(0) >= n_lat_tiles
    xv = jnp.where(is_ctx, c_ref[...], x_ref[...])
    y = _rms(xv, g_ref[...])
    h_ref[...] = (y * (1.0 + _pick(is_ctx, sc_ref)) + _pick(is_ctx, sh_ref)).astype(BF16)


def _rms_mod(x2d, ctx2d, g, mod, k_shift):
    N, D = x2d.shape
    C = ctx2d.shape[0]
    tm = ROW_TILE
    nl, nc = N // tm, C // tm
    return pl.pallas_call(
        functools.partial(_rms_mod_kernel, nl),
        grid=(nl + nc,),
        in_specs=[pl.BlockSpec((tm, D), lambda i: (jnp.minimum(i, nl - 1), 0)),
                  pl.BlockSpec((tm, D), lambda i: (jnp.maximum(i - nl, 0), 0)),
                  pl.BlockSpec((1, D), lambda i: (0, 0)),
                  pl.BlockSpec((8, D), lambda i: (0, k_shift)),
                  pl.BlockSpec((8, D), lambda i: (0, k_shift + 1))],
        out_specs=pl.BlockSpec((tm, D), lambda i: (i, 0)),
        out_shape=jax.ShapeDtypeStruct((N + C, D), BF16),
        compiler_params=_params(("arbitrary",)),
    )(x2d, ctx2d, g.reshape(1, D), mod, mod)


def _resid_kernel(n_lat_tiles, split_x, with_h, *refs):
    refs = list(refs)
    x_ref = refs.pop(0)
    c_ref = refs.pop(0) if split_x else None
    y_ref, gpost_ref, gate_ref = refs.pop(0), refs.pop(0), refs.pop(0)
    if with_h:
        gpre_ref, sh_ref, sc_ref = refs.pop(0), refs.pop(0), refs.pop(0)
    x1_ref = refs.pop(0)
    is_ctx = pl.program_id(0) >= n_lat_tiles
    xv = jnp.where(is_ctx, c_ref[...], x_ref[...]) if split_x else x_ref[...]
    x1 = xv + _pick(is_ctx, gate_ref) * _rms(y_ref[...], gpost_ref[...])
    x1_ref[...] = x1
    if with_h:
        h_ref = refs.pop(0)
        n = _rms(x1, gpre_ref[...])
        h_ref[...] = (n * (1.0 + _pick(is_ctx, sc_ref)) + _pick(is_ctx, sh_ref)).astype(BF16)


def _resid(xs, y, g_post, mod_gate, k_gate, n_lat, n_rows, pre=None):
    D = y.shape[1]
    tm = ROW_TILE
    nl = n_lat // tm
    nt = n_rows // tm
    split_x = len(xs) == 2
    row = lambda i: (i, 0)
    vec = lambda i: (0, 0)
    if split_x:
        in_specs = [pl.BlockSpec((tm, D), lambda i: (jnp.minimum(i, nl - 1), 0)),
                    pl.BlockSpec((tm, D), lambda i: (jnp.maximum(i - nl, 0), 0))]
    else:
        in_specs = [pl.BlockSpec((tm, D), row)]
    in_specs += [pl.BlockSpec((tm, D), row), pl.BlockSpec((1, D), vec),
                 pl.BlockSpec((8, D), lambda i: (0, k_gate))]
    args = list(xs) + [y, g_post.reshape(1, D), mod_gate]
    out_specs = [pl.BlockSpec((tm, D), row)]
    out_shape = [jax.ShapeDtypeStruct((n_rows, D), F32)]
    if pre is not None:
        g_pre, mod_pre, k_shift = pre
        in_specs += [pl.BlockSpec((1, D), vec),
                     pl.BlockSpec((8, D), lambda i: (0, k_shift)),
                     pl.BlockSpec((8, D), lambda i: (0, k_shift + 1))]
        args += [g_pre.reshape(1, D), mod_pre, mod_pre]
        out_specs.append(pl.BlockSpec((tm, D), row))
        out_shape.append(jax.ShapeDtypeStruct((n_rows, D), BF16))
    return pl.pallas_call(
        functools.partial(_resid_kernel, nl, split_x, pre is not None),
        grid=(nt,), in_specs=in_specs, out_specs=out_specs, out_shape=out_shape,
        compiler_params=_params(("arbitrary",)),
    )(*args)


def _mm_kernel(n_in, *refs):
    w_ref, o_ref = refs[n_in], refs[n_in + 1]
    kc = w_ref.shape[0] // n_in
    acc = None
    for k in range(n_in):
        d = jnp.dot(refs[k][...].astype(BF16), w_ref[kc * k:kc * (k + 1), :].astype(BF16),
                    preferred_element_type=F32)
        acc = d if acc is None else acc + d
    o_ref[...] = acc.astype(o_ref.dtype)


def _matmul(a_list, w3, layer, tm, tn, out_dtype):
    M = a_list[0].shape[0]
    _, K, Nn = w3.shape
    kc = K // len(a_list)
    return pl.pallas_call(
        functools.partial(_mm_kernel, len(a_list)),
        grid=(M // tm, Nn // tn),
        in_specs=[pl.BlockSpec((tm, kc), lambda i, j: (i, 0)) for _ in a_list]
        + [pl.BlockSpec((None, K, tn), lambda i, j: (layer, 0, j))],
        out_specs=pl.BlockSpec((tm, tn), lambda i, j: (i, j)),
        out_shape=jax.ShapeDtypeStruct((M, Nn), out_dtype),
        compiler_params=_params(("arbitrary", "arbitrary"), 56),
    )(*a_list, w3)


def _rope_tables(n_lat, n_ctx, n_real, half):
    t = jnp.arange(n_lat, dtype=jnp.int32)
    rows = (t // GRID_W).astype(F32)[:, None]
    cols = (t % GRID_W).astype(F32)[:, None]
    lane = np.arange(LANES)
    grp = (lane // (2 * half)) % 2
    jj = lane % (2 * half)
    first = jj < half
    real = lane < n_real
    inv = ROPE_THETA ** (-jnp.asarray(jj % half, F32) / half)
    ang = jnp.where(jnp.asarray(grp == 0)[None, :], rows, cols) * inv[None, :]
    cos, sin = jnp.cos(ang), jnp.sin(ang)
    real_j, first_j = jnp.asarray(real)[None, :], jnp.asarray(first)[None, :]
    cos_t = jnp.where(real_j, cos, 1.0)
    sa_t = jnp.where(real_j & first_j, -sin, 0.0)
    sb_t = jnp.where(real_j & (~first_j), sin, 0.0)
    pad = lambda a, v: jnp.concatenate([a, jnp.full((n_ctx, LANES), v, F32)], axis=0)
    return pad(cos_t, 1.0), pad(sa_t, 0.0), pad(sb_t, 0.0)


def _rope(v, cos, sa, sb, half):
    return v * cos + pltpu.roll(v, LANES - half, 1) * sa + pltpu.roll(v, half, 1) * sb


def _mla_q_kernel(cq_ref, g_ref, w_ref, cos_ref, sa_ref, sb_ref, o_ref):
    n = _rms(cq_ref[...], g_ref[...]).astype(BF16)
    a = jnp.dot(n, w_ref[...], preferred_element_type=F32)
    cos, sa, sb = cos_ref[...], sa_ref[...], sb_ref[...]
    for h in range(MLA_HEADS):
        lo = 2 * LANES * h
        o_ref[:, lo:lo + LANES] = (a[:, lo:lo + LANES] * MLA_SCALE).astype(BF16)
        r = _rope(a[:, lo + LANES:lo + 2 * LANES], cos, sa, sb, MLA_ROPE // 4)
        o_ref[:, lo + LANES:lo + 2 * LANES] = (r * MLA_SCALE).astype(BF16)


def _prep_tile(T):
    return 3 * ROW_TILE if T % (3 * ROW_TILE) == 0 else ROW_TILE


def _mla_q(p, g_qa, w_uq_ext, tabs):
    T = p.shape[0]
    tm = _prep_tile(T)
    tab = pl.BlockSpec((tm, LANES), lambda i: (i, 0))
    return pl.pallas_call(
        _mla_q_kernel,
        grid=(T // tm,),
        in_specs=[pl.BlockSpec((tm, MLA_Q_RANK), lambda i: (i, P_CQ // MLA_Q_RANK)),
                  pl.BlockSpec((1, MLA_Q_RANK), lambda i: (0, 0)),
                  pl.BlockSpec((MLA_Q_RANK, 2 * LANES * MLA_HEADS), lambda i: (0, 0)),
                  tab, tab, tab],
        out_specs=pl.BlockSpec((tm, 2 * LANES * MLA_HEADS), lambda i: (i, 0)),
        out_shape=jax.ShapeDtypeStruct((T, 2 * LANES * MLA_HEADS), BF16),
        compiler_params=_params(("arbitrary",)),
    )(p, g_qa.reshape(1, -1), w_uq_ext, *tabs)


def _mla_kv_kernel(ckv_ref, kr_ref, g_ref, w_ref, cos_ref, sa_ref, sb_ref, k_ref, v_ref):
    n = _rms(ckv_ref[...], g_ref[...]).astype(BF16)
    a = jnp.dot(n, w_ref[...], preferred_element_type=F32)
    kr = _rope(kr_ref[...], cos_ref[...], sa_ref[...], sb_ref[...], MLA_ROPE // 4).astype(BF16)
    ones = jnp.ones(kr.shape, BF16)
    for h in range(MLA_HEADS):
        lo = 2 * LANES * h
        k_ref[:, lo:lo + LANES] = a[:, lo:lo + LANES].astype(BF16)
        k_ref[:, lo + LANES:lo + 2 * LANES] = kr
        v_ref[:, lo:lo + LANES] = a[:, lo + LANES:lo + 2 * LANES].astype(BF16)
        v_ref[:, lo + LANES:lo + 2 * LANES] = ones


def _mla_kv(p, g_kva, w_ukv, tabs):
    T = p.shape[0]
    tm = _prep_tile(T)
    tab = pl.BlockSpec((tm, LANES), lambda i: (i, 0))
    wide = 2 * LANES * MLA_HEADS
    return pl.pallas_call(
        _mla_kv_kernel,
        grid=(T // tm,),
        in_specs=[pl.BlockSpec((tm, MLA_KV_RANK), lambda i: (i, P_CKV // MLA_KV_RANK)),
                  pl.BlockSpec((tm, LANES), lambda i: (i, P_KR // LANES)),
                  pl.BlockSpec((1, MLA_KV_RANK), lambda i: (0, 0)),
                  pl.BlockSpec((MLA_KV_RANK, wide), lambda i: (0, 0)),
                  tab, tab, tab],
        out_specs=[pl.BlockSpec((tm, wide), lambda i: (i, 0)),
                   pl.BlockSpec((tm, wide), lambda i: (i, 0))],
        out_shape=[jax.ShapeDtypeStruct((T, wide), BF16),
                   jax.ShapeDtypeStruct((T, wide), BF16)],
        compiler_params=_params(("arbitrary",)),
    )(p, p, g_kva.reshape(1, -1), w_ukv, *tabs)


def _flash_kernel(q_ref, k_ref, v_ref, *rest):
    o_ref, m_sc, acc_sc = rest[-3:]
    kk = pl.program_id(2)

    @pl.when(kk == 0)
    def _():
        m_sc[...] = jnp.full(m_sc.shape, -1e30, F32)
        acc_sc[...] = jnp.zeros(acc_sc.shape, F32)

    rc = q_ref.shape[0] // FLASH_SUBTILES
    subs = [slice(r * rc, (r + 1) * rc) for r in range(FLASH_SUBTILES)]
    scores = [lax.dot_general(q_ref[rows, :], k_ref[...], _NT, preferred_element_type=F32) for rows in subs]
    for rows, s in zip(subs, scores):
        m_prev = m_sc[rows, :]
        m_new = jnp.maximum(m_prev, jnp.max(s, axis=1, keepdims=True))
        alpha = jnp.exp(m_prev - m_new)
        p = jnp.exp((s - m_new).astype(BF16))
        acc_sc[rows, :] = alpha * acc_sc[rows, :] + jnp.dot(p, v_ref[...], preferred_element_type=F32)
        m_sc[rows, :] = m_new

    @pl.when(kk == pl.num_programs(2) - 1)
    def _():
        acc = acc_sc[...]
        o_ref[...] = (acc[:, :LANES] / acc[:, LANES:LANES + 1]).astype(o_ref.dtype)


def _flash(q, k, v, out_prev, n_rows, q_blk0, n_qblk, k_blk0, n_kblk, tq, tk):
    dq = 2 * LANES
    in_specs = [pl.BlockSpec((tq, dq), lambda h, i, kk: (i + q_blk0, h)),
                pl.BlockSpec((tk, dq), lambda h, i, kk: (kk + k_blk0, h)),
                pl.BlockSpec((tk, dq), lambda h, i, kk: (kk + k_blk0, h))]
    args = [q, k, v]
    aliases = {}
    if out_prev is not None:
        in_specs.append(pl.BlockSpec(memory_space=pl.ANY))
        args.append(out_prev)
        aliases = {3: 0}
    return pl.pallas_call(
        _flash_kernel,
        grid=(MLA_HEADS, n_qblk, n_kblk),
        in_specs=in_specs,
        out_specs=pl.BlockSpec((tq, LANES), lambda h, i, kk: (i + q_blk0, h)),
        out_shape=jax.ShapeDtypeStruct((n_rows, MLA_HEADS * LANES), BF16),
        scratch_shapes=[pltpu.VMEM((tq, 1), F32), pltpu.VMEM((tq, dq), F32)],
        input_output_aliases=aliases,
        compiler_params=_params(("arbitrary", "arbitrary", "arbitrary"), 48),
    )(*args)


def _gla_direction(q, k, v, glr, wg, bg, tri, last_row, st_ref, o_ref):
    z = jnp.dot(glr.astype(BF16), wg, preferred_element_type=F32) + bg
    la = (jnp.minimum(z, 0.0) - jnp.log(1.0 + jnp.exp(-jnp.abs(z)))) * (1.0 / GLA_GATE_TAU)
    ones = jnp.where(tri, 1.0, 0.0).astype(BF16)
    la_hi = la.astype(BF16)
    la_lo = (la - la_hi.astype(F32)).astype(BF16)
    b = (jnp.dot(ones, la_hi, preferred_element_type=F32)
         + jnp.dot(ones, la_lo, preferred_element_type=F32))
    b_tot = b[last_row:last_row + 1, :]
    qt = q * (GLA_DK ** -0.5) * jnp.exp(b)
    kt = k * jnp.exp(-b)
    kh = k * jnp.exp(b_tot - b)
    dec = jnp.exp(b_tot)
    for h in range(GLA_HEADS):
        ks = slice(GLA_DK * h, GLA_DK * (h + 1))
        vs = slice(GLA_DV * h, GLA_DV * (h + 1))
        qh = qt[:, ks].astype(BF16)
        vh = v[:, vs]
        st = st_ref[h]
        inter = lax.dot_general(qh, st.astype(BF16), _NT, preferred_element_type=F32)
        sc = lax.dot_general(qh, kt[:, ks].astype(BF16), _NT, preferred_element_type=F32)
        sc = jnp.where(tri, sc, 0.0)
        intra = jnp.dot(sc.astype(BF16), vh.astype(BF16), preferred_element_type=F32)
        o_ref[:, vs] = inter + intra
        st_ref[h] = st * dec[:, ks] + jnp.dot(vh.T.astype(BF16), kh[:, ks].astype(BF16),
                                               preferred_element_type=F32)


def _gla_kernel(qf, kf, vf, gf, qb, kb, vb, gb, wf, bf, wb, bb, of_ref, ob_ref, stf, stb):
    @pl.when(pl.program_id(0) == 0)
    def _():
        stf[...] = jnp.zeros(stf.shape, F32)
        stb[...] = jnp.zeros(stb.shape, F32)

    cs = qf.shape[0]
    t_i = lax.broadcasted_iota(jnp.int32, (cs, cs), 0)
    s_i = lax.broadcasted_iota(jnp.int32, (cs, cs), 1)
    _gla_direction(qf[...], kf[...], vf[...], gf[...], wf[...], bf[...], s_i <= t_i, cs - 1, stf, of_ref)
    _gla_direction(qb[...], kb[...], vb[...], gb[...], wb[...], bb[...], s_i >= t_i, 0, stb, ob_ref)


def _gla(p, n_lat, wf_ext, bf, wb_ext, bb):
    T = p.shape[0]
    cs = GLA_CHUNK
    nch = T // cs
    nlc = n_lat // cs
    fwd = lambda c: (c + nlc) % nch
    bwd = lambda c: nch - 1 - c
    dk, dv = GLA_HEADS * GLA_DK, GLA_HEADS * GLA_DV

    def specs(idx):
        return [pl.BlockSpec((cs, dk), lambda c: (idx(c), P_GQ // dk)),
                pl.BlockSpec((cs, dk), lambda c: (idx(c), P_GK // dk)),
                pl.BlockSpec((cs, dv), lambda c: (idx(c), P_GV // dv)),
                pl.BlockSpec((cs, LANES), lambda c: (idx(c), P_GLR // LANES))]

    wspec = pl.BlockSpec((LANES, dk), lambda c: (0, 0))
    bspec = pl.BlockSpec((1, dk), lambda c: (0, 0))
    return pl.pallas_call(
        _gla_kernel,
        grid=(nch,),
        in_specs=specs(fwd) + specs(bwd) + [wspec, bspec, wspec, bspec],
        out_specs=[pl.BlockSpec((cs, dv), lambda c: (fwd(c), 0)),
                   pl.BlockSpec((cs, dv), lambda c: (bwd(c), 0))],
        out_shape=[jax.ShapeDtypeStruct((T, dv), F32), jax.ShapeDtypeStruct((T, dv), F32)],
        scratch_shapes=[pltpu.VMEM((GLA_HEADS, GLA_DV, GLA_DK), F32),
                        pltpu.VMEM((GLA_HEADS, GLA_DV, GLA_DK), F32)],
        compiler_params=_params(("arbitrary",)),
    )(p, p, p, p, p, p, p, p, wf_ext, bf.reshape(1, dk), wb_ext, bb.reshape(1, dk))


def _gla_out_kernel(of_ref, ob_ref, gr_ref, g_ref, o_ref):
    o = of_ref[...] + ob_ref[...]
    gr = gr_ref[...]
    for h in range(GLA_HEADS):
        vs = slice(GLA_DV * h, GLA_DV * (h + 1))
        o_ref[:, vs] = (_rms(o[:, vs], g_ref[...]) * _silu(gr[:, vs])).astype(BF16)


def _gla_out(o_f, o_b, p, g_gla):
    T, dv = o_f.shape
    tm = _prep_tile(T)
    row = pl.BlockSpec((tm, dv), lambda i: (i, 0))
    return pl.pallas_call(
        _gla_out_kernel,
        grid=(T // tm,),
        in_specs=[row, row, pl.BlockSpec((tm, dv), lambda i: (i, P_GR // dv)),
                  pl.BlockSpec((1, GLA_DV), lambda i: (0, 0))],
        out_specs=row,
        out_shape=jax.ShapeDtypeStruct((T, dv), BF16),
        compiler_params=_params(("arbitrary",)),
    )(o_f, o_b, p, g_gla.reshape(1, GLA_DV))


def _dft_cos_sin(n):
    k = np.arange(n)
    ang = 2.0 * np.pi * ((k[:, None] * k[None, :]) % n) / n
    return np.cos(ang), np.sin(ang)


def _channel_dft(xb, wd):
    zr, zi = [], []
    for g in range(FNET_GROUPS):
        z = jnp.dot(xb[:, LANES * g:LANES * (g + 1)], wd, preferred_element_type=F32)
        zr.append(z[:, :LANES])
        zi.append(z[:, LANES:])
    return jnp.concatenate(zr, axis=1), jnp.concatenate(zi, axis=1)


def _fft_a_kernel(n1, x_ref, wd_ref, m_ref, bre_ref, bim_ref):
    for n2 in range(FFT_N2):
        xs = x_ref[pl.ds(n2, n1, stride=FFT_N2), :].astype(BF16)
        z = jnp.dot(xs, wd_ref[...], preferred_element_type=F32)
        zs = jnp.concatenate([z[:, :LANES], z[:, LANES:]], axis=0).astype(BF16)
        b = jnp.dot(m_ref[n2], zs, preferred_element_type=F32)
        bre_ref[n2 * n1:(n2 + 1) * n1, :] = b[:n1]
        bim_ref[n2 * n1:(n2 + 1) * n1, :] = b[n1:]


def _fft_b_kernel(n1, scale, c_ref, s_ref, bre_ref, bim_ref, o_ref):
    grp = 8
    for t in range(n1 // grp):
        rows = [pl.ds(grp * t + r, FFT_N2, stride=n1) for r in range(grp)]
        br = jnp.concatenate([bre_ref[rw, :] for rw in rows], axis=1).astype(BF16)
        bi = jnp.concatenate([bim_ref[rw, :] for rw in rows], axis=1).astype(BF16)
        o = (jnp.dot(c_ref[...], br, preferred_element_type=F32)
             + jnp.dot(s_ref[...], bi, preferred_element_type=F32)) * scale
        for r in range(grp):
            o_ref[rows[r], :] = o[:, LANES * r:LANES * (r + 1)]


def _fft_ctx_kernel(scale, x_ref, wd_ref, cs_ref, prev_ref, o_ref):
    del prev_ref
    zr, zi = _channel_dft(x_ref[...].astype(BF16), wd_ref[...])
    z = jnp.concatenate([zr, zi], axis=0).astype(BF16)
    o_ref[...] = jnp.dot(cs_ref[...], z, preferred_element_type=F32) * scale


def _fourier(p, n_lat, n_ctx):
    T = p.shape[0]
    gw = FNET_GROUPS * LANES
    n1 = n_lat // FFT_N2
    cd, sd = _dft_cos_sin(LANES)
    as_bf16 = lambda a: jnp.asarray(a, F32).astype(BF16)
    wd = as_bf16(np.concatenate([cd, -sd], axis=1))
    k1 = jnp.arange(n1, dtype=jnp.int32)[None, :, None]
    pos = (FFT_N2 * jnp.arange(n1, dtype=jnp.int32)[None, None, :]
           + jnp.arange(FFT_N2, dtype=jnp.int32)[:, None, None])
    ang = (2.0 * np.pi / n_lat) * ((k1 * pos) % n_lat).astype(F32)
    gc, gs = jnp.cos(ang), jnp.sin(ang)
    m = jnp.concatenate([jnp.concatenate([gc, gs], axis=2),
                         jnp.concatenate([-gs, gc], axis=2)], axis=1).astype(BF16)
    col = lambda g: (0, g)
    bre, bim = pl.pallas_call(
        functools.partial(_fft_a_kernel, n1),
        grid=(FNET_GROUPS,),
        in_specs=[pl.BlockSpec((n_lat, LANES), lambda g: (0, P_FU // LANES + g)),
                  pl.BlockSpec((LANES, 2 * LANES), lambda g: (0, 0)),
                  pl.BlockSpec((FFT_N2, 2 * n1, 2 * n1), lambda g: (0, 0, 0))],
        out_specs=[pl.BlockSpec((n_lat, LANES), col)] * 2,
        out_shape=[jax.ShapeDtypeStruct((n_lat, gw), F32)] * 2,
        compiler_params=_params(("arbitrary",), 48),
    )(p, wd, m)

    c2, s2 = _dft_cos_sin(FFT_N2)
    mspec = pl.BlockSpec((FFT_N2, FFT_N2), lambda g: (0, 0))
    y = pl.pallas_call(
        functools.partial(_fft_b_kernel, n1, float((n_lat * LANES) ** -0.5)),
        grid=(FNET_GROUPS,),
        in_specs=[mspec, mspec, pl.BlockSpec((n_lat, LANES), col), pl.BlockSpec((n_lat, LANES), col)],
        out_specs=pl.BlockSpec((n_lat, LANES), col),
        out_shape=jax.ShapeDtypeStruct((T, gw), F32),
        compiler_params=_params(("arbitrary",)),
    )(as_bf16(c2), as_bf16(s2), bre, bim)

    cc, sc = _dft_cos_sin(n_ctx)
    return pl.pallas_call(
        functools.partial(_fft_ctx_kernel, float((n_ctx * LANES) ** -0.5)),
        grid=(1,),
        in_specs=[pl.BlockSpec((n_ctx, gw), lambda j: (n_lat // n_ctx, P_FU // gw)),
                  pl.BlockSpec((LANES, 2 * LANES), lambda j: (0, 0)),
                  pl.BlockSpec((n_ctx, 2 * n_ctx), lambda j: (0, 0)),
                  pl.BlockSpec(memory_space=pl.ANY)],
        out_specs=pl.BlockSpec((n_ctx, gw), lambda j: (n_lat // n_ctx, 0)),
        out_shape=jax.ShapeDtypeStruct((T, gw), F32),
        input_output_aliases={3: 0},
        compiler_params=_params(("arbitrary",)),
    )(p, wd, as_bf16(np.concatenate([cc, sc], axis=1)), y)


def _swa_prep_kernel(q_ref, k_ref, v_ref, cos_ref, sa_ref, sb_ref, qo_ref, ko_ref, vo_ref):
    cos, sa, sb = cos_ref[...], sa_ref[...], sb_ref[...]
    for h in range(SWA_HEADS):
        sl = slice(LANES * h, LANES * (h + 1))
        qo_ref[:, sl] = (_rope(q_ref[:, sl], cos, sa, sb, 32) * SWA_SCALE).astype(BF16)
    for h in range(SWA_KV_HEADS):
        sl = slice(LANES * h, LANES * (h + 1))
        ko_ref[:, sl] = _rope(k_ref[:, sl], cos, sa, sb, 32).astype(BF16)
    vo_ref[...] = v_ref[...].astype(BF16)


def _swa_prep(p, tabs):
    T = p.shape[0]
    tm = _prep_tile(T)
    qw, kw = SWA_HEADS * LANES, SWA_KV_HEADS * LANES
    tab = pl.BlockSpec((tm, LANES), lambda i: (i, 0))
    return pl.pallas_call(
        _swa_prep_kernel,
        grid=(T // tm,),
        in_specs=[pl.BlockSpec((tm, qw), lambda i: (i, P_SQ // qw)),
                  pl.BlockSpec((tm, kw), lambda i: (i, P_SK // kw)),
                  pl.BlockSpec((tm, kw), lambda i: (i, P_SV // kw)),
                  tab, tab, tab],
        out_specs=[pl.BlockSpec((tm, qw), lambda i: (i, 0)),
                   pl.BlockSpec((tm, kw), lambda i: (i, 0)),
                   pl.BlockSpec((tm, kw), lambda i: (i, 0))],
        out_shape=[jax.ShapeDtypeStruct((T, qw), BF16),
                   jax.ShapeDtypeStruct((T, kw), BF16),
                   jax.ShapeDtypeStruct((T, kw), BF16)],
        compiler_params=_params(("arbitrary",)),
    )(p, p, p, *tabs)


def _sink_attend(q, kcat, vcat, mask, sink):
    s = lax.dot_general(q, kcat, _NT, preferred_element_type=F32)
    if mask is not None:
        s = jnp.where(mask, s, -1e30)
    m = jnp.maximum(jnp.max(s, axis=1, keepdims=True), sink)
    pr = jnp.exp((s - m).astype(BF16))
    den = jnp.sum(pr.astype(F32), axis=1, keepdims=True) + jnp.exp(sink - m)
    return jnp.dot(pr, vcat, preferred_element_type=F32) / den


def _swa_lat_kernel(q_ref, kc_ref, kp_ref, k0_ref, kn_ref, vc_ref, vp_ref, v0_ref, vn_ref,
                    sink_ref, o_ref):
    kvh, nb = pl.program_id(0), pl.program_id(1)
    n_ctx = kc_ref.shape[0]
    qb = q_ref.shape[0]
    win = SWA_BLOCK
    kcat = jnp.concatenate([kc_ref[...], kp_ref[...], k0_ref[...], kn_ref[...]], axis=0)
    vcat = jnp.concatenate([vc_ref[...], vp_ref[...], v0_ref[...], vn_ref[...]], axis=0)
    n_keys = n_ctx + qb + 2 * win
    i = lax.broadcasted_iota(jnp.int32, (qb, n_keys), 0)
    j = lax.broadcasted_iota(jnp.int32, (qb, n_keys), 1)
    off = j - (n_ctx + win)
    in_band = jnp.abs(i - off) <= win
    exists = ((off >= 0) | (nb > 0)) & ((off < qb) | (nb < pl.num_programs(1) - 1))
    mask = (j < n_ctx) | (in_band & exists)
    for g in range(SWA_GROUP):
        sl = slice(LANES * g, LANES * (g + 1))
        sink = sink_ref[pl.ds(kvh * SWA_GROUP + g, 1), 0:1]
        o_ref[:, sl] = _sink_attend(q_ref[:, sl], kcat, vcat, mask, sink).astype(BF16)


def _swa_ctx_kernel(q_ref, k_ref, v_ref, sink_ref, prev_ref, o_ref):
    del prev_ref
    kvh = pl.program_id(0)
    for g in range(SWA_GROUP):
        sl = slice(LANES * g, LANES * (g + 1))
        sink = sink_ref[pl.ds(kvh * SWA_GROUP + g, 1), 0:1]
        o_ref[:, sl] = _sink_attend(q_ref[:, sl], k_ref[...], v_ref[...], None, sink).astype(BF16)


def _swa(qs, ks, vs, sink, n_lat, n_ctx):
    T = qs.shape[0]
    win = SWA_BLOCK
    qb = SWA_QBLOCK
    nb = n_lat // qb
    per = qb // win
    gq = SWA_GROUP * LANES
    sink2d = jnp.broadcast_to(sink.astype(F32)[:, None], (SWA_HEADS, LANES))
    cblk = n_lat // n_ctx
    ctx_spec = pl.BlockSpec((n_ctx, LANES), lambda h, b: (cblk, h))
    prev_spec = pl.BlockSpec((win, LANES), lambda h, b: (jnp.maximum(b * per - 1, 0), h))
    cur_spec = pl.BlockSpec((qb, LANES), lambda h, b: (b, h))
    next_spec = pl.BlockSpec((win, LANES), lambda h, b: (jnp.minimum((b + 1) * per, nb * per - 1), h))
    sink_spec = pl.BlockSpec((SWA_HEADS, LANES), lambda h, b: (0, 0))
    o_lat = pl.pallas_call(
        _swa_lat_kernel,
        grid=(SWA_KV_HEADS, nb),
        in_specs=[pl.BlockSpec((qb, gq), lambda h, b: (b, h)),
                  ctx_spec, prev_spec, cur_spec, next_spec,
                  ctx_spec, prev_spec, cur_spec, next_spec, sink_spec],
        out_specs=pl.BlockSpec((qb, gq), lambda h, b: (b, h)),
        out_shape=jax.ShapeDtypeStruct((T, SWA_HEADS * LANES), BF16),
        compiler_params=_params(("arbitrary", "arbitrary")),
    )(qs, ks, ks, ks, ks, vs, vs, vs, vs, sink2d)
    return pl.pallas_call(
        _swa_ctx_kernel,
        grid=(SWA_KV_HEADS,),
        in_specs=[pl.BlockSpec((n_ctx, gq), lambda h: (cblk, h)),
                  pl.BlockSpec((n_ctx, LANES), lambda h: (cblk, h)),
                  pl.BlockSpec((n_ctx, LANES), lambda h: (cblk, h)),
                  pl.BlockSpec((SWA_HEADS, LANES), lambda h: (0, 0)),
                  pl.BlockSpec(memory_space=pl.ANY)],
        out_specs=pl.BlockSpec((n_ctx, gq), lambda h: (cblk, h)),
        out_shape=jax.ShapeDtypeStruct((T, SWA_HEADS * LANES), BF16),
        input_output_aliases={4: 0},
        compiler_params=_params(("arbitrary",)),
    )(qs, ks, vs, sink2d, o_lat)


GATE_HALO = 8
GATE_TILE = 64


def _gate_kernel(tm, n_lat, n_tot, ug_ref, ua_ref, gp_ref, gn_ref, ap_ref, an_ref, cwg_ref, cwa_ref,
                 cbg_ref, cba_ref, o_ref, ubuf):
    i = pl.program_id(0)
    first = (i == 0) | (i * tm == n_lat)
    last = ((i + 1) * tm == n_lat) | ((i + 1) * tm == n_tot)
    rows = tm + 2 * GATE_HALO

    def conv(u_ref, p_ref, n_ref, cw_ref, cb_ref):
        ubuf[0:GATE_HALO, :] = jnp.where(first, 0.0, p_ref[...])
        ubuf[GATE_HALO:GATE_HALO + tm, :] = u_ref[...]
        ubuf[GATE_HALO + tm:, :] = jnp.where(last, 0.0, n_ref[...])
        ub = ubuf[...]
        um = pltpu.roll(ub, 1, 0)[GATE_HALO:GATE_HALO + tm]
        up = pltpu.roll(ub, rows - 1, 0)[GATE_HALO:GATE_HALO + tm]
        return um * cw_ref[0:1, :] + u_ref[...] * cw_ref[1:2, :] + up * cw_ref[2:3, :] + cb_ref[...]

    g = conv(ug_ref, gp_ref, gn_ref, cwg_ref, cbg_ref)
    a = conv(ua_ref, ap_ref, an_ref, cwa_ref, cba_ref)
    o_ref[...] = (_silu(g) * a).astype(BF16)


def _gate(u, n_lat, conv_w, conv_b):
    T = u.shape[0]
    dff = u.shape[1] // 2
    tm = GATE_TILE
    hb = tm // GATE_HALO
    last_hb = T // GATE_HALO - 1
    cb2 = conv_b.reshape(1, 2 * dff)
    prev = lambda i: jnp.maximum(i * hb - 1, 0)
    nxt = lambda i: jnp.minimum((i + 1) * hb, last_hb)
    return pl.pallas_call(
        functools.partial(_gate_kernel, tm, n_lat, T),
        grid=(T // tm,),
        in_specs=[pl.BlockSpec((tm, dff), lambda i: (i, 0)),
                  pl.BlockSpec((tm, dff), lambda i: (i, 1)),
                  pl.BlockSpec((GATE_HALO, dff), lambda i: (prev(i), 0)),
                  pl.BlockSpec((GATE_HALO, dff), lambda i: (nxt(i), 0)),
                  pl.BlockSpec((GATE_HALO, dff), lambda i: (prev(i), 1)),
                  pl.BlockSpec((GATE_HALO, dff), lambda i: (nxt(i), 1)),
                  pl.BlockSpec((3, dff), lambda i: (0, 0)),
                  pl.BlockSpec((3, dff), lambda i: (0, 1)),
                  pl.BlockSpec((1, dff), lambda i: (0, 0)),
                  pl.BlockSpec((1, dff), lambda i: (0, 1))],
        out_specs=pl.BlockSpec((tm, dff), lambda i: (i, 0)),
        out_shape=jax.ShapeDtypeStruct((T, dff), BF16),
        scratch_shapes=[pltpu.VMEM((tm + 2 * GATE_HALO, dff), F32)],
        compiler_params=_params(("arbitrary",), 48),
    )(u, u, u, u, u, u, conv_w, conv_w, cb2, cb2)


def _pad_cols(w, width):
    return jnp.pad(w, ((0, 0), (0, width - w.shape[1])))


def _w_in_ext(w_in):
    cq, ckv, kr, gq, gk, gv, gr, glr, fu, sq, sk, sv = jnp.split(
        w_in, np.cumsum([768, 512, 64, 512, 512, 1024, 1024, 32, 1024, 1024, 256]).tolist(), axis=2)
    pad = lambda w: jnp.pad(w, ((0, 0), (0, 0), (0, LANES - w.shape[2])))
    parts = [fu, gv, gr, cq, sk, sq, ckv, gq, gk, sv, pad(kr), pad(glr)]
    return jnp.concatenate(parts, axis=2).astype(BF16)


def _tile_for(total, parts, mult=16):
    t = total // parts
    assert t * parts == total and t % mult == 0, (total, parts)
    return t


def _key_tile(total):
    best = LANES
    for t in range(LANES, 1536 + 1, LANES):
        if total % t == 0:
            best = t
    return best


def kernel(x, c, ctx, c_ctx, w_ada, b_ada, g_pre_mix, g_post_mix, g_pre_ffn, g_post_ffn, w_in, g_qa, w_uq,
           g_kva, w_ukv, w_gate_f, b_gate_f, w_gate_b, b_gate_b, g_gla, swa_sink, w_out, w_up, conv_w, conv_b,
           w_down):
    B, N, D = x.shape
    C = ctx.shape[1]
    T = N + C
    L = w_ada.shape[0]
    dff = w_down.shape[1]
    assert B == 1 and D == 4096 and N % (GRID_W * 16) == 0 and C % ROW_TILE == 0 and N % C == 0
    x2d, ctx2d = x[0], ctx[0]

    cc = jnp.zeros((8, D), F32).at[0].set(c[0]).at[1].set(c_ctx)
    mod = _ada(cc, w_ada, b_ada)
    mla_tabs = _rope_tables(N, C, MLA_ROPE, MLA_ROPE // 4)
    swa_tabs = _rope_tables(N, C, LANES, 32)

    tm_mm = _tile_for(T, 8)
    tm_down = _tile_for(T, 16)

    w_in_b = _w_in_ext(w_in)
    w_down_b = w_down.astype(BF16)
    tq, tk = _tile_for(N, 4), _key_tile(T)

    h = _rms_mod(x2d, ctx2d, g_pre_mix[0], mod[0], 0)
    xs = (x2d, ctx2d)
    for l in range(L):
        last = l == L - 1
        p = _matmul([h], w_in_b, l, tm_mm, 512, F32)

        w_uq_ext = jnp.pad(w_uq[l].reshape(MLA_Q_RANK, MLA_HEADS, MLA_NOPE + MLA_ROPE),
                           ((0, 0), (0, 0), (0, 2 * LANES - MLA_NOPE - MLA_ROPE))
                           ).reshape(MLA_Q_RANK, 2 * LANES * MLA_HEADS).astype(BF16)
        q_a = _mla_q(p, g_qa[l], w_uq_ext, mla_tabs)
        k_a, v_a = _mla_kv(p, g_kva[l], w_ukv[l].astype(BF16), mla_tabs)
        o_a = _flash(q_a, k_a, v_a, None, T, 0, N // tq, 0, T // tk, tq, tk)
        o_a = _flash(q_a, k_a, v_a, o_a, T, N // C, 1, N // C, 1, C, C)

        wf_ext = jnp.zeros((LANES, GLA_HEADS * GLA_DK), F32).at[:GLA_GATE_RANK].set(w_gate_f[l]).astype(BF16)
        wb_ext = jnp.zeros((LANES, GLA_HEADS * GLA_DK), F32).at[GLA_GATE_RANK:2 * GLA_GATE_RANK].set(
            w_gate_b[l]).astype(BF16)
        o_f, o_bk = _gla(p, N, wf_ext, b_gate_f[l], wb_ext, b_gate_b[l])
        o_b = _gla_out(o_f, o_bk, p, g_gla[l])

        o_c = _fourier(p, N, C)

        qs, ks, vs = _swa_prep(p, swa_tabs)
        o_d = _swa(qs, ks, vs, swa_sink[l], N, C)

        y = _matmul([o_a, o_b, o_c, o_d], w_out, l, tm_mm, 512, F32)
        x1, h2 = _resid(xs, y, g_post_mix[l], mod[l], 2, N, T, pre=(g_pre_ffn[l], mod[l], 3))
        u = _matmul([h2], w_up, l, tm_mm, 512, F32)
        act = _gate(u, N, conv_w[l], conv_b[l])
        f = _matmul([act], w_down_b, l, tm_down, 512, F32)
        if last:
            (x_out,) = _resid((x1,), f, g_post_ffn[l], mod[l], 5, N, N)
            return x_out.reshape(1, N, D)
        x2, h = _resid((x1,), f, g_post_ffn[l], mod[l], 5, N, T, pre=(g_pre_mix[l + 1], mod[l + 1], 0))
        xs = (x2,)
```

```python
import functools

import numpy as np
import jax
import jax.numpy as jnp
from jax import lax
from jax.experimental import pallas as pl
from jax.experimental.pallas import tpu as pltpu

F32 = jnp.float32
BF16 = jnp.bfloat16

GRID_W = 64
EPS = 1e-6
ROPE_THETA = 10000.0
MLA_HEADS = 8
MLA_Q_RANK = 768
MLA_KV_RANK = 512
MLA_NOPE = 128
MLA_ROPE = 64
MLA_SCALE = (MLA_NOPE + MLA_ROPE) ** -0.5
GLA_HEADS = 4
GLA_DK = 128
GLA_DV = 256
GLA_GATE_RANK = 16
GLA_GATE_TAU = 16.0
GLA_CHUNK = 128
FNET_GROUPS = 8
FFT_N2 = 64
SWA_HEADS = 8
SWA_KV_HEADS = 2
SWA_GROUP = 4
SWA_BLOCK = 128
SWA_QBLOCK = 512
FLASH_SUBTILES = 8
SWA_SCALE = 128 ** -0.5
LANES = 128
ROW_TILE = 256

P_FU, P_GV, P_GR, P_CQ, P_SK, P_SQ = 0, 1024, 2048, 3072, 3840, 4096
P_CKV, P_GQ, P_GK, P_SV, P_KR, P_GLR = 5120, 5632, 6144, 6656, 6912, 7040
P_WIDTH = 7168

_NT = (((1,), (1,)), ((), ()))
_MIB = 1024 * 1024


def _params(sem, vmem_mib=40):
    return pltpu.CompilerParams(dimension_semantics=sem, vmem_limit_bytes=vmem_mib * _MIB)


def _silu(v):
    return v / (1.0 + jnp.exp(-v))


def _rms(v, g):
    return v * lax.rsqrt(jnp.mean(v * v, axis=-1, keepdims=True) + EPS) * g


def _pick(is_ctx, ref):
    return jnp.where(is_ctx, ref[1:2, :], ref[0:1, :])


def _ada_kernel(c_ref, w_ref, b_ref, o_ref):
    a = _silu(c_ref[...]).astype(BF16)
    o_ref[...] = jnp.dot(a, w_ref[...].astype(BF16), preferred_element_type=F32) + b_ref[...]


def _ada(cc, w_ada, b_ada):
    L, D, N6 = w_ada.shape
    tn = 512
    return pl.pallas_call(
        _ada_kernel,
        grid=(L, N6 // tn),
        in_specs=[pl.BlockSpec((8, D), lambda l, j: (0, 0)),
                  pl.BlockSpec((None, D, tn), lambda l, j: (l, 0, j)),
                  pl.BlockSpec((None, 1, tn), lambda l, j: (l, 0, j))],
        out_specs=pl.BlockSpec((None, 8, tn), lambda l, j: (l, 0, j)),
        out_shape=jax.ShapeDtypeStruct((L, 8, N6), F32),
        compiler_params=_params(("arbitrary", "arbitrary")),
    )(cc, w_ada, b_ada.reshape(L, 1, N6))


def _rms_mod_kernel(n_lat_tiles, x_ref, c_ref, g_ref, sh_ref, sc_ref, h_ref):
    is_ctx = pl.program_id(0) >= n_lat_tiles
    xv = jnp.where(is_ctx, c_ref[...], x_ref[...])
    y = _rms(xv, g_ref[...])
    h_ref[...] = (y * (1.0 + _pick(is_ctx, sc_ref)) + _pick(is_ctx, sh_ref)).astype(BF16)


def _rms_mod(x2d, ctx2d, g, mod, k_shift):
    N, D = x2d.shape
    C = ctx2d.shape[0]
    tm = ROW_TILE
    nl, nc = N // tm, C // tm
    return pl.pallas_call(
        functools.partial(_rms_mod_kernel, nl),
        grid=(nl + nc,),
        in_specs=[pl.BlockSpec((tm, D), lambda i: (jnp.minimum(i, nl - 1), 0)),
                  pl.BlockSpec((tm, D), lambda i: (jnp.maximum(i - nl, 0), 0)),
                  pl.BlockSpec((1, D), lambda i: (0, 0)),
                  pl.BlockSpec((8, D), lambda i: (0, k_shift)),
                  pl.BlockSpec((8, D), lambda i: (0, k_shift + 1))],
        out_specs=pl.BlockSpec((tm, D), lambda i: (i, 0)),
        out_shape=jax.ShapeDtypeStruct((N + C, D), BF16),
        compiler_params=_params(("arbitrary",)),
    )(x2d, ctx2d, g.reshape(1, D), mod, mod)


def _resid_kernel(n_lat_tiles, split_x, with_h, *refs):
    refs = list(refs)
    x_ref = refs.pop(0)
    c_ref = refs.pop(0) if split_x else None
    y_ref, gpost_ref, gate_ref = refs.pop(0), refs.pop(0), refs.pop(0)
    if with_h:
        gpre_ref, sh_ref, sc_ref = refs.pop(0), refs.pop(0), refs.pop(0)
    x1_ref = refs.pop(0)
    is_ctx = pl.program_id(0) >= n_lat_tiles
    xv = jnp.where(is_ctx, c_ref[...], x_ref[...]) if split_x else x_ref[...]
    x1 = xv + _pick(is_ctx, gate_ref) * _rms(y_ref[...], gpost_ref[...])
    x1_ref[...] = x1
    if with_h:
        h_ref = refs.pop(0)
        n = _rms(x1, gpre_ref[...])
        h_ref[...] = (n * (1.0 + _pick(is_ctx, sc_ref)) + _pick(is_ctx, sh_ref)).astype(BF16)


def _resid(xs, y, g_post, mod_gate, k_gate, n_lat, n_rows, pre=None):
    D = y.shape[1]
    tm = ROW_TILE
    nl = n_lat // tm
    nt = n_rows // tm
    split_x = len(xs) == 2
    row = lambda i: (i, 0)
    vec = lambda i: (0, 0)
    if split_x:
        in_specs = [pl.BlockSpec((tm, D), lambda i: (jnp.minimum(i, nl - 1), 0)),
                    pl.BlockSpec((tm, D), lambda i: (jnp.maximum(i - nl, 0), 0))]
    else:
        in_specs = [pl.BlockSpec((tm, D), row)]
    in_specs += [pl.BlockSpec((tm, D), row), pl.BlockSpec((1, D), vec),
                 pl.BlockSpec((8, D), lambda i: (0, k_gate))]
    args = list(xs) + [y, g_post.reshape(1, D), mod_gate]
    out_specs = [pl.BlockSpec((tm, D), row)]
    out_shape = [jax.ShapeDtypeStruct((n_rows, D), F32)]
    if pre is not None:
        g_pre, mod_pre, k_shift = pre
        in_specs += [pl.BlockSpec((1, D), vec),
                     pl.BlockSpec((8, D), lambda i: (0, k_shift)),
                     pl.BlockSpec((8, D), lambda i: (0, k_shift + 1))]
        args += [g_pre.reshape(1, D), mod_pre, mod_pre]
        out_specs.append(pl.BlockSpec((tm, D), row))
        out_shape.append(jax.ShapeDtypeStruct((n_rows, D), BF16))
    return pl.pallas_call(
        functools.partial(_resid_kernel, nl, split_x, pre is not None),
        grid=(nt,), in_specs=in_specs, out_specs=out_specs, out_shape=out_shape,
        compiler_params=_params(("arbitrary",)),
    )(*args)


def _mm_kernel(n_in, *refs):
    w_ref, o_ref = refs[n_in], refs[n_in + 1]
    kc = w_ref.shape[0] // n_in
    acc = None
    for k in range(n_in):
        d = jnp.dot(refs[k][...].astype(BF16), w_ref[kc * k:kc * (k + 1), :].astype(BF16),
                    preferred_element_type=F32)
        acc = d if acc is None else acc + d
    o_ref[...] = acc.astype(o_ref.dtype)


def _matmul(a_list, w3, layer, tm, tn, out_dtype, single_buffer_rows=False):
    M = a_list[0].shape[0]
    _, K, Nn = w3.shape
    kc = K // len(a_list)
    mode = dict(pipeline_mode=pl.Buffered(1)) if single_buffer_rows else {}
    return pl.pallas_call(
        functools.partial(_mm_kernel, len(a_list)),
        grid=(M // tm, Nn // tn),
        in_specs=[pl.BlockSpec((tm, kc), lambda i, j: (i, 0), **mode) for _ in a_list]
        + [pl.BlockSpec((None, K, tn), lambda i, j: (layer, 0, j))],
        out_specs=pl.BlockSpec((tm, tn), lambda i, j: (i, j)),
        out_shape=jax.ShapeDtypeStruct((M, Nn), out_dtype),
        compiler_params=_params(("arbitrary", "arbitrary"), 56),
    )(*a_list, w3)


def _rope_tables(n_lat, n_ctx, n_real, half):
    t = jnp.arange(n_lat, dtype=jnp.int32)
    rows = (t // GRID_W).astype(F32)[:, None]
    cols = (t % GRID_W).astype(F32)[:, None]
    lane = np.arange(LANES)
    grp = (lane // (2 * half)) % 2
    jj = lane % (2 * half)
    first = jj < half
    real = lane < n_real
    inv = ROPE_THETA ** (-jnp.asarray(jj % half, F32) / half)
    ang = jnp.where(jnp.asarray(grp == 0)[None, :], rows, cols) * inv[None, :]
    cos, sin = jnp.cos(ang), jnp.sin(ang)
    real_j, first_j = jnp.asarray(real)[None, :], jnp.asarray(first)[None, :]
    cos_t = jnp.where(real_j, cos, 1.0)
    sa_t = jnp.where(real_j & first_j, -sin, 0.0)
    sb_t = jnp.where(real_j & (~first_j), sin, 0.0)
    pad = lambda a, v: jnp.concatenate([a, jnp.full((n_ctx, LANES), v, F32)], axis=0)
    return pad(cos_t, 1.0), pad(sa_t, 0.0), pad(sb_t, 0.0)


def _rope(v, cos, sa, sb, half):
    return v * cos + pltpu.roll(v, LANES - half, 1) * sa + pltpu.roll(v, half, 1) * sb


def _mla_q_kernel(cq_ref, g_ref, w_ref, cos_ref, sa_ref, sb_ref, o_ref):
    n = _rms(cq_ref[...], g_ref[...]).astype(BF16)
    a = jnp.dot(n, w_ref[...], preferred_element_type=F32)
    cos, sa, sb = cos_ref[...], sa_ref[...], sb_ref[...]
    for h in range(MLA_HEADS):
        lo = 2 * LANES * h
        o_ref[:, lo:lo + LANES] = (a[:, lo:lo + LANES] * MLA_SCALE).astype(BF16)
        r = _rope(a[:, lo + LANES:lo + 2 * LANES], cos, sa, sb, MLA_ROPE // 4)
        o_ref[:, lo + LANES:lo + 2 * LANES] = (r * MLA_SCALE).astype(BF16)


def _prep_tile(T):
    return 3 * ROW_TILE if T % (3 * ROW_TILE) == 0 else ROW_TILE


def _mla_q(p, g_qa, w_uq_ext, tabs):
    T = p.shape[0]
    tm = _prep_tile(T)
    tab = pl.BlockSpec((tm, LANES), lambda i: (i, 0))
    return pl.pallas_call(
        _mla_q_kernel,
        grid=(T // tm,),
        in_specs=[pl.BlockSpec((tm, MLA_Q_RANK), lambda i: (i, P_CQ // MLA_Q_RANK)),
                  pl.BlockSpec((1, MLA_Q_RANK), lambda i: (0, 0)),
                  pl.BlockSpec((MLA_Q_RANK, 2 * LANES * MLA_HEADS), lambda i: (0, 0)),
                  tab, tab, tab],
        out_specs=pl.BlockSpec((tm, 2 * LANES * MLA_HEADS), lambda i: (i, 0)),
        out_shape=jax.ShapeDtypeStruct((T, 2 * LANES * MLA_HEADS), BF16),
        compiler_params=_params(("arbitrary",)),
    )(p, g_qa.reshape(1, -1), w_uq_ext, *tabs)


def _mla_kv_kernel(ckv_ref, kr_ref, g_ref, w_ref, cos_ref, sa_ref, sb_ref, k_ref, v_ref):
    n = _rms(ckv_ref[...], g_ref[...]).astype(BF16)
    a = jnp.dot(n, w_ref[...], preferred_element_type=F32)
    kr = _rope(kr_ref[...], cos_ref[...], sa_ref[...], sb_ref[...], MLA_ROPE // 4).astype(BF16)
    ones = jnp.ones(kr.shape, BF16)
    for h in range(MLA_HEADS):
        lo = 2 * LANES * h
        k_ref[:, lo:lo + LANES] = a[:, lo:lo + LANES].astype(BF16)
        k_ref[:, lo + LANES:lo + 2 * LANES] = kr
        v_ref[:, lo:lo + LANES] = a[:, lo + LANES:lo + 2 * LANES].astype(BF16)
        v_ref[:, lo + LANES:lo + 2 * LANES] = ones


def _mla_kv(p, g_kva, w_ukv, tabs):
    T = p.shape[0]
    tm = _prep_tile(T)
    tab = pl.BlockSpec((tm, LANES), lambda i: (i, 0))
    wide = 2 * LANES * MLA_HEADS
    return pl.pallas_call(
        _mla_kv_kernel,
        grid=(T // tm,),
        in_specs=[pl.BlockSpec((tm, MLA_KV_RANK), lambda i: (i, P_CKV // MLA_KV_RANK)),
                  pl.BlockSpec((tm, LANES), lambda i: (i, P_KR // LANES)),
                  pl.BlockSpec((1, MLA_KV_RANK), lambda i: (0, 0)),
                  pl.BlockSpec((MLA_KV_RANK, wide), lambda i: (0, 0)),
                  tab, tab, tab],
        out_specs=[pl.BlockSpec((tm, wide), lambda i: (i, 0)),
                   pl.BlockSpec((tm, wide), lambda i: (i, 0))],
        out_shape=[jax.ShapeDtypeStruct((T, wide), BF16),
                   jax.ShapeDtypeStruct((T, wide), BF16)],
        compiler_params=_params(("arbitrary",)),
    )(p, p, g_kva.reshape(1, -1), w_ukv, *tabs)


def _flash_kernel(q_ref, k_ref, v_ref, *rest):
    o_ref, m_sc, acc_sc = rest[-3:]
    kk = pl.program_id(2)

    @pl.when(kk == 0)
    def _():
        m_sc[...] = jnp.full(m_sc.shape, -1e30, F32)
        acc_sc[...] = jnp.zeros(acc_sc.shape, F32)

    rc = q_ref.shape[0] // FLASH_SUBTILES
    subs = [slice(r * rc, (r + 1) * rc) for r in range(FLASH_SUBTILES)]
    scores = [lax.dot_general(q_ref[rows, :], k_ref[...], _NT, preferred_element_type=F32) for rows in subs]
    for rows, s in zip(subs, scores):
        m_prev = m_sc[rows, :]
        m_new = jnp.maximum(m_prev, jnp.max(s, axis=1, keepdims=True))
        alpha = jnp.exp(m_prev - m_new)
        p = jnp.exp((s - m_new).astype(BF16))
        acc_sc[rows, :] = alpha * acc_sc[rows, :] + jnp.dot(p, v_ref[...], preferred_element_type=F32)
        m_sc[rows, :] = m_new

    @pl.when(kk == pl.num_programs(2) - 1)
    def _():
        acc = acc_sc[...]
        o_ref[...] = (acc[:, :LANES] / acc[:, LANES:LANES + 1]).astype(o_ref.dtype)


def _flash(q, k, v, out_prev, n_rows, q_blk0, n_qblk, k_blk0, n_kblk, tq, tk):
    dq = 2 * LANES
    in_specs = [pl.BlockSpec((tq, dq), lambda h, i, kk: (i + q_blk0, h)),
                pl.BlockSpec((tk, dq), lambda h, i, kk: (kk + k_blk0, h)),
                pl.BlockSpec((tk, dq), lambda h, i, kk: (kk + k_blk0, h))]
    args = [q, k, v]
    aliases = {}
    if out_prev is not None:
        in_specs.append(pl.BlockSpec(memory_space=pl.ANY))
        args.append(out_prev)
        aliases = {3: 0}
    return pl.pallas_call(
        _flash_kernel,
        grid=(MLA_HEADS, n_qblk, n_kblk),
        in_specs=in_specs,
        out_specs=pl.BlockSpec((tq, LANES), lambda h, i, kk: (i + q_blk0, h)),
        out_shape=jax.ShapeDtypeStruct((n_rows, MLA_HEADS * LANES), BF16),
        scratch_shapes=[pltpu.VMEM((tq, 1), F32), pltpu.VMEM((tq, dq), F32)],
        input_output_aliases=aliases,
        compiler_params=_params(("arbitrary", "arbitrary", "arbitrary"), 48),
    )(*args)


def _gla_direction(q, k, v, glr, wg, bg, tri, last_row, st_ref, o_ref):
    z = jnp.dot(glr.astype(BF16), wg, preferred_element_type=F32) + bg
    la = (jnp.minimum(z, 0.0) - jnp.log(1.0 + jnp.exp(-jnp.abs(z)))) * (1.0 / GLA_GATE_TAU)
    ones = jnp.where(tri, 1.0, 0.0).astype(BF16)
    la_hi = la.astype(BF16)
    la_lo = (la - la_hi.astype(F32)).astype(BF16)
    b = (jnp.dot(ones, la_hi, preferred_element_type=F32)
         + jnp.dot(ones, la_lo, preferred_element_type=F32))
    b_tot = b[last_row:last_row + 1, :]
    qt = q * (GLA_DK ** -0.5) * jnp.exp(b)
    kt = k * jnp.exp(-b)
    kh = k * jnp.exp(b_tot - b)
    dec = jnp.exp(b_tot)
    for h in range(GLA_HEADS):
        ks = slice(GLA_DK * h, GLA_DK * (h + 1))
        vs = slice(GLA_DV * h, GLA_DV * (h + 1))
        qh = qt[:, ks].astype(BF16)
        vh = v[:, vs]
        st = st_ref[h]
        inter = lax.dot_general(qh, st.astype(BF16), _NT, preferred_element_type=F32)
        sc = lax.dot_general(qh, kt[:, ks].astype(BF16), _NT, preferred_element_type=F32)
        sc = jnp.where(tri, sc, 0.0)
        intra = jnp.dot(sc.astype(BF16), vh.astype(BF16), preferred_element_type=F32)
        o_ref[:, vs] = inter + intra
        st_ref[h] = st * dec[:, ks] + jnp.dot(vh.T.astype(BF16), kh[:, ks].astype(BF16),
                                               preferred_element_type=F32)


def _gla_kernel(qf, kf, vf, gf, qb, kb, vb, gb, wf, bf, wb, bb, of_ref, ob_ref, stf, stb):
    @pl.when(pl.program_id(0) == 0)
    def _():
        stf[...] = jnp.zeros(stf.shape, F32)
        stb[...] = jnp.zeros(stb.shape, F32)

    cs = qf.shape[0]
    t_i = lax.broadcasted_iota(jnp.int32, (cs, cs), 0)
    s_i = lax.broadcasted_iota(jnp.int32, (cs, cs), 1)
    _gla_direction(qf[...], kf[...], vf[...], gf[...], wf[...], bf[...], s_i <= t_i, cs - 1, stf, of_ref)
    _gla_direction(qb[...], kb[...], vb[...], gb[...], wb[...], bb[...], s_i >= t_i, 0, stb, ob_ref)


def _gla(p, n_lat, wf_ext, bf, wb_ext, bb):
    T = p.shape[0]
    cs = GLA_CHUNK
    nch = T // cs
    nlc = n_lat // cs
    fwd = lambda c: (c + nlc) % nch
    bwd = lambda c: nch - 1 - c
    dk, dv = GLA_HEADS * GLA_DK, GLA_HEADS * GLA_DV

    def specs(idx):
        return [pl.BlockSpec((cs, dk), lambda c: (idx(c), P_GQ // dk)),
                pl.BlockSpec((cs, dk), lambda c: (idx(c), P_GK // dk)),
                pl.BlockSpec((cs, dv), lambda c: (idx(c), P_GV // dv)),
                pl.BlockSpec((cs, LANES), lambda c: (idx(c), P_GLR // LANES))]

    wspec = pl.BlockSpec((LANES, dk), lambda c: (0, 0))
    bspec = pl.BlockSpec((1, dk), lambda c: (0, 0))
    return pl.pallas_call(
        _gla_kernel,
        grid=(nch,),
        in_specs=specs(fwd) + specs(bwd) + [wspec, bspec, wspec, bspec],
        out_specs=[pl.BlockSpec((cs, dv), lambda c: (fwd(c), 0)),
                   pl.BlockSpec((cs, dv), lambda c: (bwd(c), 0))],
        out_shape=[jax.ShapeDtypeStruct((T, dv), F32), jax.ShapeDtypeStruct((T, dv), F32)],
        scratch_shapes=[pltpu.VMEM((GLA_HEADS, GLA_DV, GLA_DK), F32),
                        pltpu.VMEM((GLA_HEADS, GLA_DV, GLA_DK), F32)],
        compiler_params=_params(("arbitrary",)),
    )(p, p, p, p, p, p, p, p, wf_ext, bf.reshape(1, dk), wb_ext, bb.reshape(1, dk))


def _gla_out_kernel(of_ref, ob_ref, gr_ref, g_ref, o_ref):
    o = of_ref[...] + ob_ref[...]
    gr = gr_ref[...]
    for h in range(GLA_HEADS):
        vs = slice(GLA_DV * h, GLA_DV * (h + 1))
        o_ref[:, vs] = (_rms(o[:, vs], g_ref[...]) * _silu(gr[:, vs])).astype(BF16)


def _gla_out(o_f, o_b, p, g_gla):
    T, dv = o_f.shape
    tm = _prep_tile(T)
    row = pl.BlockSpec((tm, dv), lambda i: (i, 0))
    return pl.pallas_call(
        _gla_out_kernel,
        grid=(T // tm,),
        in_specs=[row, row, pl.BlockSpec((tm, dv), lambda i: (i, P_GR // dv)),
                  pl.BlockSpec((1, GLA_DV), lambda i: (0, 0))],
        out_specs=row,
        out_shape=jax.ShapeDtypeStruct((T, dv), BF16),
        compiler_params=_params(("arbitrary",)),
    )(o_f, o_b, p, g_gla.reshape(1, GLA_DV))


def _dft_cos_sin(n):
    k = np.arange(n)
    ang = 2.0 * np.pi * ((k[:, None] * k[None, :]) % n) / n
    return np.cos(ang), np.sin(ang)


def _channel_dft(xb, wd):
    zr, zi = [], []
    for g in range(FNET_GROUPS):
        z = jnp.dot(xb[:, LANES * g:LANES * (g + 1)], wd, preferred_element_type=F32)
        zr.append(z[:, :LANES])
        zi.append(z[:, LANES:])
    return jnp.concatenate(zr, axis=1), jnp.concatenate(zi, axis=1)


def _fft_a_kernel(n1, x_ref, wd_ref, m_ref, bre_ref, bim_ref):
    for n2 in range(FFT_N2):
        xs = x_ref[pl.ds(n2, n1, stride=FFT_N2), :].astype(BF16)
        z = jnp.dot(xs, wd_ref[...], preferred_element_type=F32)
        zs = jnp.concatenate([z[:, :LANES], z[:, LANES:]], axis=0).astype(BF16)
        b = jnp.dot(m_ref[n2], zs, preferred_element_type=F32)
        bre_ref[n2 * n1:(n2 + 1) * n1, :] = b[:n1]
        bim_ref[n2 * n1:(n2 + 1) * n1, :] = b[n1:]


def _fft_b_kernel(n1, scale, c_ref, s_ref, bre_ref, bim_ref, o_ref):
    grp = 8
    for t in range(n1 // grp):
        rows = [pl.ds(grp * t + r, FFT_N2, stride=n1) for r in range(grp)]
        br = jnp.concatenate([bre_ref[rw, :] for rw in rows], axis=1).astype(BF16)
        bi = jnp.concatenate([bim_ref[rw, :] for rw in rows], axis=1).astype(BF16)
        o = (jnp.dot(c_ref[...], br, preferred_element_type=F32)
             + jnp.dot(s_ref[...], bi, preferred_element_type=F32)) * scale
        for r in range(grp):
            o_ref[rows[r], :] = o[:, LANES * r:LANES * (r + 1)]


def _fft_ctx_kernel(scale, x_ref, wd_ref, cs_ref, prev_ref, o_ref):
    del prev_ref
    zr, zi = _channel_dft(x_ref[...].astype(BF16), wd_ref[...])
    z = jnp.concatenate([zr, zi], axis=0).astype(BF16)
    o_ref[...] = jnp.dot(cs_ref[...], z, preferred_element_type=F32) * scale


def _fourier(p, n_lat, n_ctx):
    T = p.shape[0]
    gw = FNET_GROUPS * LANES
    n1 = n_lat // FFT_N2
    cd, sd = _dft_cos_sin(LANES)
    as_bf16 = lambda a: jnp.asarray(a, F32).astype(BF16)
    wd = as_bf16(np.concatenate([cd, -sd], axis=1))
    k1 = jnp.arange(n1, dtype=jnp.int32)[None, :, None]
    pos = (FFT_N2 * jnp.arange(n1, dtype=jnp.int32)[None, None, :]
           + jnp.arange(FFT_N2, dtype=jnp.int32)[:, None, None])
    ang = (2.0 * np.pi / n_lat) * ((k1 * pos) % n_lat).astype(F32)
    gc, gs = jnp.cos(ang), jnp.sin(ang)
    m = jnp.concatenate([jnp.concatenate([gc, gs], axis=2),
                         jnp.concatenate([-gs, gc], axis=2)], axis=1).astype(BF16)
    col = lambda g: (0, g)
    bre, bim = pl.pallas_call(
        functools.partial(_fft_a_kernel, n1),
        grid=(FNET_GROUPS,),
        in_specs=[pl.BlockSpec((n_lat, LANES), lambda g: (0, P_FU // LANES + g)),
                  pl.BlockSpec((LANES, 2 * LANES), lambda g: (0, 0)),
                  pl.BlockSpec((FFT_N2, 2 * n1, 2 * n1), lambda g: (0, 0, 0))],
        out_specs=[pl.BlockSpec((n_lat, LANES), col)] * 2,
        out_shape=[jax.ShapeDtypeStruct((n_lat, gw), F32)] * 2,
        compiler_params=_params(("arbitrary",), 48),
    )(p, wd, m)

    c2, s2 = _dft_cos_sin(FFT_N2)
    mspec = pl.BlockSpec((FFT_N2, FFT_N2), lambda g: (0, 0))
    y = pl.pallas_call(
        functools.partial(_fft_b_kernel, n1, float((n_lat * LANES) ** -0.5)),
        grid=(FNET_GROUPS,),
        in_specs=[mspec, mspec, pl.BlockSpec((n_lat, LANES), col), pl.BlockSpec((n_lat, LANES), col)],
        out_specs=pl.BlockSpec((n_lat, LANES), col),
        out_shape=jax.ShapeDtypeStruct((T, gw), F32),
        compiler_params=_params(("arbitrary",)),
    )(as_bf16(c2), as_bf16(s2), bre, bim)

    cc, sc = _dft_cos_sin(n_ctx)
    return pl.pallas_call(
        functools.partial(_fft_ctx_kernel, float((n_ctx * LANES) ** -0.5)),
        grid=(1,),
        in_specs=[pl.BlockSpec((n_ctx, gw), lambda j: (n_lat // n_ctx, P_FU // gw)),
                  pl.BlockSpec((LANES, 2 * LANES), lambda j: (0, 0)),
                  pl.BlockSpec((n_ctx, 2 * n_ctx), lambda j: (0, 0)),
                  pl.BlockSpec(memory_space=pl.ANY)],
        out_specs=pl.BlockSpec((n_ctx, gw), lambda j: (n_lat // n_ctx, 0)),
        out_shape=jax.ShapeDtypeStruct((T, gw), F32),
        input_output_aliases={3: 0},
        compiler_params=_params(("arbitrary",)),
    )(p, wd, as_bf16(np.concatenate([cc, sc], axis=1)), y)


def _swa_prep_kernel(q_ref, k_ref, v_ref, cos_ref, sa_ref, sb_ref, qo_ref, ko_ref, vo_ref):
    cos, sa, sb = cos_ref[...], sa_ref[...], sb_ref[...]
    for h in range(SWA_HEADS):
        sl = slice(LANES * h, LANES * (h + 1))
        qo_ref[:, sl] = (_rope(q_ref[:, sl], cos, sa, sb, 32) * SWA_SCALE).astype(BF16)
    for h in range(SWA_KV_HEADS):
        sl = slice(LANES * h, LANES * (h + 1))
        ko_ref[:, sl] = _rope(k_ref[:, sl], cos, sa, sb, 32).astype(BF16)
    vo_ref[...] = v_ref[...].astype(BF16)


def _swa_prep(p, tabs):
    T = p.shape[0]
    tm = _prep_tile(T)
    qw, kw = SWA_HEADS * LANES, SWA_KV_HEADS * LANES
    tab = pl.BlockSpec((tm, LANES), lambda i: (i, 0))
    return pl.pallas_call(
        _swa_prep_kernel,
        grid=(T // tm,),
        in_specs=[pl.BlockSpec((tm, qw), lambda i: (i, P_SQ // qw)),
                  pl.BlockSpec((tm, kw), lambda i: (i, P_SK // kw)),
                  pl.BlockSpec((tm, kw), lambda i: (i, P_SV // kw)),
                  tab, tab, tab],
        out_specs=[pl.BlockSpec((tm, qw), lambda i: (i, 0)),
                   pl.BlockSpec((tm, kw), lambda i: (i, 0)),
                   pl.BlockSpec((tm, kw), lambda i: (i, 0))],
        out_shape=[jax.ShapeDtypeStruct((T, qw), BF16),
                   jax.ShapeDtypeStruct((T, kw), BF16),
                   jax.ShapeDtypeStruct((T, kw), BF16)],
        compiler_params=_params(("arbitrary",)),
    )(p, p, p, *tabs)


def _sink_attend(q, kcat, vcat, mask, sink):
    s = lax.dot_general(q, kcat, _NT, preferred_element_type=F32)
    if mask is not None:
        s = jnp.where(mask, s, -1e30)
    m = jnp.maximum(jnp.max(s, axis=1, keepdims=True), sink)
    pr = jnp.exp((s - m).astype(BF16))
    den = jnp.sum(pr.astype(F32), axis=1, keepdims=True) + jnp.exp(sink - m)
    return jnp.dot(pr, vcat, preferred_element_type=F32) / den


def _swa_lat_kernel(q_ref, kc_ref, kp_ref, k0_ref, kn_ref, vc_ref, vp_ref, v0_ref, vn_ref,
                    sink_ref, o_ref):
    kvh, nb = pl.program_id(0), pl.program_id(1)
    n_ctx = kc_ref.shape[0]
    qb = q_ref.shape[0]
    win = SWA_BLOCK
    kcat = jnp.concatenate([kc_ref[...], kp_ref[...], k0_ref[...], kn_ref[...]], axis=0)
    vcat = jnp.concatenate([vc_ref[...], vp_ref[...], v0_ref[...], vn_ref[...]], axis=0)
    n_keys = n_ctx + qb + 2 * win
    i = lax.broadcasted_iota(jnp.int32, (qb, n_keys), 0)
    j = lax.broadcasted_iota(jnp.int32, (qb, n_keys), 1)
    off = j - (n_ctx + win)
    in_band = jnp.abs(i - off) <= win
    exists = ((off >= 0) | (nb > 0)) & ((off < qb) | (nb < pl.num_programs(1) - 1))
    mask = (j < n_ctx) | (in_band & exists)
    for g in range(SWA_GROUP):
        sl = slice(LANES * g, LANES * (g + 1))
        sink = sink_ref[pl.ds(kvh * SWA_GROUP + g, 1), 0:1]
        o_ref[:, sl] = _sink_attend(q_ref[:, sl], kcat, vcat, mask, sink).astype(BF16)


def _swa_ctx_kernel(q_ref, k_ref, v_ref, sink_ref, prev_ref, o_ref):
    del prev_ref
    kvh = pl.program_id(0)
    for g in range(SWA_GROUP):
        sl = slice(LANES * g, LANES * (g + 1))
        sink = sink_ref[pl.ds(kvh * SWA_GROUP + g, 1), 0:1]
        o_ref[:, sl] = _sink_attend(q_ref[:, sl], k_ref[...], v_ref[...], None, sink).astype(BF16)


def _swa(qs, ks, vs, sink, n_lat, n_ctx):
    T = qs.shape[0]
    win = SWA_BLOCK
    qb = SWA_QBLOCK
    nb = n_lat // qb
    per = qb // win
    gq = SWA_GROUP * LANES
    sink2d = jnp.broadcast_to(sink.astype(F32)[:, None], (SWA_HEADS, LANES))
    cblk = n_lat // n_ctx
    ctx_spec = pl.BlockSpec((n_ctx, LANES), lambda h, b: (cblk, h))
    prev_spec = pl.BlockSpec((win, LANES), lambda h, b: (jnp.maximum(b * per - 1, 0), h))
    cur_spec = pl.BlockSpec((qb, LANES), lambda h, b: (b, h))
    next_spec = pl.BlockSpec((win, LANES), lambda h, b: (jnp.minimum((b + 1) * per, nb * per - 1), h))
    sink_spec = pl.BlockSpec((SWA_HEADS, LANES), lambda h, b: (0, 0))
    o_lat = pl.pallas_call(
        _swa_lat_kernel,
        grid=(SWA_KV_HEADS, nb),
        in_specs=[pl.BlockSpec((qb, gq), lambda h, b: (b, h)),
                  ctx_spec, prev_spec, cur_spec, next_spec,
                  ctx_spec, prev_spec, cur_spec, next_spec, sink_spec],
        out_specs=pl.BlockSpec((qb, gq), lambda h, b: (b, h)),
        out_shape=jax.ShapeDtypeStruct((T, SWA_HEADS * LANES), BF16),
        compiler_params=_params(("arbitrary", "arbitrary")),
    )(qs, ks, ks, ks, ks, vs, vs, vs, vs, sink2d)
    return pl.pallas_call(
        _swa_ctx_kernel,
        grid=(SWA_KV_HEADS,),
        in_specs=[pl.BlockSpec((n_ctx, gq), lambda h: (cblk, h)),
                  pl.BlockSpec((n_ctx, LANES), lambda h: (cblk, h)),
                  pl.BlockSpec((n_ctx, LANES), lambda h: (cblk, h)),
                  pl.BlockSpec((SWA_HEADS, LANES), lambda h: (0, 0)),
                  pl.BlockSpec(memory_space=pl.ANY)],
        out_specs=pl.BlockSpec((n_ctx, gq), lambda h: (cblk, h)),
        out_shape=jax.ShapeDtypeStruct((T, SWA_HEADS * LANES), BF16),
        input_output_aliases={4: 0},
        compiler_params=_params(("arbitrary",)),
    )(qs, ks, vs, sink2d, o_lat)


GATE_HALO = 8
GATE_TILE = 64


def _gate_kernel(tm, n_lat, n_tot, ug_ref, ua_ref, gp_ref, gn_ref, ap_ref, an_ref, cwg_ref, cwa_ref,
                 cbg_ref, cba_ref, o_ref, ubuf):
    i = pl.program_id(0)
    first = (i == 0) | (i * tm == n_lat)
    last = ((i + 1) * tm == n_lat) | ((i + 1) * tm == n_tot)
    ng = tm // GATE_HALO
    width = ug_ref.shape[1]
    sub = lax.broadcasted_iota(jnp.int32, (1, GATE_HALO, width), 1)

    def conv(u_ref, p_ref, n_ref, cw_ref, cb_ref):
        ubuf[0] = jnp.where(first, 0.0, p_ref[...])
        ubuf[1:1 + ng] = u_ref[...].reshape(ng, GATE_HALO, width)
        ubuf[1 + ng] = jnp.where(last, 0.0, n_ref[...])
        ub = ubuf[...]
        down = pltpu.roll(ub, 1, 1)
        upw = pltpu.roll(ub, GATE_HALO - 1, 1)
        um = jnp.where(sub == 0, down[0:ng], down[1:1 + ng])
        up = jnp.where(sub == GATE_HALO - 1, upw[2:2 + ng], upw[1:1 + ng])
        return um * cw_ref[0:1, :] + ub[1:1 + ng] * cw_ref[1:2, :] + up * cw_ref[2:3, :] + cb_ref[...]

    g = conv(ug_ref, gp_ref, gn_ref, cwg_ref, cbg_ref)
    a = conv(ua_ref, ap_ref, an_ref, cwa_ref, cba_ref)
    o_ref[...] = (_silu(g) * a).reshape(tm, width).astype(BF16)


def _gate(u, n_lat, conv_w, conv_b):
    T = u.shape[0]
    dff = u.shape[1] // 2
    tm = GATE_TILE
    hb = tm // GATE_HALO
    last_hb = T // GATE_HALO - 1
    cb2 = conv_b.reshape(1, 2 * dff)
    prev = lambda i: jnp.maximum(i * hb - 1, 0)
    nxt = lambda i: jnp.minimum((i + 1) * hb, last_hb)
    return pl.pallas_call(
        functools.partial(_gate_kernel, tm, n_lat, T),
        grid=(T // tm,),
        in_specs=[pl.BlockSpec((tm, dff), lambda i: (i, 0)),
                  pl.BlockSpec((tm, dff), lambda i: (i, 1)),
                  pl.BlockSpec((GATE_HALO, dff), lambda i: (prev(i), 0)),
                  pl.BlockSpec((GATE_HALO, dff), lambda i: (nxt(i), 0)),
                  pl.BlockSpec((GATE_HALO, dff), lambda i: (prev(i), 1)),
                  pl.BlockSpec((GATE_HALO, dff), lambda i: (nxt(i), 1)),
                  pl.BlockSpec((3, dff), lambda i: (0, 0)),
                  pl.BlockSpec((3, dff), lambda i: (0, 1)),
                  pl.BlockSpec((1, dff), lambda i: (0, 0)),
                  pl.BlockSpec((1, dff), lambda i: (0, 1))],
        out_specs=pl.BlockSpec((tm, dff), lambda i: (i, 0)),
        out_shape=jax.ShapeDtypeStruct((T, dff), BF16),
        scratch_shapes=[pltpu.VMEM((tm // GATE_HALO + 2, GATE_HALO, dff), F32)],
        compiler_params=_params(("arbitrary",), 48),
    )(u, u, u, u, u, u, conv_w, conv_w, cb2, cb2)


def _pad_cols(w, width):
    return jnp.pad(w, ((0, 0), (0, width - w.shape[1])))


def _w_in_ext(w_in):
    cq, ckv, kr, gq, gk, gv, gr, glr, fu, sq, sk, sv = jnp.split(
        w_in, np.cumsum([768, 512, 64, 512, 512, 1024, 1024, 32, 1024, 1024, 256]).tolist(), axis=2)
    pad = lambda w: jnp.pad(w, ((0, 0), (0, 0), (0, LANES - w.shape[2])))
    parts = [fu, gv, gr, cq, sk, sq, ckv, gq, gk, sv, pad(kr), pad(glr)]
    return jnp.concatenate(parts, axis=2).astype(BF16)


def _tile_for(total, parts, mult=16):
    t = total // parts
    assert t * parts == total and t % mult == 0, (total, parts)
    return t


def _key_tile(total):
    best = LANES
    for t in range(LANES, 1536 + 1, LANES):
        if total % t == 0:
            best = t
    return best


def kernel(x, c, ctx, c_ctx, w_ada, b_ada, g_pre_mix, g_post_mix, g_pre_ffn, g_post_ffn, w_in, g_qa, w_uq,
           g_kva, w_ukv, w_gate_f, b_gate_f, w_gate_b, b_gate_b, g_gla, swa_sink, w_out, w_up, conv_w, conv_b,
           w_down):
    B, N, D = x.shape
    C = ctx.shape[1]
    T = N + C
    L = w_ada.shape[0]
    dff = w_down.shape[1]
    assert B == 1 and D == 4096 and N % (GRID_W * 16) == 0 and C % ROW_TILE == 0 and N % C == 0
    x2d, ctx2d = x[0], ctx[0]

    cc = jnp.zeros((8, D), F32).at[0].set(c[0]).at[1].set(c_ctx)
    mod = _ada(cc, w_ada, b_ada)
    mla_tabs = _rope_tables(N, C, MLA_ROPE, MLA_ROPE // 4)
    swa_tabs = _rope_tables(N, C, LANES, 32)

    tm_mm = _tile_for(T, 4)
    tm_down = _tile_for(T, 16)

    w_in_b = _w_in_ext(w_in)
    w_down_b = w_down.astype(BF16)
    tq, tk = _tile_for(N, 2), _key_tile(T)

    h = _rms_mod(x2d, ctx2d, g_pre_mix[0], mod[0], 0)
    xs = (x2d, ctx2d)
    for l in range(L):
        last = l == L - 1
        p = _matmul([h], w_in_b, l, tm_mm, 512, F32, True)

        w_uq_ext = jnp.pad(w_uq[l].reshape(MLA_Q_RANK, MLA_HEADS, MLA_NOPE + MLA_ROPE),
                           ((0, 0), (0, 0), (0, 2 * LANES - MLA_NOPE - MLA_ROPE))
                           ).reshape(MLA_Q_RANK, 2 * LANES * MLA_HEADS).astype(BF16)
        q_a = _mla_q(p, g_qa[l], w_uq_ext, mla_tabs)
        k_a, v_a = _mla_kv(p, g_kva[l], w_ukv[l].astype(BF16), mla_tabs)
        o_a = _flash(q_a, k_a, v_a, None, T, 0, N // tq, 0, T // tk, tq, tk)
        o_a = _flash(q_a, k_a, v_a, o_a, T, N // C, 1, N // C, 1, C, C)

        wf_ext = jnp.zeros((LANES, GLA_HEADS * GLA_DK), F32).at[:GLA_GATE_RANK].set(w_gate_f[l]).astype(BF16)
        wb_ext = jnp.zeros((LANES, GLA_HEADS * GLA_DK), F32).at[GLA_GATE_RANK:2 * GLA_GATE_RANK].set(
            w_gate_b[l]).astype(BF16)
        o_f, o_bk = _gla(p, N, wf_ext, b_gate_f[l], wb_ext, b_gate_b[l])
        o_b = _gla_out(o_f, o_bk, p, g_gla[l])

        o_c = _fourier(p, N, C)

        qs, ks, vs = _swa_prep(p, swa_tabs)
        o_d = _swa(qs, ks, vs, swa_sink[l], N, C)

        y = _matmul([o_a, o_b, o_c, o_d], w_out, l, tm_mm, 512, F32, True)
        x1, h2 = _resid(xs, y, g_post_mix[l], mod[l], 2, N, T, pre=(g_pre_ffn[l], mod[l], 3))
        u = _matmul([h2], w_up, l, tm_mm, 512, F32, True)
        act = _gate(u, N, conv_w[l], conv_b[l])
        f = _matmul([act], w_down_b, l, tm_down, 512, F32)
        if last:
            (x_out,) = _resid((x1,), f, g_post_ffn[l], mod[l], 5, N, N)
            return x_out.reshape(1, N, D)
        x2, h = _resid((x1,), f, g_post_ffn[l], mod[l], 5, N, T, pre=(g_pre_mix[l + 1], mod[l + 1], 0))
        xs = (x2,)
```

```python
import functools

import numpy as np
import jax
import jax.numpy as jnp
from jax import lax
from jax.experimental import pallas as pl
from jax.experimental.pallas import tpu as pltpu

F32 = jnp.float32
BF16 = jnp.bfloat16

GRID_W = 64
EPS = 1e-6
ROPE_THETA = 10000.0
MLA_HEADS = 8
MLA_Q_RANK = 768
MLA_KV_RANK = 512
MLA_NOPE = 128
MLA_ROPE = 64
MLA_SCALE = (MLA_NOPE + MLA_ROPE) ** -0.5
GLA_HEADS = 4
GLA_DK = 128
GLA_DV = 256
GLA_GATE_RANK = 16
GLA_GATE_TAU = 16.0
GLA_CHUNK = 128
FNET_GROUPS = 8
FFT_N2 = 64
SWA_HEADS = 8
SWA_KV_HEADS = 2
SWA_GROUP = 4
SWA_BLOCK = 128
SWA_QBLOCK = 512
FLASH_SUBTILES = 8
SWA_SCALE = 128 ** -0.5
LANES = 128
ROW_TILE = 256

P_FU, P_GV, P_GR, P_CQ, P_SK, P_SQ = 0, 1024, 2048, 3072, 3840, 4096
P_CKV, P_GQ, P_GK, P_SV, P_KR, P_GLR = 5120, 5632, 6144, 6656, 6912, 7040
P_WIDTH = 7168

_NT = (((1,), (1,)), ((), ()))
_MIB = 1024 * 1024


def _params(sem, vmem_mib=40):
    return pltpu.CompilerParams(dimension_semantics=sem, vmem_limit_bytes=vmem_mib * _MIB)


def _silu(v):
    return v / (1.0 + jnp.exp(-v))


def _rms(v, g):
    return v * lax.rsqrt(jnp.mean(v * v, axis=-1, keepdims=True) + EPS) * g


def _pick(is_ctx, ref):
    return jnp.where(is_ctx, ref[1:2, :], ref[0:1, :])


def _ada_kernel(c_ref, w_ref, b_ref, o_ref):
    a = _silu(c_ref[...]).astype(BF16)
    o_ref[...] = jnp.dot(a, w_ref[...].astype(BF16), preferred_element_type=F32) + b_ref[...]


def _ada(cc, w_ada, b_ada):
    L, D, N6 = w_ada.shape
    tn = 512
    return pl.pallas_call(
        _ada_kernel,
        grid=(L, N6 // tn),
        in_specs=[pl.BlockSpec((8, D), lambda l, j: (0, 0)),
                  pl.BlockSpec((None, D, tn), lambda l, j: (l, 0, j)),
                  pl.BlockSpec((None, 1, tn), lambda l, j: (l, 0, j))],
        out_specs=pl.BlockSpec((None, 8, tn), lambda l, j: (l, 0, j)),
        out_shape=jax.ShapeDtypeStruct((L, 8, N6), F32),
        compiler_params=_params(("arbitrary", "arbitrary")),
    )(cc, w_ada, b_ada.reshape(L, 1, N6))


def _rms_mod_kernel(n_lat_tiles, x_ref, c_ref, g_ref, sh_ref, sc_ref, h_ref):
    is_ctx = pl.program_id(0) >= n_lat_tiles
    xv = jnp.where(is_ctx, c_ref[...], x_ref[...])
    y = _rms(xv, g_ref[...])
    h_ref[...] = (y * (1.0 + _pick(is_ctx, sc_ref)) + _pick(is_ctx, sh_ref)).astype(BF16)


def _rms_mod(x2d, ctx2d, g, mod, k_shift):
    N, D = x2d.shape
    C = ctx2d.shape[0]
    tm = ROW_TILE
    nl, nc = N // tm, C // tm
    return pl.pallas_call(
        functools.partial(_rms_mod_kernel, nl),
        grid=(nl + nc,),
        in_specs=[pl.BlockSpec((tm, D), lambda i: (jnp.minimum(i, nl - 1), 0)),
                  pl.BlockSpec((tm, D), lambda i: (jnp.maximum(i - nl, 0), 0)),
                  pl.BlockSpec((1, D), lambda i: (0, 0)),
                  pl.BlockSpec((8, D), lambda i: (0, k_shift)),
                  pl.BlockSpec((8, D), lambda i: (0, k_shift + 1))],
        out_specs=pl.BlockSpec((tm, D), lambda i: (i, 0)),
        out_shape=jax.ShapeDtypeStruct((N + C, D), BF16),
        compiler_params=_params(("arbitrary",)),
    )(x2d, ctx2d, g.reshape(1, D), mod, mod)


def _resid_kernel(n_lat_tiles, split_x, with_h, *refs):
    refs = list(refs)
    x_ref = refs.pop(0)
    c_ref = refs.pop(0) if split_x else None
    y_ref, gpost_ref, gate_ref = refs.pop(0), refs.pop(0), refs.pop(0)
    if with_h:
        gpre_ref, sh_ref, sc_ref = refs.pop(0), refs.pop(0), refs.pop(0)
    x1_ref = refs.pop(0)
    is_ctx = pl.program_id(0) >= n_lat_tiles
    xv = jnp.where(is_ctx, c_ref[...], x_ref[...]) if split_x else x_ref[...]
    x1 = xv + _pick(is_ctx, gate_ref) * _rms(y_ref[...], gpost_ref[...])
    x1_ref[...] = x1
    if with_h:
        h_ref = refs.pop(0)
        n = _rms(x1, gpre_ref[...])
        h_ref[...] = (n * (1.0 + _pick(is_ctx, sc_ref)) + _pick(is_ctx, sh_ref)).astype(BF16)


def _resid(xs, y, g_post, mod_gate, k_gate, n_lat, n_rows, pre=None):
    D = y.shape[1]
    tm = ROW_TILE
    nl = n_lat // tm
    nt = n_rows // tm
    split_x = len(xs) == 2
    row = lambda i: (i, 0)
    vec = lambda i: (0, 0)
    if split_x:
        in_specs = [pl.BlockSpec((tm, D), lambda i: (jnp.minimum(i, nl - 1), 0)),
                    pl.BlockSpec((tm, D), lambda i: (jnp.maximum(i - nl, 0), 0))]
    else:
        in_specs = [pl.BlockSpec((tm, D), row)]
    in_specs += [pl.BlockSpec((tm, D), row), pl.BlockSpec((1, D), vec),
                 pl.BlockSpec((8, D), lambda i: (0, k_gate))]
    args = list(xs) + [y, g_post.reshape(1, D), mod_gate]
    out_specs = [pl.BlockSpec((tm, D), row)]
    out_shape = [jax.ShapeDtypeStruct((n_rows, D), F32)]
    if pre is not None:
        g_pre, mod_pre, k_shift = pre
        in_specs += [pl.BlockSpec((1, D), vec),
                     pl.BlockSpec((8, D), lambda i: (0, k_shift)),
                     pl.BlockSpec((8, D), lambda i: (0, k_shift + 1))]
        args += [g_pre.reshape(1, D), mod_pre, mod_pre]
        out_specs.append(pl.BlockSpec((tm, D), row))
        out_shape.append(jax.ShapeDtypeStruct((n_rows, D), BF16))
    return pl.pallas_call(
        functools.partial(_resid_kernel, nl, split_x, pre is not None),
        grid=(nt,), in_specs=in_specs, out_specs=out_specs, out_shape=out_shape,
        compiler_params=_params(("arbitrary",)),
    )(*args)


def _mm_kernel(n_in, *refs):
    w_ref, o_ref = refs[n_in], refs[n_in + 1]
    kc = w_ref.shape[0] // n_in
    acc = None
    for k in range(n_in):
        d = jnp.dot(refs[k][...].astype(BF16), w_ref[kc * k:kc * (k + 1), :].astype(BF16),
                    preferred_element_type=F32)
        acc = d if acc is None else acc + d
    o_ref[...] = acc.astype(o_ref.dtype)


def _matmul(a_list, w3, layer, tm, tn, out_dtype, single_buffer_rows=False, m_rows=None):
    M = a_list[0].shape[0] if m_rows is None else m_rows
    _, K, Nn = w3.shape
    kc = K // len(a_list)
    mode = dict(pipeline_mode=pl.Buffered(1)) if single_buffer_rows else {}
    return pl.pallas_call(
        functools.partial(_mm_kernel, len(a_list)),
        grid=(M // tm, Nn // tn),
        in_specs=[pl.BlockSpec((tm, kc), lambda i, j: (i, 0), **mode) for _ in a_list]
        + [pl.BlockSpec((None, K, tn), lambda i, j: (layer, 0, j))],
        out_specs=pl.BlockSpec((tm, tn), lambda i, j: (i, j)),
        out_shape=jax.ShapeDtypeStruct((M, Nn), out_dtype),
        compiler_params=_params(("arbitrary", "arbitrary"), 56),
    )(*a_list, w3)


def _rope_tables(n_lat, n_ctx, n_real, half):
    t = jnp.arange(n_lat, dtype=jnp.int32)
    rows = (t // GRID_W).astype(F32)[:, None]
    cols = (t % GRID_W).astype(F32)[:, None]
    lane = np.arange(LANES)
    grp = (lane // (2 * half)) % 2
    jj = lane % (2 * half)
    first = jj < half
    real = lane < n_real
    inv = ROPE_THETA ** (-jnp.asarray(jj % half, F32) / half)
    ang = jnp.where(jnp.asarray(grp == 0)[None, :], rows, cols) * inv[None, :]
    cos, sin = jnp.cos(ang), jnp.sin(ang)
    real_j, first_j = jnp.asarray(real)[None, :], jnp.asarray(first)[None, :]
    cos_t = jnp.where(real_j, cos, 1.0)
    sa_t = jnp.where(real_j & first_j, -sin, 0.0)
    sb_t = jnp.where(real_j & (~first_j), sin, 0.0)
    pad = lambda a, v: jnp.concatenate([a, jnp.full((n_ctx, LANES), v, F32)], axis=0)
    return pad(cos_t, 1.0), pad(sa_t, 0.0), pad(sb_t, 0.0)


def _rope(v, cos, sa, sb, half):
    return v * cos + pltpu.roll(v, LANES - half, 1) * sa + pltpu.roll(v, half, 1) * sb


def _mla_q_kernel(cq_ref, g_ref, w_ref, cos_ref, sa_ref, sb_ref, o_ref):
    n = _rms(cq_ref[...], g_ref[...]).astype(BF16)
    a = jnp.dot(n, w_ref[...], preferred_element_type=F32)
    cos, sa, sb = cos_ref[...], sa_ref[...], sb_ref[...]
    for h in range(MLA_HEADS):
        lo = 2 * LANES * h
        o_ref[:, lo:lo + LANES] = (a[:, lo:lo + LANES] * MLA_SCALE).astype(BF16)
        r = _rope(a[:, lo + LANES:lo + 2 * LANES], cos, sa, sb, MLA_ROPE // 4)
        o_ref[:, lo + LANES:lo + 2 * LANES] = (r * MLA_SCALE).astype(BF16)


def _prep_tile(T):
    return 3 * ROW_TILE if T % (3 * ROW_TILE) == 0 else ROW_TILE


def _mla_q(p, g_qa, w_uq_ext, tabs):
    T = p.shape[0]
    tm = _prep_tile(T)
    tab = pl.BlockSpec((tm, LANES), lambda i: (i, 0))
    return pl.pallas_call(
        _mla_q_kernel,
        grid=(T // tm,),
        in_specs=[pl.BlockSpec((tm, MLA_Q_RANK), lambda i: (i, P_CQ // MLA_Q_RANK)),
                  pl.BlockSpec((1, MLA_Q_RANK), lambda i: (0, 0)),
                  pl.BlockSpec((MLA_Q_RANK, 2 * LANES * MLA_HEADS), lambda i: (0, 0)),
                  tab, tab, tab],
        out_specs=pl.BlockSpec((tm, 2 * LANES * MLA_HEADS), lambda i: (i, 0)),
        out_shape=jax.ShapeDtypeStruct((T, 2 * LANES * MLA_HEADS), BF16),
        compiler_params=_params(("arbitrary",)),
    )(p, g_qa.reshape(1, -1), w_uq_ext, *tabs)


def _mla_kv_kernel(ckv_ref, kr_ref, g_ref, w_ref, cos_ref, sa_ref, sb_ref, k_ref, v_ref):
    n = _rms(ckv_ref[...], g_ref[...]).astype(BF16)
    a = jnp.dot(n, w_ref[...], preferred_element_type=F32)
    kr = _rope(kr_ref[...], cos_ref[...], sa_ref[...], sb_ref[...], MLA_ROPE // 4).astype(BF16)
    ones = jnp.ones(kr.shape, BF16)
    for h in range(MLA_HEADS):
        lo = 2 * LANES * h
        k_ref[:, lo:lo + LANES] = a[:, lo:lo + LANES].astype(BF16)
        k_ref[:, lo + LANES:lo + 2 * LANES] = kr
        v_ref[:, lo:lo + LANES] = a[:, lo + LANES:lo + 2 * LANES].astype(BF16)
        v_ref[:, lo + LANES:lo + 2 * LANES] = ones


def _mla_kv(p, g_kva, w_ukv, tabs):
    T = p.shape[0]
    tm = _prep_tile(T)
    tab = pl.BlockSpec((tm, LANES), lambda i: (i, 0))
    wide = 2 * LANES * MLA_HEADS
    return pl.pallas_call(
        _mla_kv_kernel,
        grid=(T // tm,),
        in_specs=[pl.BlockSpec((tm, MLA_KV_RANK), lambda i: (i, P_CKV // MLA_KV_RANK)),
                  pl.BlockSpec((tm, LANES), lambda i: (i, P_KR // LANES)),
                  pl.BlockSpec((1, MLA_KV_RANK), lambda i: (0, 0)),
                  pl.BlockSpec((MLA_KV_RANK, wide), lambda i: (0, 0)),
                  tab, tab, tab],
        out_specs=[pl.BlockSpec((tm, wide), lambda i: (i, 0)),
                   pl.BlockSpec((tm, wide), lambda i: (i, 0))],
        out_shape=[jax.ShapeDtypeStruct((T, wide), BF16),
                   jax.ShapeDtypeStruct((T, wide), BF16)],
        compiler_params=_params(("arbitrary",)),
    )(p, p, g_kva.reshape(1, -1), w_ukv, *tabs)


def _flash_kernel(q_ref, k_ref, v_ref, *rest):
    o_ref, m_sc, acc_sc = rest[-3:]
    kk = pl.program_id(2)

    @pl.when(kk == 0)
    def _():
        m_sc[...] = jnp.full(m_sc.shape, -1e30, F32)
        acc_sc[...] = jnp.zeros(acc_sc.shape, F32)

    rc = q_ref.shape[0] // FLASH_SUBTILES
    subs = [slice(r * rc, (r + 1) * rc) for r in range(FLASH_SUBTILES)]
    scores = [lax.dot_general(q_ref[rows, :], k_ref[...], _NT, preferred_element_type=F32) for rows in subs]
    for rows, s in zip(subs, scores):
        m_prev = m_sc[rows, :]
        m_new = jnp.maximum(m_prev, jnp.max(s, axis=1, keepdims=True))
        alpha = jnp.exp(m_prev - m_new)
        p = jnp.exp((s - m_new).astype(BF16))
        acc_sc[rows, :] = alpha * acc_sc[rows, :] + jnp.dot(p, v_ref[...], preferred_element_type=F32)
        m_sc[rows, :] = m_new

    @pl.when(kk == pl.num_programs(2) - 1)
    def _():
        acc = acc_sc[...]
        o_ref[...] = (acc[:, :LANES] / acc[:, LANES:LANES + 1]).astype(o_ref.dtype)


def _flash(q, k, v, out_prev, n_rows, q_blk0, n_qblk, k_blk0, n_kblk, tq, tk):
    dq = 2 * LANES
    in_specs = [pl.BlockSpec((tq, dq), lambda h, i, kk: (i + q_blk0, h)),
                pl.BlockSpec((tk, dq), lambda h, i, kk: (kk + k_blk0, h)),
                pl.BlockSpec((tk, dq), lambda h, i, kk: (kk + k_blk0, h))]
    args = [q, k, v]
    aliases = {}
    if out_prev is not None:
        in_specs.append(pl.BlockSpec(memory_space=pl.ANY))
        args.append(out_prev)
        aliases = {3: 0}
    return pl.pallas_call(
        _flash_kernel,
        grid=(MLA_HEADS, n_qblk, n_kblk),
        in_specs=in_specs,
        out_specs=pl.BlockSpec((tq, LANES), lambda h, i, kk: (i + q_blk0, h)),
        out_shape=jax.ShapeDtypeStruct((n_rows, MLA_HEADS * LANES), BF16),
        scratch_shapes=[pltpu.VMEM((tq, 1), F32), pltpu.VMEM((tq, dq), F32)],
        input_output_aliases=aliases,
        compiler_params=_params(("arbitrary", "arbitrary", "arbitrary"), 48),
    )(*args)


def _gla_direction(q, k, v, glr, wg, bg, tri, last_row, st_ref, o_ref):
    z = jnp.dot(glr.astype(BF16), wg, preferred_element_type=F32) + bg
    la = (jnp.minimum(z, 0.0) - jnp.log(1.0 + jnp.exp(-jnp.abs(z)))) * (1.0 / GLA_GATE_TAU)
    ones = jnp.where(tri, 1.0, 0.0).astype(BF16)
    la_hi = la.astype(BF16)
    la_lo = (la - la_hi.astype(F32)).astype(BF16)
    b = (jnp.dot(ones, la_hi, preferred_element_type=F32)
         + jnp.dot(ones, la_lo, preferred_element_type=F32))
    b_tot = b[last_row:last_row + 1, :]
    qt = q * (GLA_DK ** -0.5) * jnp.exp(b)
    kt = k * jnp.exp(-b)
    kh = k * jnp.exp(b_tot - b)
    dec = jnp.exp(b_tot)
    for h in range(GLA_HEADS):
        ks = slice(GLA_DK * h, GLA_DK * (h + 1))
        vs = slice(GLA_DV * h, GLA_DV * (h + 1))
        qh = qt[:, ks].astype(BF16)
        vh = v[:, vs]
        st = st_ref[h]
        inter = lax.dot_general(qh, st.astype(BF16), _NT, preferred_element_type=F32)
        sc = lax.dot_general(qh, kt[:, ks].astype(BF16), _NT, preferred_element_type=F32)
        sc = jnp.where(tri, sc, 0.0)
        intra = jnp.dot(sc.astype(BF16), vh.astype(BF16), preferred_element_type=F32)
        o_ref[:, vs] = inter + intra
        st_ref[h] = st * dec[:, ks] + jnp.dot(vh.T.astype(BF16), kh[:, ks].astype(BF16),
                                               preferred_element_type=F32)


def _gla_kernel(qf, kf, vf, gf, qb, kb, vb, gb, wf, bf, wb, bb, of_ref, ob_ref, stf, stb):
    @pl.when(pl.program_id(0) == 0)
    def _():
        stf[...] = jnp.zeros(stf.shape, F32)
        stb[...] = jnp.zeros(stb.shape, F32)

    cs = qf.shape[0]
    t_i = lax.broadcasted_iota(jnp.int32, (cs, cs), 0)
    s_i = lax.broadcasted_iota(jnp.int32, (cs, cs), 1)
    _gla_direction(qf[...], kf[...], vf[...], gf[...], wf[...], bf[...], s_i <= t_i, cs - 1, stf, of_ref)
    _gla_direction(qb[...], kb[...], vb[...], gb[...], wb[...], bb[...], s_i >= t_i, 0, stb, ob_ref)


def _gla(p, n_lat, wf_ext, bf, wb_ext, bb):
    T = p.shape[0]
    cs = GLA_CHUNK
    nch = T // cs
    nlc = n_lat // cs
    fwd = lambda c: (c + nlc) % nch
    bwd = lambda c: nch - 1 - c
    dk, dv = GLA_HEADS * GLA_DK, GLA_HEADS * GLA_DV

    def specs(idx):
        return [pl.BlockSpec((cs, dk), lambda c: (idx(c), P_GQ // dk)),
                pl.BlockSpec((cs, dk), lambda c: (idx(c), P_GK // dk)),
                pl.BlockSpec((cs, dv), lambda c: (idx(c), P_GV // dv)),
                pl.BlockSpec((cs, LANES), lambda c: (idx(c), P_GLR // LANES))]

    wspec = pl.BlockSpec((LANES, dk), lambda c: (0, 0))
    bspec = pl.BlockSpec((1, dk), lambda c: (0, 0))
    return pl.pallas_call(
        _gla_kernel,
        grid=(nch,),
        in_specs=specs(fwd) + specs(bwd) + [wspec, bspec, wspec, bspec],
        out_specs=[pl.BlockSpec((cs, dv), lambda c: (fwd(c), 0)),
                   pl.BlockSpec((cs, dv), lambda c: (bwd(c), 0))],
        out_shape=[jax.ShapeDtypeStruct((T, dv), F32), jax.ShapeDtypeStruct((T, dv), F32)],
        scratch_shapes=[pltpu.VMEM((GLA_HEADS, GLA_DV, GLA_DK), F32),
                        pltpu.VMEM((GLA_HEADS, GLA_DV, GLA_DK), F32)],
        compiler_params=_params(("arbitrary",)),
    )(p, p, p, p, p, p, p, p, wf_ext, bf.reshape(1, dk), wb_ext, bb.reshape(1, dk))


def _gla_out_kernel(of_ref, ob_ref, gr_ref, g_ref, o_ref):
    o = of_ref[...] + ob_ref[...]
    gr = gr_ref[...]
    for h in range(GLA_HEADS):
        vs = slice(GLA_DV * h, GLA_DV * (h + 1))
        o_ref[:, vs] = (_rms(o[:, vs], g_ref[...]) * _silu(gr[:, vs])).astype(BF16)


def _gla_out(o_f, o_b, p, g_gla):
    T, dv = o_f.shape
    tm = _prep_tile(T)
    row = pl.BlockSpec((tm, dv), lambda i: (i, 0))
    return pl.pallas_call(
        _gla_out_kernel,
        grid=(T // tm,),
        in_specs=[row, row, pl.BlockSpec((tm, dv), lambda i: (i, P_GR // dv)),
                  pl.BlockSpec((1, GLA_DV), lambda i: (0, 0))],
        out_specs=row,
        out_shape=jax.ShapeDtypeStruct((T, dv), BF16),
        compiler_params=_params(("arbitrary",)),
    )(o_f, o_b, p, g_gla.reshape(1, GLA_DV))


def _dft_cos_sin(n):
    k = np.arange(n)
    ang = 2.0 * np.pi * ((k[:, None] * k[None, :]) % n) / n
    return np.cos(ang), np.sin(ang)


def _channel_dft(xb, wd):
    zr, zi = [], []
    for g in range(FNET_GROUPS):
        z = jnp.dot(xb[:, LANES * g:LANES * (g + 1)], wd, preferred_element_type=F32)
        zr.append(z[:, :LANES])
        zi.append(z[:, LANES:])
    return jnp.concatenate(zr, axis=1), jnp.concatenate(zi, axis=1)


def _fft_a_kernel(n1, x_ref, wd_ref, m_ref, bre_ref, bim_ref):
    for n2 in range(FFT_N2):
        xs = x_ref[pl.ds(n2, n1, stride=FFT_N2), :].astype(BF16)
        z = jnp.dot(xs, wd_ref[...], preferred_element_type=F32)
        zs = jnp.concatenate([z[:, :LANES], z[:, LANES:]], axis=0).astype(BF16)
        b = jnp.dot(m_ref[n2], zs, preferred_element_type=F32)
        bre_ref[n2 * n1:(n2 + 1) * n1, :] = b[:n1]
        bim_ref[n2 * n1:(n2 + 1) * n1, :] = b[n1:]


def _fft_b_kernel(n1, scale, c_ref, s_ref, bre_ref, bim_ref, o_ref):
    grp = 8
    for t in range(n1 // grp):
        rows = [pl.ds(grp * t + r, FFT_N2, stride=n1) for r in range(grp)]
        br = jnp.concatenate([bre_ref[rw, :] for rw in rows], axis=1).astype(BF16)
        bi = jnp.concatenate([bim_ref[rw, :] for rw in rows], axis=1).astype(BF16)
        o = (jnp.dot(c_ref[...], br, preferred_element_type=F32)
             + jnp.dot(s_ref[...], bi, preferred_element_type=F32)) * scale
        for r in range(grp):
            o_ref[rows[r], :] = o[:, LANES * r:LANES * (r + 1)]


def _fft_ctx_kernel(scale, x_ref, wd_ref, cs_ref, prev_ref, o_ref):
    del prev_ref
    zr, zi = _channel_dft(x_ref[...].astype(BF16), wd_ref[...])
    z = jnp.concatenate([zr, zi], axis=0).astype(BF16)
    o_ref[...] = jnp.dot(cs_ref[...], z, preferred_element_type=F32) * scale


def _fourier(p, n_lat, n_ctx):
    T = p.shape[0]
    gw = FNET_GROUPS * LANES
    n1 = n_lat // FFT_N2
    cd, sd = _dft_cos_sin(LANES)
    as_bf16 = lambda a: jnp.asarray(a, F32).astype(BF16)
    wd = as_bf16(np.concatenate([cd, -sd], axis=1))
    k1 = jnp.arange(n1, dtype=jnp.int32)[None, :, None]
    pos = (FFT_N2 * jnp.arange(n1, dtype=jnp.int32)[None, None, :]
           + jnp.arange(FFT_N2, dtype=jnp.int32)[:, None, None])
    ang = (2.0 * np.pi / n_lat) * ((k1 * pos) % n_lat).astype(F32)
    gc, gs = jnp.cos(ang), jnp.sin(ang)
    m = jnp.concatenate([jnp.concatenate([gc, gs], axis=2),
                         jnp.concatenate([-gs, gc], axis=2)], axis=1).astype(BF16)
    col = lambda g: (0, g)
    bre, bim = pl.pallas_call(
        functools.partial(_fft_a_kernel, n1),
        grid=(FNET_GROUPS,),
        in_specs=[pl.BlockSpec((n_lat, LANES), lambda g: (0, P_FU // LANES + g)),
                  pl.BlockSpec((LANES, 2 * LANES), lambda g: (0, 0)),
                  pl.BlockSpec((FFT_N2, 2 * n1, 2 * n1), lambda g: (0, 0, 0))],
        out_specs=[pl.BlockSpec((n_lat, LANES), col)] * 2,
        out_shape=[jax.ShapeDtypeStruct((n_lat, gw), F32)] * 2,
        compiler_params=_params(("arbitrary",), 48),
    )(p, wd, m)

    c2, s2 = _dft_cos_sin(FFT_N2)
    mspec = pl.BlockSpec((FFT_N2, FFT_N2), lambda g: (0, 0))
    y = pl.pallas_call(
        functools.partial(_fft_b_kernel, n1, float((n_lat * LANES) ** -0.5)),
        grid=(FNET_GROUPS,),
        in_specs=[mspec, mspec, pl.BlockSpec((n_lat, LANES), col), pl.BlockSpec((n_lat, LANES), col)],
        out_specs=pl.BlockSpec((n_lat, LANES), col),
        out_shape=jax.ShapeDtypeStruct((T, gw), F32),
        compiler_params=_params(("arbitrary",)),
    )(as_bf16(c2), as_bf16(s2), bre, bim)

    cc, sc = _dft_cos_sin(n_ctx)
    return pl.pallas_call(
        functools.partial(_fft_ctx_kernel, float((n_ctx * LANES) ** -0.5)),
        grid=(1,),
        in_specs=[pl.BlockSpec((n_ctx, gw), lambda j: (n_lat // n_ctx, P_FU // gw)),
                  pl.BlockSpec((LANES, 2 * LANES), lambda j: (0, 0)),
                  pl.BlockSpec((n_ctx, 2 * n_ctx), lambda j: (0, 0)),
                  pl.BlockSpec(memory_space=pl.ANY)],
        out_specs=pl.BlockSpec((n_ctx, gw), lambda j: (n_lat // n_ctx, 0)),
        out_shape=jax.ShapeDtypeStruct((T, gw), F32),
        input_output_aliases={3: 0},
        compiler_params=_params(("arbitrary",)),
    )(p, wd, as_bf16(np.concatenate([cc, sc], axis=1)), y)


def _swa_prep_kernel(q_ref, k_ref, v_ref, cos_ref, sa_ref, sb_ref, qo_ref, ko_ref, vo_ref):
    cos, sa, sb = cos_ref[...], sa_ref[...], sb_ref[...]
    for h in range(SWA_HEADS):
        sl = slice(LANES * h, LANES * (h + 1))
        qo_ref[:, sl] = (_rope(q_ref[:, sl], cos, sa, sb, 32) * SWA_SCALE).astype(BF16)
    for h in range(SWA_KV_HEADS):
        sl = slice(LANES * h, LANES * (h + 1))
        ko_ref[:, sl] = _rope(k_ref[:, sl], cos, sa, sb, 32).astype(BF16)
    vo_ref[...] = v_ref[...].astype(BF16)


def _swa_prep(p, tabs):
    T = p.shape[0]
    tm = _prep_tile(T)
    qw, kw = SWA_HEADS * LANES, SWA_KV_HEADS * LANES
    tab = pl.BlockSpec((tm, LANES), lambda i: (i, 0))
    return pl.pallas_call(
        _swa_prep_kernel,
        grid=(T // tm,),
        in_specs=[pl.BlockSpec((tm, qw), lambda i: (i, P_SQ // qw)),
                  pl.BlockSpec((tm, kw), lambda i: (i, P_SK // kw)),
                  pl.BlockSpec((tm, kw), lambda i: (i, P_SV // kw)),
                  tab, tab, tab],
        out_specs=[pl.BlockSpec((tm, qw), lambda i: (i, 0)),
                   pl.BlockSpec((tm, kw), lambda i: (i, 0)),
                   pl.BlockSpec((tm, kw), lambda i: (i, 0))],
        out_shape=[jax.ShapeDtypeStruct((T, qw), BF16),
                   jax.ShapeDtypeStruct((T, kw), BF16),
                   jax.ShapeDtypeStruct((T, kw), BF16)],
        compiler_params=_params(("arbitrary",)),
    )(p, p, p, *tabs)


def _sink_attend(q, kcat, vcat, mask, sink):
    s = lax.dot_general(q, kcat, _NT, preferred_element_type=F32)
    if mask is not None:
        s = jnp.where(mask, s, -1e30)
    m = jnp.maximum(jnp.max(s, axis=1, keepdims=True), sink)
    pr = jnp.exp((s - m).astype(BF16))
    den = jnp.sum(pr.astype(F32), axis=1, keepdims=True) + jnp.exp(sink - m)
    return jnp.dot(pr, vcat, preferred_element_type=F32) / den


def _swa_lat_kernel(q_ref, kc_ref, kp_ref, k0_ref, kn_ref, vc_ref, vp_ref, v0_ref, vn_ref,
                    sink_ref, o_ref):
    kvh, nb = pl.program_id(0), pl.program_id(1)
    n_ctx = kc_ref.shape[0]
    qb = q_ref.shape[0]
    win = SWA_BLOCK
    kcat = jnp.concatenate([kc_ref[...], kp_ref[...], k0_ref[...], kn_ref[...]], axis=0)
    vcat = jnp.concatenate([vc_ref[...], vp_ref[...], v0_ref[...], vn_ref[...]], axis=0)
    n_keys = n_ctx + qb + 2 * win
    i = lax.broadcasted_iota(jnp.int32, (qb, n_keys), 0)
    j = lax.broadcasted_iota(jnp.int32, (qb, n_keys), 1)
    off = j - (n_ctx + win)
    in_band = jnp.abs(i - off) <= win
    exists = ((off >= 0) | (nb > 0)) & ((off < qb) | (nb < pl.num_programs(1) - 1))
    mask = (j < n_ctx) | (in_band & exists)
    for g in range(SWA_GROUP):
        sl = slice(LANES * g, LANES * (g + 1))
        sink = sink_ref[pl.ds(kvh * SWA_GROUP + g, 1), 0:1]
        o_ref[:, sl] = _sink_attend(q_ref[:, sl], kcat, vcat, mask, sink).astype(BF16)


def _swa_ctx_kernel(q_ref, k_ref, v_ref, sink_ref, prev_ref, o_ref):
    del prev_ref
    kvh = pl.program_id(0)
    for g in range(SWA_GROUP):
        sl = slice(LANES * g, LANES * (g + 1))
        sink = sink_ref[pl.ds(kvh * SWA_GROUP + g, 1), 0:1]
        o_ref[:, sl] = _sink_attend(q_ref[:, sl], k_ref[...], v_ref[...], None, sink).astype(BF16)


def _swa(qs, ks, vs, sink, n_lat, n_ctx):
    T = qs.shape[0]
    win = SWA_BLOCK
    qb = SWA_QBLOCK
    nb = n_lat // qb
    per = qb // win
    gq = SWA_GROUP * LANES
    sink2d = jnp.broadcast_to(sink.astype(F32)[:, None], (SWA_HEADS, LANES))
    cblk = n_lat // n_ctx
    ctx_spec = pl.BlockSpec((n_ctx, LANES), lambda h, b: (cblk, h))
    prev_spec = pl.BlockSpec((win, LANES), lambda h, b: (jnp.maximum(b * per - 1, 0), h))
    cur_spec = pl.BlockSpec((qb, LANES), lambda h, b: (b, h))
    next_spec = pl.BlockSpec((win, LANES), lambda h, b: (jnp.minimum((b + 1) * per, nb * per - 1), h))
    sink_spec = pl.BlockSpec((SWA_HEADS, LANES), lambda h, b: (0, 0))
    o_lat = pl.pallas_call(
        _swa_lat_kernel,
        grid=(SWA_KV_HEADS, nb),
        in_specs=[pl.BlockSpec((qb, gq), lambda h, b: (b, h)),
                  ctx_spec, prev_spec, cur_spec, next_spec,
                  ctx_spec, prev_spec, cur_spec, next_spec, sink_spec],
        out_specs=pl.BlockSpec((qb, gq), lambda h, b: (b, h)),
        out_shape=jax.ShapeDtypeStruct((T, SWA_HEADS * LANES), BF16),
        compiler_params=_params(("arbitrary", "arbitrary")),
    )(qs, ks, ks, ks, ks, vs, vs, vs, vs, sink2d)
    return pl.pallas_call(
        _swa_ctx_kernel,
        grid=(SWA_KV_HEADS,),
        in_specs=[pl.BlockSpec((n_ctx, gq), lambda h: (cblk, h)),
                  pl.BlockSpec((n_ctx, LANES), lambda h: (cblk, h)),
                  pl.BlockSpec((n_ctx, LANES), lambda h: (cblk, h)),
                  pl.BlockSpec((SWA_HEADS, LANES), lambda h: (0, 0)),
                  pl.BlockSpec(memory_space=pl.ANY)],
        out_specs=pl.BlockSpec((n_ctx, gq), lambda h: (cblk, h)),
        out_shape=jax.ShapeDtypeStruct((T, SWA_HEADS * LANES), BF16),
        input_output_aliases={4: 0},
        compiler_params=_params(("arbitrary",)),
    )(qs, ks, vs, sink2d, o_lat)


GATE_TILE = 64
GATE_HALO = 32


def _gate_kernel(tm, n_lat, n_tot, ug_ref, ua_ref, gp_ref, gn_ref, ap_ref, an_ref, cwg_ref, cwa_ref,
                 cbg_ref, cba_ref, o_ref, ubuf):
    i = pl.program_id(0)
    first = (i == 0) | (i * tm == n_lat)
    last = ((i + 1) * tm == n_lat) | ((i + 1) * tm == n_tot)
    rows = tm + 2 * GATE_HALO
    out_r = lax.broadcasted_iota(jnp.int32, (2 * tm, rows), 0)
    src_r = lax.broadcasted_iota(jnp.int32, (2 * tm, rows), 1)
    want = jnp.where(out_r < tm, out_r + (GATE_HALO - 1), out_r - tm + (GATE_HALO + 1))
    pick = jnp.where(src_r == want, 1.0, 0.0).astype(BF16)

    def conv(u_ref, p_ref, n_ref, cw_ref, cb_ref):
        ubuf[0:GATE_HALO, :] = jnp.where(first, jnp.zeros_like(p_ref[...]), p_ref[...])
        ubuf[GATE_HALO:GATE_HALO + tm, :] = u_ref[...]
        ubuf[GATE_HALO + tm:, :] = jnp.where(last, jnp.zeros_like(n_ref[...]), n_ref[...])
        shifted = jnp.dot(pick, ubuf[...], preferred_element_type=F32)
        um, up = shifted[:tm], shifted[tm:]
        return (um * cw_ref[0:1, :] + u_ref[...].astype(F32) * cw_ref[1:2, :] + up * cw_ref[2:3, :]
                + cb_ref[...])

    g = conv(ug_ref, gp_ref, gn_ref, cwg_ref, cbg_ref)
    a = conv(ua_ref, ap_ref, an_ref, cwa_ref, cba_ref)
    o_ref[...] = (_silu(g) * a).astype(BF16)


def _gate(u, n_lat, conv_w, conv_b):
    T = u.shape[0]
    dff = u.shape[1] // 2
    tm = GATE_TILE
    hb = tm // GATE_HALO
    last_hb = T // GATE_HALO - 1
    cb2 = conv_b.reshape(1, 2 * dff)
    prev = lambda i: jnp.maximum(i * hb - 1, 0)
    nxt = lambda i: jnp.minimum((i + 1) * hb, last_hb)
    return pl.pallas_call(
        functools.partial(_gate_kernel, tm, n_lat, T),
        grid=(T // tm,),
        in_specs=[pl.BlockSpec((tm, dff), lambda i: (i, 0)),
                  pl.BlockSpec((tm, dff), lambda i: (i, 1)),
                  pl.BlockSpec((GATE_HALO, dff), lambda i: (prev(i), 0)),
                  pl.BlockSpec((GATE_HALO, dff), lambda i: (nxt(i), 0)),
                  pl.BlockSpec((GATE_HALO, dff), lambda i: (prev(i), 1)),
                  pl.BlockSpec((GATE_HALO, dff), lambda i: (nxt(i), 1)),
                  pl.BlockSpec((3, dff), lambda i: (0, 0)),
                  pl.BlockSpec((3, dff), lambda i: (0, 1)),
                  pl.BlockSpec((1, dff), lambda i: (0, 0)),
                  pl.BlockSpec((1, dff), lambda i: (0, 1))],
        out_specs=pl.BlockSpec((tm, dff), lambda i: (i, 0)),
        out_shape=jax.ShapeDtypeStruct((T, dff), BF16),
        scratch_shapes=[pltpu.VMEM((tm + 2 * GATE_HALO, dff), BF16)],
        compiler_params=_params(("arbitrary",), 48),
    )(u, u, u, u, u, u, conv_w, conv_w, cb2, cb2)


def _pad_cols(w, width):
    return jnp.pad(w, ((0, 0), (0, width - w.shape[1])))


def _w_in_ext(w_in):
    cq, ckv, kr, gq, gk, gv, gr, glr, fu, sq, sk, sv = jnp.split(
        w_in, np.cumsum([768, 512, 64, 512, 512, 1024, 1024, 32, 1024, 1024, 256]).tolist(), axis=2)
    pad = lambda w: jnp.pad(w, ((0, 0), (0, 0), (0, LANES - w.shape[2])))
    parts = [fu, gv, gr, cq, sk, sq, ckv, gq, gk, sv, pad(kr), pad(glr)]
    return jnp.concatenate(parts, axis=2).astype(BF16)


def _tile_for(total, parts, mult=16):
    t = total // parts
    assert t * parts == total and t % mult == 0, (total, parts)
    return t


def _key_tile(total):
    best = LANES
    for t in range(LANES, 1536 + 1, LANES):
        if total % t == 0:
            best = t
    return best


def kernel(x, c, ctx, c_ctx, w_ada, b_ada, g_pre_mix, g_post_mix, g_pre_ffn, g_post_ffn, w_in, g_qa, w_uq,
           g_kva, w_ukv, w_gate_f, b_gate_f, w_gate_b, b_gate_b, g_gla, swa_sink, w_out, w_up, conv_w, conv_b,
           w_down):
    B, N, D = x.shape
    C = ctx.shape[1]
    T = N + C
    L = w_ada.shape[0]
    dff = w_down.shape[1]
    assert B == 1 and D == 4096 and N % (GRID_W * 16) == 0 and C % ROW_TILE == 0 and N % C == 0
    x2d, ctx2d = x[0], ctx[0]

    cc = jnp.zeros((8, D), F32).at[0].set(c[0]).at[1].set(c_ctx)
    mod = _ada(cc, w_ada, b_ada)
    mla_tabs = _rope_tables(N, C, MLA_ROPE, MLA_ROPE // 4)
    swa_tabs = _rope_tables(N, C, LANES, 32)

    tm_mm = _tile_for(T, 4)

    w_in_b = _w_in_ext(w_in)
    w_down_b = w_down.astype(BF16)
    tq, tk = _tile_for(N, 2), _key_tile(T)

    h = _rms_mod(x2d, ctx2d, g_pre_mix[0], mod[0], 0)
    xs = (x2d, ctx2d)
    for l in range(L):
        last = l == L - 1
        p = _matmul([h], w_in_b, l, tm_mm, 512, F32, True)

        w_uq_ext = jnp.pad(w_uq[l].reshape(MLA_Q_RANK, MLA_HEADS, MLA_NOPE + MLA_ROPE),
                           ((0, 0), (0, 0), (0, 2 * LANES - MLA_NOPE - MLA_ROPE))
                           ).reshape(MLA_Q_RANK, 2 * LANES * MLA_HEADS).astype(BF16)
        q_a = _mla_q(p, g_qa[l], w_uq_ext, mla_tabs)
        k_a, v_a = _mla_kv(p, g_kva[l], w_ukv[l].astype(BF16), mla_tabs)
        o_a = _flash(q_a, k_a, v_a, None, T, 0, N // tq, 0, T // tk, tq, tk)
        o_a = _flash(q_a, k_a, v_a, o_a, T, N // C, 1, N // C, 1, C, C)

        wf_ext = jnp.zeros((LANES, GLA_HEADS * GLA_DK), F32).at[:GLA_GATE_RANK].set(w_gate_f[l]).astype(BF16)
        wb_ext = jnp.zeros((LANES, GLA_HEADS * GLA_DK), F32).at[GLA_GATE_RANK:2 * GLA_GATE_RANK].set(
            w_gate_b[l]).astype(BF16)
        o_f, o_bk = _gla(p, N, wf_ext, b_gate_f[l], wb_ext, b_gate_b[l])
        o_b = _gla_out(o_f, o_bk, p, g_gla[l])

        o_c = _fourier(p, N, C)

        qs, ks, vs = _swa_prep(p, swa_tabs)
        o_d = _swa(qs, ks, vs, swa_sink[l], N, C)

        rows = N if last else T
        y = _matmul([o_a, o_b, o_c, o_d], w_out, l, _tile_for(rows, 4), 512, F32, True, m_rows=rows)
        x1, h2 = _resid(xs, y, g_post_mix[l], mod[l], 2, N, rows, pre=(g_pre_ffn[l], mod[l], 3))
        u = _matmul([h2], w_up, l, _tile_for(rows, 4), 512, BF16, True)
        act = _gate(u, N, conv_w[l], conv_b[l])
        f = _matmul([act], w_down_b, l, _tile_for(rows, 16), 512, F32)
        if last:
            (x_out,) = _resid((x1,), f, g_post_ffn[l], mod[l], 5, N, N)
            return x_out.reshape(1, N, D)
        x2, h = _resid((x1,), f, g_post_ffn[l], mod[l], 5, N, T, pre=(g_pre_mix[l + 1], mod[l + 1], 0))
        xs = (x2,)
```

```python
import functools

import numpy as np
import jax
import jax.numpy as jnp
from jax import lax
from jax.experimental import pallas as pl
from jax.experimental.pallas import tpu as pltpu

F32 = jnp.float32
BF16 = jnp.bfloat16

GRID_W = 64
EPS = 1e-6
ROPE_THETA = 10000.0
MLA_HEADS = 8
MLA_Q_RANK = 768
MLA_KV_RANK = 512
MLA_NOPE = 128
MLA_ROPE = 64
MLA_SCALE = (MLA_NOPE + MLA_ROPE) ** -0.5
GLA_HEADS = 4
GLA_DK = 128
GLA_DV = 256
GLA_GATE_RANK = 16
GLA_GATE_TAU = 16.0
GLA_CHUNK = 128
FNET_GROUPS = 8
FFT_N2 = 64
SWA_HEADS = 8
SWA_KV_HEADS = 2
SWA_GROUP = 4
SWA_BLOCK = 128
SWA_QBLOCK = 512
FLASH_SUBTILES = 8
SWA_SCALE = 128 ** -0.5
LANES = 128
ROW_TILE = 256

P_FU, P_GV, P_GR, P_CQ, P_SK, P_SQ = 0, 1024, 2048, 3072, 3840, 4096
P_CKV, P_GQ, P_GK, P_SV, P_KR, P_GLR = 5120, 5632, 6144, 6656, 6912, 7040
P_WIDTH = 7168

_NT = (((1,), (1,)), ((), ()))
_MIB = 1024 * 1024


def _params(sem, vmem_mib=40):
    return pltpu.CompilerParams(dimension_semantics=sem, vmem_limit_bytes=vmem_mib * _MIB)


def _silu(v):
    return v / (1.0 + jnp.exp(-v))


def _rms(v, g):
    return v * lax.rsqrt(jnp.mean(v * v, axis=-1, keepdims=True) + EPS) * g


def _pick(is_ctx, ref):
    return jnp.where(is_ctx, ref[1:2, :], ref[0:1, :])


def _ada_kernel(c_ref, w_ref, b_ref, o_ref):
    a = _silu(c_ref[...]).astype(BF16)
    o_ref[...] = jnp.dot(a, w_ref[...].astype(BF16), preferred_element_type=F32) + b_ref[...]


def _ada(cc, w_ada, b_ada):
    L, D, N6 = w_ada.shape
    tn = 512
    return pl.pallas_call(
        _ada_kernel,
        grid=(L, N6 // tn),
        in_specs=[pl.BlockSpec((8, D), lambda l, j: (0, 0)),
                  pl.BlockSpec((None, D, tn), lambda l, j: (l, 0, j)),
                  pl.BlockSpec((None, 1, tn), lambda l, j: (l, 0, j))],
        out_specs=pl.BlockSpec((None, 8, tn), lambda l, j: (l, 0, j)),
        out_shape=jax.ShapeDtypeStruct((L, 8, N6), F32),
        compiler_params=_params(("arbitrary", "arbitrary")),
    )(cc, w_ada, b_ada.reshape(L, 1, N6))


def _rms_mod_kernel(n_lat_tiles, x_ref, c_ref, g_ref, sh_ref, sc_ref, h_ref):
    is_ctx = pl.program_id(0) >= n_lat_tiles
    xv = jnp.where(is_ctx, c_ref[...], x_ref[...])
    y = _rms(xv, g_ref[...])
    h_ref[...] = (y * (1.0 + _pick(is_ctx, sc_ref)) + _pick(is_ctx, sh_ref)).astype(BF16)


def _rms_mod(x2d, ctx2d, g, mod, k_shift):
    N, D = x2d.shape
    C = ctx2d.shape[0]
    tm = ROW_TILE
    nl, nc = N // tm, C // tm
    return pl.pallas_call(
        functools.partial(_rms_mod_kernel, nl),
        grid=(nl + nc,),
        in_specs=[pl.BlockSpec((tm, D), lambda i: (jnp.minimum(i, nl - 1), 0)),
                  pl.BlockSpec((tm, D), lambda i: (jnp.maximum(i - nl, 0), 0)),
                  pl.BlockSpec((1, D), lambda i: (0, 0)),
                  pl.BlockSpec((8, D), lambda i: (0, k_shift)),
                  pl.BlockSpec((8, D), lambda i: (0, k_shift + 1))],
        out_specs=pl.BlockSpec((tm, D), lambda i: (i, 0)),
        out_shape=jax.ShapeDtypeStruct((N + C, D), BF16),
        compiler_params=_params(("arbitrary",)),
    )(x2d, ctx2d, g.reshape(1, D), mod, mod)


def _resid_kernel(n_lat_tiles, split_x, with_h, *refs):
    refs = list(refs)
    x_ref = refs.pop(0)
    c_ref = refs.pop(0) if split_x else None
    y_ref, gpost_ref, gate_ref = refs.pop(0), refs.pop(0), refs.pop(0)
    if with_h:
        gpre_ref, sh_ref, sc_ref = refs.pop(0), refs.pop(0), refs.pop(0)
    x1_ref = refs.pop(0)
    is_ctx = pl.program_id(0) >= n_lat_tiles
    xv = jnp.where(is_ctx, c_ref[...], x_ref[...]) if split_x else x_ref[...]
    x1 = xv + _pick(is_ctx, gate_ref) * _rms(y_ref[...], gpost_ref[...])
    x1_ref[...] = x1
    if with_h:
        h_ref = refs.pop(0)
        n = _rms(x1, gpre_ref[...])
        h_ref[...] = (n * (1.0 + _pick(is_ctx, sc_ref)) + _pick(is_ctx, sh_ref)).astype(BF16)


def _resid(xs, y, g_post, mod_gate, k_gate, n_lat, n_rows, pre=None):
    D = y.shape[1]
    tm = ROW_TILE
    nl = n_lat // tm
    nt = n_rows // tm
    split_x = len(xs) == 2
    row = lambda i: (i, 0)
    vec = lambda i: (0, 0)
    if split_x:
        in_specs = [pl.BlockSpec((tm, D), lambda i: (jnp.minimum(i, nl - 1), 0)),
                    pl.BlockSpec((tm, D), lambda i: (jnp.maximum(i - nl, 0), 0))]
    else:
        in_specs = [pl.BlockSpec((tm, D), row)]
    in_specs += [pl.BlockSpec((tm, D), row), pl.BlockSpec((1, D), vec),
                 pl.BlockSpec((8, D), lambda i: (0, k_gate))]
    args = list(xs) + [y, g_post.reshape(1, D), mod_gate]
    out_specs = [pl.BlockSpec((tm, D), row)]
    out_shape = [jax.ShapeDtypeStruct((n_rows, D), F32)]
    if pre is not None:
        g_pre, mod_pre, k_shift = pre
        in_specs += [pl.BlockSpec((1, D), vec),
                     pl.BlockSpec((8, D), lambda i: (0, k_shift)),
                     pl.BlockSpec((8, D), lambda i: (0, k_shift + 1))]
        args += [g_pre.reshape(1, D), mod_pre, mod_pre]
        out_specs.append(pl.BlockSpec((tm, D), row))
        out_shape.append(jax.ShapeDtypeStruct((n_rows, D), BF16))
    return pl.pallas_call(
        functools.partial(_resid_kernel, nl, split_x, pre is not None),
        grid=(nt,), in_specs=in_specs, out_specs=out_specs, out_shape=out_shape,
        compiler_params=_params(("arbitrary",)),
    )(*args)


def _mm_kernel(n_in, *refs):
    w_ref, o_ref = refs[n_in], refs[n_in + 1]
    kc = w_ref.shape[0] // n_in
    acc = None
    for k in range(n_in):
        d = jnp.dot(refs[k][...].astype(BF16), w_ref[kc * k:kc * (k + 1), :].astype(BF16),
                    preferred_element_type=F32)
        acc = d if acc is None else acc + d
    o_ref[...] = acc.astype(o_ref.dtype)


def _matmul(a_list, w3, layer, tm, tn, out_dtype, single_buffer_rows=False, m_rows=None):
    M = a_list[0].shape[0] if m_rows is None else m_rows
    _, K, Nn = w3.shape
    kc = K // len(a_list)
    mode = dict(pipeline_mode=pl.Buffered(1)) if single_buffer_rows else {}
    return pl.pallas_call(
        functools.partial(_mm_kernel, len(a_list)),
        grid=(M // tm, Nn // tn),
        in_specs=[pl.BlockSpec((tm, kc), lambda i, j: (i, 0), **mode) for _ in a_list]
        + [pl.BlockSpec((None, K, tn), lambda i, j: (layer, 0, j))],
        out_specs=pl.BlockSpec((tm, tn), lambda i, j: (i, j)),
        out_shape=jax.ShapeDtypeStruct((M, Nn), out_dtype),
        compiler_params=_params(("arbitrary", "arbitrary"), 56),
    )(*a_list, w3)


def _rope_tables(n_lat, n_ctx, n_real, half):
    t = jnp.arange(n_lat, dtype=jnp.int32)
    rows = (t // GRID_W).astype(F32)[:, None]
    cols = (t % GRID_W).astype(F32)[:, None]
    lane = np.arange(LANES)
    grp = (lane // (2 * half)) % 2
    jj = lane % (2 * half)
    first = jj < half
    real = lane < n_real
    inv = ROPE_THETA ** (-jnp.asarray(jj % half, F32) / half)
    ang = jnp.where(jnp.asarray(grp == 0)[None, :], rows, cols) * inv[None, :]
    cos, sin = jnp.cos(ang), jnp.sin(ang)
    real_j, first_j = jnp.asarray(real)[None, :], jnp.asarray(first)[None, :]
    cos_t = jnp.where(real_j, cos, 1.0)
    sa_t = jnp.where(real_j & first_j, -sin, 0.0)
    sb_t = jnp.where(real_j & (~first_j), sin, 0.0)
    pad = lambda a, v: jnp.concatenate([a, jnp.full((n_ctx, LANES), v, F32)], axis=0)
    return pad(cos_t, 1.0), pad(sa_t, 0.0), pad(sb_t, 0.0)


def _rope(v, cos, sa, sb, half):
    return v * cos + pltpu.roll(v, LANES - half, 1) * sa + pltpu.roll(v, half, 1) * sb


def _mla_q_kernel(cq_ref, g_ref, w_ref, cos_ref, sa_ref, sb_ref, o_ref):
    n = _rms(cq_ref[...], g_ref[...]).astype(BF16)
    a = jnp.dot(n, w_ref[...], preferred_element_type=F32)
    cos, sa, sb = cos_ref[...], sa_ref[...], sb_ref[...]
    for h in range(MLA_HEADS):
        lo = 2 * LANES * h
        o_ref[:, lo:lo + LANES] = (a[:, lo:lo + LANES] * MLA_SCALE).astype(BF16)
        r = _rope(a[:, lo + LANES:lo + 2 * LANES], cos, sa, sb, MLA_ROPE // 4)
        o_ref[:, lo + LANES:lo + 2 * LANES] = (r * MLA_SCALE).astype(BF16)


def _prep_tile(T):
    return 3 * ROW_TILE if T % (3 * ROW_TILE) == 0 else ROW_TILE


def _mla_q(p, g_qa, w_uq_ext, tabs):
    T = p.shape[0]
    tm = _prep_tile(T)
    tab = pl.BlockSpec((tm, LANES), lambda i: (i, 0))
    return pl.pallas_call(
        _mla_q_kernel,
        grid=(T // tm,),
        in_specs=[pl.BlockSpec((tm, MLA_Q_RANK), lambda i: (i, P_CQ // MLA_Q_RANK)),
                  pl.BlockSpec((1, MLA_Q_RANK), lambda i: (0, 0)),
                  pl.BlockSpec((MLA_Q_RANK, 2 * LANES * MLA_HEADS), lambda i: (0, 0)),
                  tab, tab, tab],
        out_specs=pl.BlockSpec((tm, 2 * LANES * MLA_HEADS), lambda i: (i, 0)),
        out_shape=jax.ShapeDtypeStruct((T, 2 * LANES * MLA_HEADS), BF16),
        compiler_params=_params(("arbitrary",)),
    )(p, g_qa.reshape(1, -1), w_uq_ext, *tabs)


def _mla_kv_kernel(ckv_ref, kr_ref, g_ref, w_ref, cos_ref, sa_ref, sb_ref, k_ref, v_ref):
    n = _rms(ckv_ref[...], g_ref[...]).astype(BF16)
    a = jnp.dot(n, w_ref[...], preferred_element_type=F32)
    kr = _rope(kr_ref[...], cos_ref[...], sa_ref[...], sb_ref[...], MLA_ROPE // 4).astype(BF16)
    ones = jnp.ones(kr.shape, BF16)
    for h in range(MLA_HEADS):
        lo = 2 * LANES * h
        k_ref[:, lo:lo + LANES] = a[:, lo:lo + LANES].astype(BF16)
        k_ref[:, lo + LANES:lo + 2 * LANES] = kr
        v_ref[:, lo:lo + LANES] = a[:, lo + LANES:lo + 2 * LANES].astype(BF16)
        v_ref[:, lo + LANES:lo + 2 * LANES] = ones


def _mla_kv(p, g_kva, w_ukv, tabs):
    T = p.shape[0]
    tm = _prep_tile(T)
    tab = pl.BlockSpec((tm, LANES), lambda i: (i, 0))
    wide = 2 * LANES * MLA_HEADS
    return pl.pallas_call(
        _mla_kv_kernel,
        grid=(T // tm,),
        in_specs=[pl.BlockSpec((tm, MLA_KV_RANK), lambda i: (i, P_CKV // MLA_KV_RANK)),
                  pl.BlockSpec((tm, LANES), lambda i: (i, P_KR // LANES)),
                  pl.BlockSpec((1, MLA_KV_RANK), lambda i: (0, 0)),
                  pl.BlockSpec((MLA_KV_RANK, wide), lambda i: (0, 0)),
                  tab, tab, tab],
        out_specs=[pl.BlockSpec((tm, wide), lambda i: (i, 0)),
                   pl.BlockSpec((tm, wide), lambda i: (i, 0))],
        out_shape=[jax.ShapeDtypeStruct((T, wide), BF16),
                   jax.ShapeDtypeStruct((T, wide), BF16)],
        compiler_params=_params(("arbitrary",)),
    )(p, p, g_kva.reshape(1, -1), w_ukv, *tabs)


def _flash_kernel(q_ref, k_ref, v_ref, *rest):
    o_ref, m_sc, acc_sc = rest[-3:]
    kk = pl.program_id(2)

    @pl.when(kk == 0)
    def _():
        m_sc[...] = jnp.full(m_sc.shape, -1e30, F32)
        acc_sc[...] = jnp.zeros(acc_sc.shape, F32)

    rc = q_ref.shape[0] // FLASH_SUBTILES
    subs = [slice(r * rc, (r + 1) * rc) for r in range(FLASH_SUBTILES)]
    scores = [lax.dot_general(q_ref[rows, :], k_ref[...], _NT, preferred_element_type=F32) for rows in subs]
    for rows, s in zip(subs, scores):
        m_prev = m_sc[rows, :]
        m_new = jnp.maximum(m_prev, jnp.max(s, axis=1, keepdims=True))
        alpha = jnp.exp(m_prev - m_new)
        p = jnp.exp((s - m_new).astype(BF16))
        acc_sc[rows, :] = alpha * acc_sc[rows, :] + jnp.dot(p, v_ref[...], preferred_element_type=F32)
        m_sc[rows, :] = m_new

    @pl.when(kk == pl.num_programs(2) - 1)
    def _():
        acc = acc_sc[...]
        o_ref[...] = (acc[:, :LANES] / acc[:, LANES:LANES + 1]).astype(o_ref.dtype)


def _flash(q, k, v, out_prev, n_rows, q_blk0, n_qblk, k_blk0, n_kblk, tq, tk):
    dq = 2 * LANES
    in_specs = [pl.BlockSpec((tq, dq), lambda h, i, kk: (i + q_blk0, h)),
                pl.BlockSpec((tk, dq), lambda h, i, kk: (kk + k_blk0, h)),
                pl.BlockSpec((tk, dq), lambda h, i, kk: (kk + k_blk0, h))]
    args = [q, k, v]
    aliases = {}
    if out_prev is not None:
        in_specs.append(pl.BlockSpec(memory_space=pl.ANY))
        args.append(out_prev)
        aliases = {3: 0}
    return pl.pallas_call(
        _flash_kernel,
        grid=(MLA_HEADS, n_qblk, n_kblk),
        in_specs=in_specs,
        out_specs=pl.BlockSpec((tq, LANES), lambda h, i, kk: (i + q_blk0, h)),
        out_shape=jax.ShapeDtypeStruct((n_rows, MLA_HEADS * LANES), BF16),
        scratch_shapes=[pltpu.VMEM((tq, 1), F32), pltpu.VMEM((tq, dq), F32)],
        input_output_aliases=aliases,
        compiler_params=_params(("arbitrary", "arbitrary", "arbitrary"), 48),
    )(*args)


def _gla_direction(q, k, v, glr, wg, bg, tri, last_row, st_ref, o_ref):
    z = jnp.dot(glr.astype(BF16), wg, preferred_element_type=F32) + bg
    la = (jnp.minimum(z, 0.0) - jnp.log(1.0 + jnp.exp(-jnp.abs(z)))) * (1.0 / GLA_GATE_TAU)
    ones = jnp.where(tri, 1.0, 0.0).astype(BF16)
    la_hi = la.astype(BF16)
    la_lo = (la - la_hi.astype(F32)).astype(BF16)
    b = (jnp.dot(ones, la_hi, preferred_element_type=F32)
         + jnp.dot(ones, la_lo, preferred_element_type=F32))
    b_tot = b[last_row:last_row + 1, :]
    qt = q * (GLA_DK ** -0.5) * jnp.exp(b)
    kt = k * jnp.exp(-b)
    kh = k * jnp.exp(b_tot - b)
    dec = jnp.exp(b_tot)
    for h in range(GLA_HEADS):
        ks = slice(GLA_DK * h, GLA_DK * (h + 1))
        vs = slice(GLA_DV * h, GLA_DV * (h + 1))
        qh = qt[:, ks].astype(BF16)
        vh = v[:, vs]
        st = st_ref[h]
        inter = lax.dot_general(qh, st.astype(BF16), _NT, preferred_element_type=F32)
        sc = lax.dot_general(qh, kt[:, ks].astype(BF16), _NT, preferred_element_type=F32)
        sc = jnp.where(tri, sc, 0.0)
        intra = jnp.dot(sc.astype(BF16), vh.astype(BF16), preferred_element_type=F32)
        o_ref[:, vs] = inter + intra
        st_ref[h] = st * dec[:, ks] + jnp.dot(vh.T.astype(BF16), kh[:, ks].astype(BF16),
                                               preferred_element_type=F32)


def _gla_kernel(qf, kf, vf, gf, qb, kb, vb, gb, wf, bf, wb, bb, of_ref, ob_ref, stf, stb):
    @pl.when(pl.program_id(0) == 0)
    def _():
        stf[...] = jnp.zeros(stf.shape, F32)
        stb[...] = jnp.zeros(stb.shape, F32)

    cs = qf.shape[0]
    t_i = lax.broadcasted_iota(jnp.int32, (cs, cs), 0)
    s_i = lax.broadcasted_iota(jnp.int32, (cs, cs), 1)
    _gla_direction(qf[...], kf[...], vf[...], gf[...], wf[...], bf[...], s_i <= t_i, cs - 1, stf, of_ref)
    _gla_direction(qb[...], kb[...], vb[...], gb[...], wb[...], bb[...], s_i >= t_i, 0, stb, ob_ref)


def _gla(p, n_lat, wf_ext, bf, wb_ext, bb):
    T = p.shape[0]
    cs = GLA_CHUNK
    nch = T // cs
    nlc = n_lat // cs
    fwd = lambda c: (c + nlc) % nch
    bwd = lambda c: nch - 1 - c
    dk, dv = GLA_HEADS * GLA_DK, GLA_HEADS * GLA_DV

    def specs(idx):
        return [pl.BlockSpec((cs, dk), lambda c: (idx(c), P_GQ // dk)),
                pl.BlockSpec((cs, dk), lambda c: (idx(c), P_GK // dk)),
                pl.BlockSpec((cs, dv), lambda c: (idx(c), P_GV // dv)),
                pl.BlockSpec((cs, LANES), lambda c: (idx(c), P_GLR // LANES))]

    wspec = pl.BlockSpec((LANES, dk), lambda c: (0, 0))
    bspec = pl.BlockSpec((1, dk), lambda c: (0, 0))
    return pl.pallas_call(
        _gla_kernel,
        grid=(nch,),
        in_specs=specs(fwd) + specs(bwd) + [wspec, bspec, wspec, bspec],
        out_specs=[pl.BlockSpec((cs, dv), lambda c: (fwd(c), 0)),
                   pl.BlockSpec((cs, dv), lambda c: (bwd(c), 0))],
        out_shape=[jax.ShapeDtypeStruct((T, dv), F32), jax.ShapeDtypeStruct((T, dv), F32)],
        scratch_shapes=[pltpu.VMEM((GLA_HEADS, GLA_DV, GLA_DK), F32),
                        pltpu.VMEM((GLA_HEADS, GLA_DV, GLA_DK), F32)],
        compiler_params=_params(("arbitrary",)),
    )(p, p, p, p, p, p, p, p, wf_ext, bf.reshape(1, dk), wb_ext, bb.reshape(1, dk))


def _gla_out_kernel(of_ref, ob_ref, gr_ref, g_ref, o_ref):
    o = of_ref[...] + ob_ref[...]
    gr = gr_ref[...]
    for h in range(GLA_HEADS):
        vs = slice(GLA_DV * h, GLA_DV * (h + 1))
        o_ref[:, vs] = (_rms(o[:, vs], g_ref[...]) * _silu(gr[:, vs])).astype(BF16)


def _gla_out(o_f, o_b, p, g_gla):
    T, dv = o_f.shape
    tm = _prep_tile(T)
    row = pl.BlockSpec((tm, dv), lambda i: (i, 0))
    return pl.pallas_call(
        _gla_out_kernel,
        grid=(T // tm,),
        in_specs=[row, row, pl.BlockSpec((tm, dv), lambda i: (i, P_GR // dv)),
                  pl.BlockSpec((1, GLA_DV), lambda i: (0, 0))],
        out_specs=row,
        out_shape=jax.ShapeDtypeStruct((T, dv), BF16),
        compiler_params=_params(("arbitrary",)),
    )(o_f, o_b, p, g_gla.reshape(1, GLA_DV))


def _dft_cos_sin(n):
    k = np.arange(n)
    ang = 2.0 * np.pi * ((k[:, None] * k[None, :]) % n) / n
    return np.cos(ang), np.sin(ang)


def _channel_dft(xb, wd):
    zr, zi = [], []
    for g in range(FNET_GROUPS):
        z = jnp.dot(xb[:, LANES * g:LANES * (g + 1)], wd, preferred_element_type=F32)
        zr.append(z[:, :LANES])
        zi.append(z[:, LANES:])
    return jnp.concatenate(zr, axis=1), jnp.concatenate(zi, axis=1)


def _fft_a_kernel(n1, x_ref, wd_ref, m_ref, bre_ref, bim_ref):
    for n2 in range(FFT_N2):
        xs = x_ref[pl.ds(n2, n1, stride=FFT_N2), :].astype(BF16)
        z = jnp.dot(xs, wd_ref[...], preferred_element_type=F32)
        zs = jnp.concatenate([z[:, :LANES], z[:, LANES:]], axis=0).astype(BF16)
        b = jnp.dot(m_ref[n2], zs, preferred_element_type=F32)
        bre_ref[n2 * n1:(n2 + 1) * n1, :] = b[:n1]
        bim_ref[n2 * n1:(n2 + 1) * n1, :] = b[n1:]


def _fft_b_kernel(n1, scale, c_ref, s_ref, bre_ref, bim_ref, o_ref):
    grp = 8
    for t in range(n1 // grp):
        rows = [pl.ds(grp * t + r, FFT_N2, stride=n1) for r in range(grp)]
        br = jnp.concatenate([bre_ref[rw, :] for rw in rows], axis=1).astype(BF16)
        bi = jnp.concatenate([bim_ref[rw, :] for rw in rows], axis=1).astype(BF16)
        o = (jnp.dot(c_ref[...], br, preferred_element_type=F32)
             + jnp.dot(s_ref[...], bi, preferred_element_type=F32)) * scale
        for r in range(grp):
            o_ref[rows[r], :] = o[:, LANES * r:LANES * (r + 1)]


def _fft_ctx_kernel(scale, x_ref, wd_ref, cs_ref, prev_ref, o_ref):
    del prev_ref
    zr, zi = _channel_dft(x_ref[...].astype(BF16), wd_ref[...])
    z = jnp.concatenate([zr, zi], axis=0).astype(BF16)
    o_ref[...] = jnp.dot(cs_ref[...], z, preferred_element_type=F32) * scale


def _fourier(p, n_lat, n_ctx):
    T = p.shape[0]
    gw = FNET_GROUPS * LANES
    n1 = n_lat // FFT_N2
    cd, sd = _dft_cos_sin(LANES)
    as_bf16 = lambda a: jnp.asarray(a, F32).astype(BF16)
    wd = as_bf16(np.concatenate([cd, -sd], axis=1))
    k1 = jnp.arange(n1, dtype=jnp.int32)[None, :, None]
    pos = (FFT_N2 * jnp.arange(n1, dtype=jnp.int32)[None, None, :]
           + jnp.arange(FFT_N2, dtype=jnp.int32)[:, None, None])
    ang = (2.0 * np.pi / n_lat) * ((k1 * pos) % n_lat).astype(F32)
    gc, gs = jnp.cos(ang), jnp.sin(ang)
    m = jnp.concatenate([jnp.concatenate([gc, gs], axis=2),
                         jnp.concatenate([-gs, gc], axis=2)], axis=1).astype(BF16)
    col = lambda g: (0, g)
    bre, bim = pl.pallas_call(
        functools.partial(_fft_a_kernel, n1),
        grid=(FNET_GROUPS,),
        in_specs=[pl.BlockSpec((n_lat, LANES), lambda g: (0, P_FU // LANES + g)),
                  pl.BlockSpec((LANES, 2 * LANES), lambda g: (0, 0)),
                  pl.BlockSpec((FFT_N2, 2 * n1, 2 * n1), lambda g: (0, 0, 0))],
        out_specs=[pl.BlockSpec((n_lat, LANES), col)] * 2,
        out_shape=[jax.ShapeDtypeStruct((n_lat, gw), F32)] * 2,
        compiler_params=_params(("arbitrary",), 48),
    )(p, wd, m)

    c2, s2 = _dft_cos_sin(FFT_N2)
    mspec = pl.BlockSpec((FFT_N2, FFT_N2), lambda g: (0, 0))
    y = pl.pallas_call(
        functools.partial(_fft_b_kernel, n1, float((n_lat * LANES) ** -0.5)),
        grid=(FNET_GROUPS,),
        in_specs=[mspec, mspec, pl.BlockSpec((n_lat, LANES), col), pl.BlockSpec((n_lat, LANES), col)],
        out_specs=pl.BlockSpec((n_lat, LANES), col),
        out_shape=jax.ShapeDtypeStruct((T, gw), F32),
        compiler_params=_params(("arbitrary",)),
    )(as_bf16(c2), as_bf16(s2), bre, bim)

    cc, sc = _dft_cos_sin(n_ctx)
    return pl.pallas_call(
        functools.partial(_fft_ctx_kernel, float((n_ctx * LANES) ** -0.5)),
        grid=(1,),
        in_specs=[pl.BlockSpec((n_ctx, gw), lambda j: (n_lat // n_ctx, P_FU // gw)),
                  pl.BlockSpec((LANES, 2 * LANES), lambda j: (0, 0)),
                  pl.BlockSpec((n_ctx, 2 * n_ctx), lambda j: (0, 0)),
                  pl.BlockSpec(memory_space=pl.ANY)],
        out_specs=pl.BlockSpec((n_ctx, gw), lambda j: (n_lat // n_ctx, 0)),
        out_shape=jax.ShapeDtypeStruct((T, gw), F32),
        input_output_aliases={3: 0},
        compiler_params=_params(("arbitrary",)),
    )(p, wd, as_bf16(np.concatenate([cc, sc], axis=1)), y)


def _swa_prep_kernel(q_ref, k_ref, v_ref, cos_ref, sa_ref, sb_ref, qo_ref, ko_ref, vo_ref):
    cos, sa, sb = cos_ref[...], sa_ref[...], sb_ref[...]
    for h in range(SWA_HEADS):
        sl = slice(LANES * h, LANES * (h + 1))
        qo_ref[:, sl] = (_rope(q_ref[:, sl], cos, sa, sb, 32) * SWA_SCALE).astype(BF16)
    for h in range(SWA_KV_HEADS):
        sl = slice(LANES * h, LANES * (h + 1))
        ko_ref[:, sl] = _rope(k_ref[:, sl], cos, sa, sb, 32).astype(BF16)
    vo_ref[...] = v_ref[...].astype(BF16)


def _swa_prep(p, tabs):
    T = p.shape[0]
    tm = _prep_tile(T)
    qw, kw = SWA_HEADS * LANES, SWA_KV_HEADS * LANES
    tab = pl.BlockSpec((tm, LANES), lambda i: (i, 0))
    return pl.pallas_call(
        _swa_prep_kernel,
        grid=(T // tm,),
        in_specs=[pl.BlockSpec((tm, qw), lambda i: (i, P_SQ // qw)),
                  pl.BlockSpec((tm, kw), lambda i: (i, P_SK // kw)),
                  pl.BlockSpec((tm, kw), lambda i: (i, P_SV // kw)),
                  tab, tab, tab],
        out_specs=[pl.BlockSpec((tm, qw), lambda i: (i, 0)),
                   pl.BlockSpec((tm, kw), lambda i: (i, 0)),
                   pl.BlockSpec((tm, kw), lambda i: (i, 0))],
        out_shape=[jax.ShapeDtypeStruct((T, qw), BF16),
                   jax.ShapeDtypeStruct((T, kw), BF16),
                   jax.ShapeDtypeStruct((T, kw), BF16)],
        compiler_params=_params(("arbitrary",)),
    )(p, p, p, *tabs)


def _sink_attend(q, kcat, vcat, mask, sink):
    s = lax.dot_general(q, kcat, _NT, preferred_element_type=F32)
    if mask is not None:
        s = jnp.where(mask, s, -1e30)
    m = jnp.maximum(jnp.max(s, axis=1, keepdims=True), sink)
    pr = jnp.exp((s - m).astype(BF16))
    den = jnp.sum(pr.astype(F32), axis=1, keepdims=True) + jnp.exp(sink - m)
    return jnp.dot(pr, vcat, preferred_element_type=F32) / den


def _swa_lat_kernel(q_ref, kc_ref, kp_ref, k0_ref, kn_ref, vc_ref, vp_ref, v0_ref, vn_ref,
                    sink_ref, o_ref):
    kvh, nb = pl.program_id(0), pl.program_id(1)
    n_ctx = kc_ref.shape[0]
    qb = q_ref.shape[0]
    win = SWA_BLOCK
    kcat = jnp.concatenate([kc_ref[...], kp_ref[...], k0_ref[...], kn_ref[...]], axis=0)
    vcat = jnp.concatenate([vc_ref[...], vp_ref[...], v0_ref[...], vn_ref[...]], axis=0)
    n_keys = n_ctx + qb + 2 * win
    i = lax.broadcasted_iota(jnp.int32, (qb, n_keys), 0)
    j = lax.broadcasted_iota(jnp.int32, (qb, n_keys), 1)
    off = j - (n_ctx + win)
    in_band = jnp.abs(i - off) <= win
    exists = ((off >= 0) | (nb > 0)) & ((off < qb) | (nb < pl.num_programs(1) - 1))
    mask = (j < n_ctx) | (in_band & exists)
    for g in range(SWA_GROUP):
        sl = slice(LANES * g, LANES * (g + 1))
        sink = sink_ref[pl.ds(kvh * SWA_GROUP + g, 1), 0:1]
        o_ref[:, sl] = _sink_attend(q_ref[:, sl], kcat, vcat, mask, sink).astype(BF16)


def _swa_ctx_kernel(q_ref, k_ref, v_ref, sink_ref, prev_ref, o_ref):
    del prev_ref
    kvh = pl.program_id(0)
    for g in range(SWA_GROUP):
        sl = slice(LANES * g, LANES * (g + 1))
        sink = sink_ref[pl.ds(kvh * SWA_GROUP + g, 1), 0:1]
        o_ref[:, sl] = _sink_attend(q_ref[:, sl], k_ref[...], v_ref[...], None, sink).astype(BF16)


def _swa(qs, ks, vs, sink, n_lat, n_ctx):
    T = qs.shape[0]
    win = SWA_BLOCK
    qb = SWA_QBLOCK
    nb = n_lat // qb
    per = qb // win
    gq = SWA_GROUP * LANES
    sink2d = jnp.broadcast_to(sink.astype(F32)[:, None], (SWA_HEADS, LANES))
    cblk = n_lat // n_ctx
    ctx_spec = pl.BlockSpec((n_ctx, LANES), lambda h, b: (cblk, h))
    prev_spec = pl.BlockSpec((win, LANES), lambda h, b: (jnp.maximum(b * per - 1, 0), h))
    cur_spec = pl.BlockSpec((qb, LANES), lambda h, b: (b, h))
    next_spec = pl.BlockSpec((win, LANES), lambda h, b: (jnp.minimum((b + 1) * per, nb * per - 1), h))
    sink_spec = pl.BlockSpec((SWA_HEADS, LANES), lambda h, b: (0, 0))
    o_lat = pl.pallas_call(
        _swa_lat_kernel,
        grid=(SWA_KV_HEADS, nb),
        in_specs=[pl.BlockSpec((qb, gq), lambda h, b: (b, h)),
                  ctx_spec, prev_spec, cur_spec, next_spec,
                  ctx_spec, prev_spec, cur_spec, next_spec, sink_spec],
        out_specs=pl.BlockSpec((qb, gq), lambda h, b: (b, h)),
        out_shape=jax.ShapeDtypeStruct((T, SWA_HEADS * LANES), BF16),
        compiler_params=_params(("arbitrary", "arbitrary")),
    )(qs, ks, ks, ks, ks, vs, vs, vs, vs, sink2d)
    return pl.pallas_call(
        _swa_ctx_kernel,
        grid=(SWA_KV_HEADS,),
        in_specs=[pl.BlockSpec((n_ctx, gq), lambda h: (cblk, h)),
                  pl.BlockSpec((n_ctx, LANES), lambda h: (cblk, h)),
                  pl.BlockSpec((n_ctx, LANES), lambda h: (cblk, h)),
                  pl.BlockSpec((SWA_HEADS, LANES), lambda h: (0, 0)),
                  pl.BlockSpec(memory_space=pl.ANY)],
        out_specs=pl.BlockSpec((n_ctx, gq), lambda h: (cblk, h)),
        out_shape=jax.ShapeDtypeStruct((T, SWA_HEADS * LANES), BF16),
        input_output_aliases={4: 0},
        compiler_params=_params(("arbitrary",)),
    )(qs, ks, vs, sink2d, o_lat)


GATE_TILE = 128
GATE_HALO = 16


def _gate_kernel(tm, n_lat, n_tot, ug_ref, ua_ref, gp_ref, gn_ref, ap_ref, an_ref, cwg_ref, cwa_ref,
                 cbg_ref, cba_ref, o_ref, ubuf):
    i = pl.program_id(0)
    first = (i == 0) | (i * tm == n_lat)
    last = ((i + 1) * tm == n_lat) | ((i + 1) * tm == n_tot)
    rows = tm + 2 * GATE_HALO
    out_r = lax.broadcasted_iota(jnp.int32, (2 * tm, rows), 0)
    src_r = lax.broadcasted_iota(jnp.int32, (2 * tm, rows), 1)
    want = jnp.where(out_r < tm, out_r + (GATE_HALO - 1), out_r - tm + (GATE_HALO + 1))
    pick = jnp.where(src_r == want, 1.0, 0.0).astype(BF16)

    def conv(u_ref, p_ref, n_ref, cw_ref, cb_ref):
        ubuf[0:GATE_HALO, :] = jnp.where(first, jnp.zeros_like(p_ref[...]), p_ref[...])
        ubuf[GATE_HALO:GATE_HALO + tm, :] = u_ref[...]
        ubuf[GATE_HALO + tm:, :] = jnp.where(last, jnp.zeros_like(n_ref[...]), n_ref[...])
        shifted = jnp.dot(pick, ubuf[...], preferred_element_type=F32)
        um, up = shifted[:tm], shifted[tm:]
        return (um * cw_ref[0:1, :] + u_ref[...].astype(F32) * cw_ref[1:2, :] + up * cw_ref[2:3, :]
                + cb_ref[...])

    g = conv(ug_ref, gp_ref, gn_ref, cwg_ref, cbg_ref)
    a = conv(ua_ref, ap_ref, an_ref, cwa_ref, cba_ref)
    o_ref[...] = (_silu(g) * a).astype(BF16)


def _gate(u, n_lat, conv_w, conv_b):
    T = u.shape[0]
    dff = u.shape[1] // 2
    tm = GATE_TILE
    hb = tm // GATE_HALO
    last_hb = T // GATE_HALO - 1
    cb2 = conv_b.reshape(1, 2 * dff)
    prev = lambda i: jnp.maximum(i * hb - 1, 0)
    nxt = lambda i: jnp.minimum((i + 1) * hb, last_hb)
    return pl.pallas_call(
        functools.partial(_gate_kernel, tm, n_lat, T),
        grid=(T // tm,),
        in_specs=[pl.BlockSpec((tm, dff), lambda i: (i, 0)),
                  pl.BlockSpec((tm, dff), lambda i: (i, 1)),
                  pl.BlockSpec((GATE_HALO, dff), lambda i: (prev(i), 0)),
                  pl.BlockSpec((GATE_HALO, dff), lambda i: (nxt(i), 0)),
                  pl.BlockSpec((GATE_HALO, dff), lambda i: (prev(i), 1)),
                  pl.BlockSpec((GATE_HALO, dff), lambda i: (nxt(i), 1)),
                  pl.BlockSpec((3, dff), lambda i: (0, 0)),
                  pl.BlockSpec((3, dff), lambda i: (0, 1)),
                  pl.BlockSpec((1, dff), lambda i: (0, 0)),
                  pl.BlockSpec((1, dff), lambda i: (0, 1))],
        out_specs=pl.BlockSpec((tm, dff), lambda i: (i, 0)),
        out_shape=jax.ShapeDtypeStruct((T, dff), BF16),
        scratch_shapes=[pltpu.VMEM((tm + 2 * GATE_HALO, dff), BF16)],
        compiler_params=_params(("arbitrary",), 48),
    )(u, u, u, u, u, u, conv_w, conv_w, cb2, cb2)


def _pad_cols(w, width):
    return jnp.pad(w, ((0, 0), (0, width - w.shape[1])))


def _w_in_ext(w_in):
    cq, ckv, kr, gq, gk, gv, gr, glr, fu, sq, sk, sv = jnp.split(
        w_in, np.cumsum([768, 512, 64, 512, 512, 1024, 1024, 32, 1024, 1024, 256]).tolist(), axis=2)
    pad = lambda w: jnp.pad(w, ((0, 0), (0, 0), (0, LANES - w.shape[2])))
    parts = [fu, gv, gr, cq, sk, sq, ckv, gq, gk, sv, pad(kr), pad(glr)]
    return jnp.concatenate(parts, axis=2).astype(BF16)


def _tile_for(total, parts, mult=16):
    t = total // parts
    assert t * parts == total and t % mult == 0, (total, parts)
    return t


def _key_tile(total):
    best = LANES
    for t in range(LANES, 1536 + 1, LANES):
        if total % t == 0:
            best = t
    return best


def kernel(x, c, ctx, c_ctx, w_ada, b_ada, g_pre_mix, g_post_mix, g_pre_ffn, g_post_ffn, w_in, g_qa, w_uq,
           g_kva, w_ukv, w_gate_f, b_gate_f, w_gate_b, b_gate_b, g_gla, swa_sink, w_out, w_up, conv_w, conv_b,
           w_down):
    B, N, D = x.shape
    C = ctx.shape[1]
    T = N + C
    L = w_ada.shape[0]
    dff = w_down.shape[1]
    assert B == 1 and D == 4096 and N % (GRID_W * 16) == 0 and C % ROW_TILE == 0 and N % C == 0
    x2d, ctx2d = x[0], ctx[0]

    cc = jnp.zeros((8, D), F32).at[0].set(c[0]).at[1].set(c_ctx)
    mod = _ada(cc, w_ada, b_ada)
    mla_tabs = _rope_tables(N, C, MLA_ROPE, MLA_ROPE // 4)
    swa_tabs = _rope_tables(N, C, LANES, 32)

    tm_mm = _tile_for(T, 4)

    w_in_x = _w_in_ext(w_in)
    w_down_b = w_down.astype(BF16)
    tq, tk = _tile_for(N, 2), _key_tile(T)

    h = _rms_mod(x2d, ctx2d, g_pre_mix[0], mod[0], 0)
    xs = (x2d, ctx2d)
    for l in range(L):
        last = l == L - 1
        p = _matmul([h], w_in_x, l, tm_mm, 512, F32, True)

        w_uq_ext = jnp.pad(w_uq[l].reshape(MLA_Q_RANK, MLA_HEADS, MLA_NOPE + MLA_ROPE),
                           ((0, 0), (0, 0), (0, 2 * LANES - MLA_NOPE - MLA_ROPE))
                           ).reshape(MLA_Q_RANK, 2 * LANES * MLA_HEADS).astype(BF16)
        q_a = _mla_q(p, g_qa[l], w_uq_ext, mla_tabs)
        k_a, v_a = _mla_kv(p, g_kva[l], w_ukv[l].astype(BF16), mla_tabs)
        o_a = _flash(q_a, k_a, v_a, None, T, 0, N // tq, 0, T // tk, tq, tk)
        o_a = _flash(q_a, k_a, v_a, o_a, T, N // C, 1, N // C, 1, C, C)

        wf_ext = jnp.zeros((LANES, GLA_HEADS * GLA_DK), F32).at[:GLA_GATE_RANK].set(w_gate_f[l]).astype(BF16)
        wb_ext = jnp.zeros((LANES, GLA_HEADS * GLA_DK), F32).at[GLA_GATE_RANK:2 * GLA_GATE_RANK].set(
            w_gate_b[l]).astype(BF16)
        o_f, o_bk = _gla(p, N, wf_ext, b_gate_f[l], wb_ext, b_gate_b[l])
        o_b = _gla_out(o_f, o_bk, p, g_gla[l])

        o_c = _fourier(p, N, C)

        qs, ks, vs = _swa_prep(p, swa_tabs)
        o_d = _swa(qs, ks, vs, swa_sink[l], N, C)

        rows = N if last else T
        y = _matmul([o_a, o_b, o_c, o_d], w_out, l, _tile_for(rows, 4), 512, F32, True, m_rows=rows)
        x1, h2 = _resid(xs, y, g_post_mix[l], mod[l], 2, N, rows, pre=(g_pre_ffn[l], mod[l], 3))
        u = _matmul([h2], w_up, l, _tile_for(rows, 4), 512, BF16, True)
        act = _gate(u, N, conv_w[l], conv_b[l])
        f = _matmul([act], w_down_b, l, _tile_for(rows, 16), 512, F32)
        if last:
            (x_out,) = _resid((x1,), f, g_post_ffn[l], mod[l], 5, N, N)
            return x_out.reshape(1, N, D)
        x2, h = _resid((x1,), f, g_post_ffn[l], mod[l], 5, N, T, pre=(g_pre_mix[l + 1], mod[l + 1], 0))
        xs = (x2,)
```

```python
import functools

import numpy as np
import jax
import jax.numpy as jnp
from jax import lax
from jax.experimental import pallas as pl
from jax.experimental.pallas import tpu as pltpu

F32 = jnp.float32
BF16 = jnp.bfloat16

GRID_W = 64
EPS = 1e-6
ROPE_THETA = 10000.0
MLA_HEADS = 8
MLA_Q_RANK = 768
MLA_KV_RANK = 512
MLA_NOPE = 128
MLA_ROPE = 64
MLA_SCALE = (MLA_NOPE + MLA_ROPE) ** -0.5
GLA_HEADS = 4
GLA_DK = 128
GLA_DV = 256
GLA_GATE_RANK = 16
GLA_GATE_TAU = 16.0
GLA_CHUNK = 128
FNET_GROUPS = 8
FFT_N2 = 64
SWA_HEADS = 8
SWA_KV_HEADS = 2
SWA_GROUP = 4
SWA_BLOCK = 128
SWA_QBLOCK = 512
FLASH_SUBTILES = 4
SWA_SCALE = 128 ** -0.5
LANES = 128
ROW_TILE = 256

P_FU, P_GV, P_GR, P_CQ, P_SK, P_SQ = 0, 1024, 2048, 3072, 3840, 4096
P_CKV, P_GQ, P_GK, P_SV, P_KR, P_GLR = 5120, 5632, 6144, 6656, 6912, 7040
P_WIDTH = 7168

_NT = (((1,), (1,)), ((), ()))
_MIB = 1024 * 1024


def _params(sem, vmem_mib=40):
    return pltpu.CompilerParams(dimension_semantics=sem, vmem_limit_bytes=vmem_mib * _MIB)


def _silu(v):
    return v / (1.0 + jnp.exp(-v))


def _rms(v, g):
    return v * lax.rsqrt(jnp.mean(v * v, axis=-1, keepdims=True) + EPS) * g


def _pick(is_ctx, ref):
    return jnp.where(is_ctx, ref[1:2, :], ref[0:1, :])


def _ada_kernel(c_ref, w_ref, b_ref, o_ref):
    a = _silu(c_ref[...]).astype(BF16)
    o_ref[...] = jnp.dot(a, w_ref[...].astype(BF16), preferred_element_type=F32) + b_ref[...]


def _ada(cc, w_ada, b_ada):
    L, D, N6 = w_ada.shape
    tn = 512
    return pl.pallas_call(
        _ada_kernel,
        grid=(L, N6 // tn),
        in_specs=[pl.BlockSpec((8, D), lambda l, j: (0, 0)),
                  pl.BlockSpec((None, D, tn), lambda l, j: (l, 0, j)),
                  pl.BlockSpec((None, 1, tn), lambda l, j: (l, 0, j))],
        out_specs=pl.BlockSpec((None, 8, tn), lambda l, j: (l, 0, j)),
        out_shape=jax.ShapeDtypeStruct((L, 8, N6), F32),
        compiler_params=_params(("arbitrary", "arbitrary")),
    )(cc, w_ada, b_ada.reshape(L, 1, N6))


def _rms_mod_kernel(n_lat_tiles, x_ref, c_ref, g_ref, sh_ref, sc_ref, h_ref):
    is_ctx = pl.program_id(0) >= n_lat_tiles
    xv = jnp.where(is_ctx, c_ref[...], x_ref[...])
    y = _rms(xv, g_ref[...])
    h_ref[...] = (y * (1.0 + _pick(is_ctx, sc_ref)) + _pick(is_ctx, sh_ref)).astype(BF16)


def _rms_mod(x2d, ctx2d, g, mod, k_shift):
    N, D = x2d.shape
    C = ctx2d.shape[0]
    tm = ROW_TILE
    nl, nc = N // tm, C // tm
    return pl.pallas_call(
        functools.partial(_rms_mod_kernel, nl),
        grid=(nl + nc,),
        in_specs=[pl.BlockSpec((tm, D), lambda i: (jnp.minimum(i, nl - 1), 0)),
                  pl.BlockSpec((tm, D), lambda i: (jnp.maximum(i - nl, 0), 0)),
                  pl.BlockSpec((1, D), lambda i: (0, 0)),
                  pl.BlockSpec((8, D), lambda i: (0, k_shift)),
                  pl.BlockSpec((8, D), lambda i: (0, k_shift + 1))],
        out_specs=pl.BlockSpec((tm, D), lambda i: (i, 0)),
        out_shape=jax.ShapeDtypeStruct((N + C, D), BF16),
        compiler_params=_params(("arbitrary",)),
    )(x2d, ctx2d, g.reshape(1, D), mod, mod)


def _resid_kernel(n_lat_tiles, split_x, with_h, *refs):
    refs = list(refs)
    x_ref = refs.pop(0)
    c_ref = refs.pop(0) if split_x else None
    y_ref, gpost_ref, gate_ref = refs.pop(0), refs.pop(0), refs.pop(0)
    if with_h:
        gpre_ref, sh_ref, sc_ref = refs.pop(0), refs.pop(0), refs.pop(0)
    x1_ref = refs.pop(0)
    is_ctx = pl.program_id(0) >= n_lat_tiles
    xv = jnp.where(is_ctx, c_ref[...], x_ref[...]) if split_x else x_ref[...]
    x1 = xv + _pick(is_ctx, gate_ref) * _rms(y_ref[...], gpost_ref[...])
    x1_ref[...] = x1
    if with_h:
        h_ref = refs.pop(0)
        n = _rms(x1, gpre_ref[...])
        h_ref[...] = (n * (1.0 + _pick(is_ctx, sc_ref)) + _pick(is_ctx, sh_ref)).astype(BF16)


def _resid(xs, y, g_post, mod_gate, k_gate, n_lat, n_rows, pre=None):
    D = y.shape[1]
    tm = ROW_TILE
    nl = n_lat // tm
    nt = n_rows // tm
    split_x = len(xs) == 2
    row = lambda i: (i, 0)
    vec = lambda i: (0, 0)
    if split_x:
        in_specs = [pl.BlockSpec((tm, D), lambda i: (jnp.minimum(i, nl - 1), 0)),
                    pl.BlockSpec((tm, D), lambda i: (jnp.maximum(i - nl, 0), 0))]
    else:
        in_specs = [pl.BlockSpec((tm, D), row)]
    in_specs += [pl.BlockSpec((tm, D), row), pl.BlockSpec((1, D), vec),
                 pl.BlockSpec((8, D), lambda i: (0, k_gate))]
    args = list(xs) + [y, g_post.reshape(1, D), mod_gate]
    out_specs = [pl.BlockSpec((tm, D), row)]
    out_shape = [jax.ShapeDtypeStruct((n_rows, D), F32)]
    if pre is not None:
        g_pre, mod_pre, k_shift = pre
        in_specs += [pl.BlockSpec((1, D), vec),
                     pl.BlockSpec((8, D), lambda i: (0, k_shift)),
                     pl.BlockSpec((8, D), lambda i: (0, k_shift + 1))]
        args += [g_pre.reshape(1, D), mod_pre, mod_pre]
        out_specs.append(pl.BlockSpec((tm, D), row))
        out_shape.append(jax.ShapeDtypeStruct((n_rows, D), BF16))
    return pl.pallas_call(
        functools.partial(_resid_kernel, nl, split_x, pre is not None),
        grid=(nt,), in_specs=in_specs, out_specs=out_specs, out_shape=out_shape,
        compiler_params=_params(("arbitrary",)),
    )(*args)


def _mm_kernel(n_in, *refs):
    w_ref, o_ref = refs[n_in], refs[n_in + 1]
    kc = w_ref.shape[0] // n_in
    acc = None
    for k in range(n_in):
        d = jnp.dot(refs[k][...].astype(BF16), w_ref[kc * k:kc * (k + 1), :].astype(BF16),
                    preferred_element_type=F32)
        acc = d if acc is None else acc + d
    o_ref[...] = acc.astype(o_ref.dtype)


def _matmul(a_list, w3, layer, tm, tn, out_dtype, single_buffer_rows=False, m_rows=None):
    M = a_list[0].shape[0] if m_rows is None else m_rows
    _, K, Nn = w3.shape
    kc = K // len(a_list)
    mode = dict(pipeline_mode=pl.Buffered(1)) if single_buffer_rows else {}
    return pl.pallas_call(
        functools.partial(_mm_kernel, len(a_list)),
        grid=(M // tm, Nn // tn),
        in_specs=[pl.BlockSpec((tm, kc), lambda i, j: (i, 0), **mode) for _ in a_list]
        + [pl.BlockSpec((None, K, tn), lambda i, j: (layer, 0, j))],
        out_specs=pl.BlockSpec((tm, tn), lambda i, j: (i, j)),
        out_shape=jax.ShapeDtypeStruct((M, Nn), out_dtype),
        compiler_params=_params(("arbitrary", "arbitrary"), 56),
    )(*a_list, w3)


def _rope_tables(n_lat, n_ctx, n_real, half):
    t = jnp.arange(n_lat, dtype=jnp.int32)
    rows = (t // GRID_W).astype(F32)[:, None]
    cols = (t % GRID_W).astype(F32)[:, None]
    lane = np.arange(LANES)
    grp = (lane // (2 * half)) % 2
    jj = lane % (2 * half)
    first = jj < half
    real = lane < n_real
    inv = ROPE_THETA ** (-jnp.asarray(jj % half, F32) / half)
    ang = jnp.where(jnp.asarray(grp == 0)[None, :], rows, cols) * inv[None, :]
    cos, sin = jnp.cos(ang), jnp.sin(ang)
    real_j, first_j = jnp.asarray(real)[None, :], jnp.asarray(first)[None, :]
    cos_t = jnp.where(real_j, cos, 1.0)
    sa_t = jnp.where(real_j & first_j, -sin, 0.0)
    sb_t = jnp.where(real_j & (~first_j), sin, 0.0)
    pad = lambda a, v: jnp.concatenate([a, jnp.full((n_ctx, LANES), v, F32)], axis=0)
    return pad(cos_t, 1.0), pad(sa_t, 0.0), pad(sb_t, 0.0)


def _rope(v, cos, sa, sb, half):
    return v * cos + pltpu.roll(v, LANES - half, 1) * sa + pltpu.roll(v, half, 1) * sb


def _mla_q_kernel(cq_ref, g_ref, w_ref, cos_ref, sa_ref, sb_ref, o_ref):
    n = _rms(cq_ref[...], g_ref[...]).astype(BF16)
    a = jnp.dot(n, w_ref[...], preferred_element_type=F32)
    cos, sa, sb = cos_ref[...], sa_ref[...], sb_ref[...]
    for h in range(MLA_HEADS):
        lo = 2 * LANES * h
        o_ref[:, lo:lo + LANES] = (a[:, lo:lo + LANES] * MLA_SCALE).astype(BF16)
        r = _rope(a[:, lo + LANES:lo + 2 * LANES], cos, sa, sb, MLA_ROPE // 4)
        o_ref[:, lo + LANES:lo + 2 * LANES] = (r * MLA_SCALE).astype(BF16)


def _prep_tile(T):
    return 3 * ROW_TILE if T % (3 * ROW_TILE) == 0 else ROW_TILE


def _mla_q(p, g_qa, w_uq_ext, tabs):
    T = p.shape[0]
    tm = _prep_tile(T)
    tab = pl.BlockSpec((tm, LANES), lambda i: (i, 0))
    return pl.pallas_call(
        _mla_q_kernel,
        grid=(T // tm,),
        in_specs=[pl.BlockSpec((tm, MLA_Q_RANK), lambda i: (i, P_CQ // MLA_Q_RANK)),
                  pl.BlockSpec((1, MLA_Q_RANK), lambda i: (0, 0)),
                  pl.BlockSpec((MLA_Q_RANK, 2 * LANES * MLA_HEADS), lambda i: (0, 0)),
                  tab, tab, tab],
        out_specs=pl.BlockSpec((tm, 2 * LANES * MLA_HEADS), lambda i: (i, 0)),
        out_shape=jax.ShapeDtypeStruct((T, 2 * LANES * MLA_HEADS), BF16),
        compiler_params=_params(("arbitrary",)),
    )(p, g_qa.reshape(1, -1), w_uq_ext, *tabs)


def _mla_kv_kernel(ckv_ref, kr_ref, g_ref, w_ref, cos_ref, sa_ref, sb_ref, k_ref, v_ref):
    n = _rms(ckv_ref[...], g_ref[...]).astype(BF16)
    a = jnp.dot(n, w_ref[...], preferred_element_type=F32)
    kr = _rope(kr_ref[...], cos_ref[...], sa_ref[...], sb_ref[...], MLA_ROPE // 4).astype(BF16)
    ones = jnp.ones(kr.shape, BF16)
    for h in range(MLA_HEADS):
        lo = 2 * LANES * h
        k_ref[:, lo:lo + LANES] = a[:, lo:lo + LANES].astype(BF16)
        k_ref[:, lo + LANES:lo + 2 * LANES] = kr
        v_ref[:, lo:lo + LANES] = a[:, lo + LANES:lo + 2 * LANES].astype(BF16)
        v_ref[:, lo + LANES:lo + 2 * LANES] = ones


def _mla_kv(p, g_kva, w_ukv, tabs):
    T = p.shape[0]
    tm = _prep_tile(T)
    tab = pl.BlockSpec((tm, LANES), lambda i: (i, 0))
    wide = 2 * LANES * MLA_HEADS
    return pl.pallas_call(
        _mla_kv_kernel,
        grid=(T // tm,),
        in_specs=[pl.BlockSpec((tm, MLA_KV_RANK), lambda i: (i, P_CKV // MLA_KV_RANK)),
                  pl.BlockSpec((tm, LANES), lambda i: (i, P_KR // LANES)),
                  pl.BlockSpec((1, MLA_KV_RANK), lambda i: (0, 0)),
                  pl.BlockSpec((MLA_KV_RANK, wide), lambda i: (0, 0)),
                  tab, tab, tab],
        out_specs=[pl.BlockSpec((tm, wide), lambda i: (i, 0)),
                   pl.BlockSpec((tm, wide), lambda i: (i, 0))],
        out_shape=[jax.ShapeDtypeStruct((T, wide), BF16),
                   jax.ShapeDtypeStruct((T, wide), BF16)],
        compiler_params=_params(("arbitrary",)),
    )(p, p, g_kva.reshape(1, -1), w_ukv, *tabs)


def _flash_kernel(q_ref, k_ref, v_ref, *rest):
    o_ref, m_sc, acc_sc = rest[-3:]
    kk = pl.program_id(2)

    @pl.when(kk == 0)
    def _():
        m_sc[...] = jnp.full(m_sc.shape, -1e30, F32)
        acc_sc[...] = jnp.zeros(acc_sc.shape, F32)

    rc = q_ref.shape[0] // FLASH_SUBTILES
    subs = [slice(r * rc, (r + 1) * rc) for r in range(FLASH_SUBTILES)]
    scores = [lax.dot_general(q_ref[rows, :], k_ref[...], _NT, preferred_element_type=F32) for rows in subs]
    for rows, s in zip(subs, scores):
        m_prev = m_sc[rows, :]
        m_new = jnp.maximum(m_prev, jnp.max(s, axis=1, keepdims=True))
        alpha = jnp.exp(m_prev - m_new)
        p = jnp.exp((s - m_new).astype(BF16))
        acc_sc[rows, :] = alpha * acc_sc[rows, :] + jnp.dot(p, v_ref[...], preferred_element_type=F32)
        m_sc[rows, :] = m_new

    @pl.when(kk == pl.num_programs(2) - 1)
    def _():
        acc = acc_sc[...]
        o_ref[...] = (acc[:, :LANES] / acc[:, LANES:LANES + 1]).astype(o_ref.dtype)


def _flash(q, k, v, out_prev, n_rows, q_blk0, n_qblk, k_blk0, n_kblk, tq, tk):
    dq = 2 * LANES
    in_specs = [pl.BlockSpec((tq, dq), lambda h, i, kk: (i + q_blk0, h)),
                pl.BlockSpec((tk, dq), lambda h, i, kk: (kk + k_blk0, h)),
                pl.BlockSpec((tk, dq), lambda h, i, kk: (kk + k_blk0, h))]
    args = [q, k, v]
    aliases = {}
    if out_prev is not None:
        in_specs.append(pl.BlockSpec(memory_space=pl.ANY))
        args.append(out_prev)
        aliases = {3: 0}
    return pl.pallas_call(
        _flash_kernel,
        grid=(MLA_HEADS, n_qblk, n_kblk),
        in_specs=in_specs,
        out_specs=pl.BlockSpec((tq, LANES), lambda h, i, kk: (i + q_blk0, h)),
        out_shape=jax.ShapeDtypeStruct((n_rows, MLA_HEADS * LANES), BF16),
        scratch_shapes=[pltpu.VMEM((tq, 1), F32), pltpu.VMEM((tq, dq), F32)],
        input_output_aliases=aliases,
        compiler_params=_params(("arbitrary", "arbitrary", "arbitrary"), 48),
    )(*args)


def _gla_direction(q, k, v, glr, wg, bg, tri, last_row, st_ref, o_ref):
    z = jnp.dot(glr.astype(BF16), wg, preferred_element_type=F32) + bg
    la = (jnp.minimum(z, 0.0) - jnp.log(1.0 + jnp.exp(-jnp.abs(z)))) * (1.0 / GLA_GATE_TAU)
    ones = jnp.where(tri, 1.0, 0.0).astype(BF16)
    la_hi = la.astype(BF16)
    la_lo = (la - la_hi.astype(F32)).astype(BF16)
    b = (jnp.dot(ones, la_hi, preferred_element_type=F32)
         + jnp.dot(ones, la_lo, preferred_element_type=F32))
    b_tot = b[last_row:last_row + 1, :]
    qt = q * (GLA_DK ** -0.5) * jnp.exp(b)
    kt = k * jnp.exp(-b)
    kh = k * jnp.exp(b_tot - b)
    dec = jnp.exp(b_tot)
    for h in range(GLA_HEADS):
        ks = slice(GLA_DK * h, GLA_DK * (h + 1))
        vs = slice(GLA_DV * h, GLA_DV * (h + 1))
        qh = qt[:, ks].astype(BF16)
        vh = v[:, vs]
        st = st_ref[h]
        inter = lax.dot_general(qh, st.astype(BF16), _NT, preferred_element_type=F32)
        sc = lax.dot_general(qh, kt[:, ks].astype(BF16), _NT, preferred_element_type=F32)
        sc = jnp.where(tri, sc, 0.0)
        intra = jnp.dot(sc.astype(BF16), vh.astype(BF16), preferred_element_type=F32)
        o_ref[:, vs] = inter + intra
        st_ref[h] = st * dec[:, ks] + jnp.dot(vh.T.astype(BF16), kh[:, ks].astype(BF16),
                                               preferred_element_type=F32)


def _gla_kernel(qf, kf, vf, gf, qb, kb, vb, gb, wf, bf, wb, bb, of_ref, ob_ref, stf, stb):
    @pl.when(pl.program_id(0) == 0)
    def _():
        stf[...] = jnp.zeros(stf.shape, F32)
        stb[...] = jnp.zeros(stb.shape, F32)

    cs = qf.shape[0]
    t_i = lax.broadcasted_iota(jnp.int32, (cs, cs), 0)
    s_i = lax.broadcasted_iota(jnp.int32, (cs, cs), 1)
    _gla_direction(qf[...], kf[...], vf[...], gf[...], wf[...], bf[...], s_i <= t_i, cs - 1, stf, of_ref)
    _gla_direction(qb[...], kb[...], vb[...], gb[...], wb[...], bb[...], s_i >= t_i, 0, stb, ob_ref)


def _gla(p, n_lat, wf_ext, bf, wb_ext, bb):
    T = p.shape[0]
    cs = GLA_CHUNK
    nch = T // cs
    nlc = n_lat // cs
    fwd = lambda c: (c + nlc) % nch
    bwd = lambda c: nch - 1 - c
    dk, dv = GLA_HEADS * GLA_DK, GLA_HEADS * GLA_DV

    def specs(idx):
        return [pl.BlockSpec((cs, dk), lambda c: (idx(c), P_GQ // dk)),
                pl.BlockSpec((cs, dk), lambda c: (idx(c), P_GK // dk)),
                pl.BlockSpec((cs, dv), lambda c: (idx(c), P_GV // dv)),
                pl.BlockSpec((cs, LANES), lambda c: (idx(c), P_GLR // LANES))]

    wspec = pl.BlockSpec((LANES, dk), lambda c: (0, 0))
    bspec = pl.BlockSpec((1, dk), lambda c: (0, 0))
    return pl.pallas_call(
        _gla_kernel,
        grid=(nch,),
        in_specs=specs(fwd) + specs(bwd) + [wspec, bspec, wspec, bspec],
        out_specs=[pl.BlockSpec((cs, dv), lambda c: (fwd(c), 0)),
                   pl.BlockSpec((cs, dv), lambda c: (bwd(c), 0))],
        out_shape=[jax.ShapeDtypeStruct((T, dv), F32), jax.ShapeDtypeStruct((T, dv), F32)],
        scratch_shapes=[pltpu.VMEM((GLA_HEADS, GLA_DV, GLA_DK), F32),
                        pltpu.VMEM((GLA_HEADS, GLA_DV, GLA_DK), F32)],
        compiler_params=_params(("arbitrary",)),
    )(p, p, p, p, p, p, p, p, wf_ext, bf.reshape(1, dk), wb_ext, bb.reshape(1, dk))


def _gla_out_kernel(of_ref, ob_ref, gr_ref, g_ref, o_ref):
    o = of_ref[...] + ob_ref[...]
    gr = gr_ref[...]
    for h in range(GLA_HEADS):
        vs = slice(GLA_DV * h, GLA_DV * (h + 1))
        o_ref[:, vs] = (_rms(o[:, vs], g_ref[...]) * _silu(gr[:, vs])).astype(BF16)


def _gla_out(o_f, o_b, p, g_gla):
    T, dv = o_f.shape
    tm = _prep_tile(T)
    row = pl.BlockSpec((tm, dv), lambda i: (i, 0))
    return pl.pallas_call(
        _gla_out_kernel,
        grid=(T // tm,),
        in_specs=[row, row, pl.BlockSpec((tm, dv), lambda i: (i, P_GR // dv)),
                  pl.BlockSpec((1, GLA_DV), lambda i: (0, 0))],
        out_specs=row,
        out_shape=jax.ShapeDtypeStruct((T, dv), BF16),
        compiler_params=_params(("arbitrary",)),
    )(o_f, o_b, p, g_gla.reshape(1, GLA_DV))


def _dft_cos_sin(n):
    k = np.arange(n)
    ang = 2.0 * np.pi * ((k[:, None] * k[None, :]) % n) / n
    return np.cos(ang), np.sin(ang)


def _channel_dft(xb, wd):
    zr, zi = [], []
    for g in range(FNET_GROUPS):
        z = jnp.dot(xb[:, LANES * g:LANES * (g + 1)], wd, preferred_element_type=F32)
        zr.append(z[:, :LANES])
        zi.append(z[:, LANES:])
    return jnp.concatenate(zr, axis=1), jnp.concatenate(zi, axis=1)


def _fft_a_kernel(n1, x_ref, wd_ref, m_ref, bre_ref, bim_ref):
    for n2 in range(FFT_N2):
        xs = x_ref[pl.ds(n2, n1, stride=FFT_N2), :].astype(BF16)
        z = jnp.dot(xs, wd_ref[...], preferred_element_type=F32)
        zs = jnp.concatenate([z[:, :LANES], z[:, LANES:]], axis=0).astype(BF16)
        b = jnp.dot(m_ref[n2], zs, preferred_element_type=F32)
        bre_ref[n2 * n1:(n2 + 1) * n1, :] = b[:n1]
        bim_ref[n2 * n1:(n2 + 1) * n1, :] = b[n1:]


def _fft_b_kernel(n1, scale, c_ref, s_ref, bre_ref, bim_ref, o_ref):
    grp = 8
    for t in range(n1 // grp):
        rows = [pl.ds(grp * t + r, FFT_N2, stride=n1) for r in range(grp)]
        br = jnp.concatenate([bre_ref[rw, :] for rw in rows], axis=1).astype(BF16)
        bi = jnp.concatenate([bim_ref[rw, :] for rw in rows], axis=1).astype(BF16)
        o = (jnp.dot(c_ref[...], br, preferred_element_type=F32)
             + jnp.dot(s_ref[...], bi, preferred_element_type=F32)) * scale
        for r in range(grp):
            o_ref[rows[r], :] = o[:, LANES * r:LANES * (r + 1)]


def _fft_ctx_kernel(scale, x_ref, wd_ref, cs_ref, prev_ref, o_ref):
    del prev_ref
    zr, zi = _channel_dft(x_ref[...].astype(BF16), wd_ref[...])
    z = jnp.concatenate([zr, zi], axis=0).astype(BF16)
    o_ref[...] = jnp.dot(cs_ref[...], z, preferred_element_type=F32) * scale


def _fourier(p, n_lat, n_ctx):
    T = p.shape[0]
    gw = FNET_GROUPS * LANES
    n1 = n_lat // FFT_N2
    cd, sd = _dft_cos_sin(LANES)
    as_bf16 = lambda a: jnp.asarray(a, F32).astype(BF16)
    wd = as_bf16(np.concatenate([cd, -sd], axis=1))
    k1 = jnp.arange(n1, dtype=jnp.int32)[None, :, None]
    pos = (FFT_N2 * jnp.arange(n1, dtype=jnp.int32)[None, None, :]
           + jnp.arange(FFT_N2, dtype=jnp.int32)[:, None, None])
    ang = (2.0 * np.pi / n_lat) * ((k1 * pos) % n_lat).astype(F32)
    gc, gs = jnp.cos(ang), jnp.sin(ang)
    m = jnp.concatenate([jnp.concatenate([gc, gs], axis=2),
                         jnp.concatenate([-gs, gc], axis=2)], axis=1).astype(BF16)
    col = lambda g: (0, g)
    bre, bim = pl.pallas_call(
        functools.partial(_fft_a_kernel, n1),
        grid=(FNET_GROUPS,),
        in_specs=[pl.BlockSpec((n_lat, LANES), lambda g: (0, P_FU // LANES + g)),
                  pl.BlockSpec((LANES, 2 * LANES), lambda g: (0, 0)),
                  pl.BlockSpec((FFT_N2, 2 * n1, 2 * n1), lambda g: (0, 0, 0))],
        out_specs=[pl.BlockSpec((n_lat, LANES), col)] * 2,
        out_shape=[jax.ShapeDtypeStruct((n_lat, gw), F32)] * 2,
        compiler_params=_params(("arbitrary",), 48),
    )(p, wd, m)

    c2, s2 = _dft_cos_sin(FFT_N2)
    mspec = pl.BlockSpec((FFT_N2, FFT_N2), lambda g: (0, 0))
    y = pl.pallas_call(
        functools.partial(_fft_b_kernel, n1, float((n_lat * LANES) ** -0.5)),
        grid=(FNET_GROUPS,),
        in_specs=[mspec, mspec, pl.BlockSpec((n_lat, LANES), col), pl.BlockSpec((n_lat, LANES), col)],
        out_specs=pl.BlockSpec((n_lat, LANES), col),
        out_shape=jax.ShapeDtypeStruct((T, gw), F32),
        compiler_params=_params(("arbitrary",)),
    )(as_bf16(c2), as_bf16(s2), bre, bim)

    cc, sc = _dft_cos_sin(n_ctx)
    return pl.pallas_call(
        functools.partial(_fft_ctx_kernel, float((n_ctx * LANES) ** -0.5)),
        grid=(1,),
        in_specs=[pl.BlockSpec((n_ctx, gw), lambda j: (n_lat // n_ctx, P_FU // gw)),
                  pl.BlockSpec((LANES, 2 * LANES), lambda j: (0, 0)),
                  pl.BlockSpec((n_ctx, 2 * n_ctx), lambda j: (0, 0)),
                  pl.BlockSpec(memory_space=pl.ANY)],
        out_specs=pl.BlockSpec((n_ctx, gw), lambda j: (n_lat // n_ctx, 0)),
        out_shape=jax.ShapeDtypeStruct((T, gw), F32),
        input_output_aliases={3: 0},
        compiler_params=_params(("arbitrary",)),
    )(p, wd, as_bf16(np.concatenate([cc, sc], axis=1)), y)


def _swa_prep_kernel(q_ref, k_ref, v_ref, cos_ref, sa_ref, sb_ref, qo_ref, ko_ref, vo_ref):
    cos, sa, sb = cos_ref[...], sa_ref[...], sb_ref[...]
    for h in range(SWA_HEADS):
        sl = slice(LANES * h, LANES * (h + 1))
        qo_ref[:, sl] = (_rope(q_ref[:, sl], cos, sa, sb, 32) * SWA_SCALE).astype(BF16)
    for h in range(SWA_KV_HEADS):
        sl = slice(LANES * h, LANES * (h + 1))
        ko_ref[:, sl] = _rope(k_ref[:, sl], cos, sa, sb, 32).astype(BF16)
    vo_ref[...] = v_ref[...].astype(BF16)


def _swa_prep(p, tabs):
    T = p.shape[0]
    tm = _prep_tile(T)
    qw, kw = SWA_HEADS * LANES, SWA_KV_HEADS * LANES
    tab = pl.BlockSpec((tm, LANES), lambda i: (i, 0))
    return pl.pallas_call(
        _swa_prep_kernel,
        grid=(T // tm,),
        in_specs=[pl.BlockSpec((tm, qw), lambda i: (i, P_SQ // qw)),
                  pl.BlockSpec((tm, kw), lambda i: (i, P_SK // kw)),
                  pl.BlockSpec((tm, kw), lambda i: (i, P_SV // kw)),
                  tab, tab, tab],
        out_specs=[pl.BlockSpec((tm, qw), lambda i: (i, 0)),
                   pl.BlockSpec((tm, kw), lambda i: (i, 0)),
                   pl.BlockSpec((tm, kw), lambda i: (i, 0))],
        out_shape=[jax.ShapeDtypeStruct((T, qw), BF16),
                   jax.ShapeDtypeStruct((T, kw), BF16),
                   jax.ShapeDtypeStruct((T, kw), BF16)],
        compiler_params=_params(("arbitrary",)),
    )(p, p, p, *tabs)


def _sink_attend(q, kcat, vcat, mask, sink):
    s = lax.dot_general(q, kcat, _NT, preferred_element_type=F32)
    if mask is not None:
        s = jnp.where(mask, s, -1e30)
    m = jnp.maximum(jnp.max(s, axis=1, keepdims=True), sink)
    pr = jnp.exp((s - m).astype(BF16))
    den = jnp.sum(pr.astype(F32), axis=1, keepdims=True) + jnp.exp(sink - m)
    return jnp.dot(pr, vcat, preferred_element_type=F32) / den


def _swa_lat_kernel(q_ref, kc_ref, kp_ref, k0_ref, kn_ref, vc_ref, vp_ref, v0_ref, vn_ref,
                    sink_ref, o_ref):
    kvh, nb = pl.program_id(0), pl.program_id(1)
    n_ctx = kc_ref.shape[0]
    qb = q_ref.shape[0]
    win = SWA_BLOCK
    kcat = jnp.concatenate([kc_ref[...], kp_ref[...], k0_ref[...], kn_ref[...]], axis=0)
    vcat = jnp.concatenate([vc_ref[...], vp_ref[...], v0_ref[...], vn_ref[...]], axis=0)
    n_keys = n_ctx + qb + 2 * win
    i = lax.broadcasted_iota(jnp.int32, (qb, n_keys), 0)
    j = lax.broadcasted_iota(jnp.int32, (qb, n_keys), 1)
    off = j - (n_ctx + win)
    in_band = jnp.abs(i - off) <= win
    exists = ((off >= 0) | (nb > 0)) & ((off < qb) | (nb < pl.num_programs(1) - 1))
    mask = (j < n_ctx) | (in_band & exists)
    for g in range(SWA_GROUP):
        sl = slice(LANES * g, LANES * (g + 1))
        sink = sink_ref[pl.ds(kvh * SWA_GROUP + g, 1), 0:1]
        o_ref[:, sl] = _sink_attend(q_ref[:, sl], kcat, vcat, mask, sink).astype(BF16)


def _swa_ctx_kernel(q_ref, k_ref, v_ref, sink_ref, prev_ref, o_ref):
    del prev_ref
    kvh = pl.program_id(0)
    for g in range(SWA_GROUP):
        sl = slice(LANES * g, LANES * (g + 1))
        sink = sink_ref[pl.ds(kvh * SWA_GROUP + g, 1), 0:1]
        o_ref[:, sl] = _sink_attend(q_ref[:, sl], k_ref[...], v_ref[...], None, sink).astype(BF16)


def _swa(qs, ks, vs, sink, n_lat, n_ctx):
    T = qs.shape[0]
    win = SWA_BLOCK
    qb = SWA_QBLOCK
    nb = n_lat // qb
    per = qb // win
    gq = SWA_GROUP * LANES
    sink2d = jnp.broadcast_to(sink.astype(F32)[:, None], (SWA_HEADS, LANES))
    cblk = n_lat // n_ctx
    ctx_spec = pl.BlockSpec((n_ctx, LANES), lambda h, b: (cblk, h))
    prev_spec = pl.BlockSpec((win, LANES), lambda h, b: (jnp.maximum(b * per - 1, 0), h))
    cur_spec = pl.BlockSpec((qb, LANES), lambda h, b: (b, h))
    next_spec = pl.BlockSpec((win, LANES), lambda h, b: (jnp.minimum((b + 1) * per, nb * per - 1), h))
    sink_spec = pl.BlockSpec((SWA_HEADS, LANES), lambda h, b: (0, 0))
    o_lat = pl.pallas_call(
        _swa_lat_kernel,
        grid=(SWA_KV_HEADS, nb),
        in_specs=[pl.BlockSpec((qb, gq), lambda h, b: (b, h)),
                  ctx_spec, prev_spec, cur_spec, next_spec,
                  ctx_spec, prev_spec, cur_spec, next_spec, sink_spec],
        out_specs=pl.BlockSpec((qb, gq), lambda h, b: (b, h)),
        out_shape=jax.ShapeDtypeStruct((T, SWA_HEADS * LANES), BF16),
        compiler_params=_params(("arbitrary", "arbitrary")),
    )(qs, ks, ks, ks, ks, vs, vs, vs, vs, sink2d)
    return pl.pallas_call(
        _swa_ctx_kernel,
        grid=(SWA_KV_HEADS,),
        in_specs=[pl.BlockSpec((n_ctx, gq), lambda h: (cblk, h)),
                  pl.BlockSpec((n_ctx, LANES), lambda h: (cblk, h)),
                  pl.BlockSpec((n_ctx, LANES), lambda h: (cblk, h)),
                  pl.BlockSpec((SWA_HEADS, LANES), lambda h: (0, 0)),
                  pl.BlockSpec(memory_space=pl.ANY)],
        out_specs=pl.BlockSpec((n_ctx, gq), lambda h: (cblk, h)),
        out_shape=jax.ShapeDtypeStruct((T, SWA_HEADS * LANES), BF16),
        input_output_aliases={4: 0},
        compiler_params=_params(("arbitrary",)),
    )(qs, ks, vs, sink2d, o_lat)


GATE_TILE = 128
GATE_HALO = 16


def _gate_kernel(tm, n_lat, n_tot, ug_ref, ua_ref, gp_ref, gn_ref, ap_ref, an_ref, cwg_ref, cwa_ref,
                 cbg_ref, cba_ref, o_ref, ubuf):
    i = pl.program_id(0)
    first = (i == 0) | (i * tm == n_lat)
    last = ((i + 1) * tm == n_lat) | ((i + 1) * tm == n_tot)
    rows = tm + 2 * GATE_HALO
    out_r = lax.broadcasted_iota(jnp.int32, (2 * tm, rows), 0)
    src_r = lax.broadcasted_iota(jnp.int32, (2 * tm, rows), 1)
    want = jnp.where(out_r < tm, out_r + (GATE_HALO - 1), out_r - tm + (GATE_HALO + 1))
    pick = jnp.where(src_r == want, 1.0, 0.0).astype(BF16)

    def conv(u_ref, p_ref, n_ref, cw_ref, cb_ref):
        ubuf[0:GATE_HALO, :] = jnp.where(first, jnp.zeros_like(p_ref[...]), p_ref[...])
        ubuf[GATE_HALO:GATE_HALO + tm, :] = u_ref[...]
        ubuf[GATE_HALO + tm:, :] = jnp.where(last, jnp.zeros_like(n_ref[...]), n_ref[...])
        shifted = jnp.dot(pick, ubuf[...], preferred_element_type=F32)
        um, up = shifted[:tm], shifted[tm:]
        return (um * cw_ref[0:1, :] + u_ref[...].astype(F32) * cw_ref[1:2, :] + up * cw_ref[2:3, :]
                + cb_ref[...])

    g = conv(ug_ref, gp_ref, gn_ref, cwg_ref, cbg_ref)
    a = conv(ua_ref, ap_ref, an_ref, cwa_ref, cba_ref)
    o_ref[...] = (_silu(g) * a).astype(BF16)


def _gate(u, n_lat, conv_w, conv_b):
    T = u.shape[0]
    dff = u.shape[1] // 2
    tm = GATE_TILE
    hb = tm // GATE_HALO
    last_hb = T // GATE_HALO - 1
    cb2 = conv_b.reshape(1, 2 * dff)
    prev = lambda i: jnp.maximum(i * hb - 1, 0)
    nxt = lambda i: jnp.minimum((i + 1) * hb, last_hb)
    return pl.pallas_call(
        functools.partial(_gate_kernel, tm, n_lat, T),
        grid=(T // tm,),
        in_specs=[pl.BlockSpec((tm, dff), lambda i: (i, 0)),
                  pl.BlockSpec((tm, dff), lambda i: (i, 1)),
                  pl.BlockSpec((GATE_HALO, dff), lambda i: (prev(i), 0)),
                  pl.BlockSpec((GATE_HALO, dff), lambda i: (nxt(i), 0)),
                  pl.BlockSpec((GATE_HALO, dff), lambda i: (prev(i), 1)),
                  pl.BlockSpec((GATE_HALO, dff), lambda i: (nxt(i), 1)),
                  pl.BlockSpec((3, dff), lambda i: (0, 0)),
                  pl.BlockSpec((3, dff), lambda i: (0, 1)),
                  pl.BlockSpec((1, dff), lambda i: (0, 0)),
                  pl.BlockSpec((1, dff), lambda i: (0, 1))],
        out_specs=pl.BlockSpec((tm, dff), lambda i: (i, 0)),
        out_shape=jax.ShapeDtypeStruct((T, dff), BF16),
        scratch_shapes=[pltpu.VMEM((tm + 2 * GATE_HALO, dff), BF16)],
        compiler_params=_params(("arbitrary",), 48),
    )(u, u, u, u, u, u, conv_w, conv_w, cb2, cb2)


def _pad_cols(w, width):
    return jnp.pad(w, ((0, 0), (0, width - w.shape[1])))


def _w_in_ext(w_in):
    cq, ckv, kr, gq, gk, gv, gr, glr, fu, sq, sk, sv = jnp.split(
        w_in, np.cumsum([768, 512, 64, 512, 512, 1024, 1024, 32, 1024, 1024, 256]).tolist(), axis=2)
    pad = lambda w: jnp.pad(w, ((0, 0), (0, 0), (0, LANES - w.shape[2])))
    parts = [fu, gv, gr, cq, sk, sq, ckv, gq, gk, sv, pad(kr), pad(glr)]
    return jnp.concatenate(parts, axis=2).astype(BF16)


def _tile_for(total, parts, mult=16):
    t = total // parts
    assert t * parts == total and t % mult == 0, (total, parts)
    return t


def _key_tile(total):
    best = LANES
    for t in range(LANES, 3072 + 1, LANES):
        if total % t == 0:
            best = t
    return best


def kernel(x, c, ctx, c_ctx, w_ada, b_ada, g_pre_mix, g_post_mix, g_pre_ffn, g_post_ffn, w_in, g_qa, w_uq,
           g_kva, w_ukv, w_gate_f, b_gate_f, w_gate_b, b_gate_b, g_gla, swa_sink, w_out, w_up, conv_w, conv_b,
           w_down):
    B, N, D = x.shape
    C = ctx.shape[1]
    T = N + C
    L = w_ada.shape[0]
    dff = w_down.shape[1]
    assert B == 1 and D == 4096 and N % (GRID_W * 16) == 0 and C % ROW_TILE == 0 and N % C == 0
    x2d, ctx2d = x[0], ctx[0]

    cc = jnp.zeros((8, D), F32).at[0].set(c[0]).at[1].set(c_ctx)
    mod = _ada(cc, w_ada, b_ada)
    mla_tabs = _rope_tables(N, C, MLA_ROPE, MLA_ROPE // 4)
    swa_tabs = _rope_tables(N, C, LANES, 32)

    tm_mm = _tile_for(T, 4)

    w_in_x = _w_in_ext(w_in)
    w_down_b = w_down.astype(BF16)
    tq, tk = _tile_for(N, 4), _key_tile(T)

    h = _rms_mod(x2d, ctx2d, g_pre_mix[0], mod[0], 0)
    xs = (x2d, ctx2d)
    for l in range(L):
        last = l == L - 1
        p = _matmul([h], w_in_x, l, tm_mm, 512, F32, True)

        w_uq_ext = jnp.pad(w_uq[l].reshape(MLA_Q_RANK, MLA_HEADS, MLA_NOPE + MLA_ROPE),
                           ((0, 0), (0, 0), (0, 2 * LANES - MLA_NOPE - MLA_ROPE))
                           ).reshape(MLA_Q_RANK, 2 * LANES * MLA_HEADS).astype(BF16)
        q_a = _mla_q(p, g_qa[l], w_uq_ext, mla_tabs)
        k_a, v_a = _mla_kv(p, g_kva[l], w_ukv[l].astype(BF16), mla_tabs)
        o_a = _flash(q_a, k_a, v_a, None, T, 0, N // tq, 0, T // tk, tq, tk)
        o_a = _flash(q_a, k_a, v_a, o_a, T, N // C, 1, N // C, 1, C, C)

        wf_ext = jnp.zeros((LANES, GLA_HEADS * GLA_DK), F32).at[:GLA_GATE_RANK].set(w_gate_f[l]).astype(BF16)
        wb_ext = jnp.zeros((LANES, GLA_HEADS * GLA_DK), F32).at[GLA_GATE_RANK:2 * GLA_GATE_RANK].set(
            w_gate_b[l]).astype(BF16)
        o_f, o_bk = _gla(p, N, wf_ext, b_gate_f[l], wb_ext, b_gate_b[l])
        o_b = _gla_out(o_f, o_bk, p, g_gla[l])

        o_c = _fourier(p, N, C)

        qs, ks, vs = _swa_prep(p, swa_tabs)
        o_d = _swa(qs, ks, vs, swa_sink[l], N, C)

        rows = N if last else T
        y = _matmul([o_a, o_b, o_c, o_d], w_out, l, _tile_for(rows, 4), 512, F32, True, m_rows=rows)
        x1, h2 = _resid(xs, y, g_post_mix[l], mod[l], 2, N, rows, pre=(g_pre_ffn[l], mod[l], 3))
        u = _matmul([h2], w_up, l, _tile_for(rows, 4), 512, BF16, True)
        act = _gate(u, N, conv_w[l], conv_b[l])
        f = _matmul([act], w_down_b, l, _tile_for(rows, 16), 512, F32)
        if last:
            (x_out,) = _resid((x1,), f, g_post_ffn[l], mod[l], 5, N, N)
            return x_out.reshape(1, N, D)
        x2, h = _resid((x1,), f, g_post_ffn[l], mod[l], 5, N, T, pre=(g_pre_mix[l + 1], mod[l + 1], 0))
        xs = (x2,)
```

```python
import functools

import numpy as np
import jax
import jax.numpy as jnp
from jax import lax
from jax.experimental import pallas as pl
from jax.experimental.pallas import tpu as pltpu

F32 = jnp.float32
BF16 = jnp.bfloat16

GRID_W = 64
EPS = 1e-6
ROPE_THETA = 10000.0
MLA_HEADS = 8
MLA_Q_RANK = 768
MLA_KV_RANK = 512
MLA_NOPE = 128
MLA_ROPE = 64
MLA_SCALE = (MLA_NOPE + MLA_ROPE) ** -0.5
GLA_HEADS = 4
GLA_DK = 128
GLA_DV = 256
GLA_GATE_RANK = 16
GLA_GATE_TAU = 16.0
GLA_CHUNK = 128
FNET_GROUPS = 8
FFT_N2 = 64
SWA_HEADS = 8
SWA_KV_HEADS = 2
SWA_GROUP = 4
SWA_BLOCK = 128
SWA_QBLOCK = 512
FLASH_SUBTILES = 4
SWA_SCALE = 128 ** -0.5
LANES = 128
ROW_TILE = 256

P_FU, P_GV, P_GR, P_CQ, P_SK, P_SQ = 0, 1024, 2048, 3072, 3840, 4096
P_CKV, P_GQ, P_GK, P_SV, P_KR, P_GLR = 5120, 5632, 6144, 6656, 6912, 7040
P_WIDTH = 7168

_NT = (((1,), (1,)), ((), ()))
_MIB = 1024 * 1024


def _params(sem, vmem_mib=40):
    return pltpu.CompilerParams(dimension_semantics=sem, vmem_limit_bytes=vmem_mib * _MIB)


def _silu(v):
    return v / (1.0 + jnp.exp(-v))


def _rms(v, g):
    return v * lax.rsqrt(jnp.mean(v * v, axis=-1, keepdims=True) + EPS) * g


def _pick(is_ctx, ref):
    return jnp.where(is_ctx, ref[1:2, :], ref[0:1, :])


def _ada_kernel(c_ref, w_ref, b_ref, o_ref):
    a = _silu(c_ref[...]).astype(BF16)
    o_ref[...] = jnp.dot(a, w_ref[...].astype(BF16), preferred_element_type=F32) + b_ref[...]


def _ada(cc, w_ada, b_ada):
    L, D, N6 = w_ada.shape
    tn = 512
    return pl.pallas_call(
        _ada_kernel,
        grid=(L, N6 // tn),
        in_specs=[pl.BlockSpec((8, D), lambda l, j: (0, 0)),
                  pl.BlockSpec((None, D, tn), lambda l, j: (l, 0, j)),
                  pl.BlockSpec((None, 1, tn), lambda l, j: (l, 0, j))],
        out_specs=pl.BlockSpec((None, 8, tn), lambda l, j: (l, 0, j)),
        out_shape=jax.ShapeDtypeStruct((L, 8, N6), F32),
        compiler_params=_params(("arbitrary", "arbitrary")),
    )(cc, w_ada, b_ada.reshape(L, 1, N6))


def _rms_mod_kernel(n_lat_tiles, x_ref, c_ref, g_ref, sh_ref, sc_ref, h_ref):
    is_ctx = pl.program_id(0) >= n_lat_tiles
    xv = jnp.where(is_ctx, c_ref[...], x_ref[...])
    y = _rms(xv, g_ref[...])
    h_ref[...] = (y * (1.0 + _pick(is_ctx, sc_ref)) + _pick(is_ctx, sh_ref)).astype(BF16)


def _rms_mod(x2d, ctx2d, g, mod, k_shift):
    N, D = x2d.shape
    C = ctx2d.shape[0]
    tm = ROW_TILE
    nl, nc = N // tm, C // tm
    return pl.pallas_call(
        functools.partial(_rms_mod_kernel, nl),
        grid=(nl + nc,),
        in_specs=[pl.BlockSpec((tm, D), lambda i: (jnp.minimum(i, nl - 1), 0)),
                  pl.BlockSpec((tm, D), lambda i: (jnp.maximum(i - nl, 0), 0)),
                  pl.BlockSpec((1, D), lambda i: (0, 0)),
                  pl.BlockSpec((8, D), lambda i: (0, k_shift)),
                  pl.BlockSpec((8, D), lambda i: (0, k_shift + 1))],
        out_specs=pl.BlockSpec((tm, D), lambda i: (i, 0)),
        out_shape=jax.ShapeDtypeStruct((N + C, D), BF16),
        compiler_params=_params(("arbitrary",)),
    )(x2d, ctx2d, g.reshape(1, D), mod, mod)


def _resid_kernel(n_lat_tiles, split_x, with_h, *refs):
    refs = list(refs)
    x_ref = refs.pop(0)
    c_ref = refs.pop(0) if split_x else None
    y_ref, gpost_ref, gate_ref = refs.pop(0), refs.pop(0), refs.pop(0)
    if with_h:
        gpre_ref, sh_ref, sc_ref = refs.pop(0), refs.pop(0), refs.pop(0)
    x1_ref = refs.pop(0)
    is_ctx = pl.program_id(0) >= n_lat_tiles
    xv = jnp.where(is_ctx, c_ref[...], x_ref[...]) if split_x else x_ref[...]
    x1 = xv + _pick(is_ctx, gate_ref) * _rms(y_ref[...], gpost_ref[...])
    x1_ref[...] = x1
    if with_h:
        h_ref = refs.pop(0)
        n = _rms(x1, gpre_ref[...])
        h_ref[...] = (n * (1.0 + _pick(is_ctx, sc_ref)) + _pick(is_ctx, sh_ref)).astype(BF16)


def _resid(xs, y, g_post, mod_gate, k_gate, n_lat, n_rows, pre=None):
    D = y.shape[1]
    tm = ROW_TILE
    nl = n_lat // tm
    nt = n_rows // tm
    split_x = len(xs) == 2
    row = lambda i: (i, 0)
    vec = lambda i: (0, 0)
    if split_x:
        in_specs = [pl.BlockSpec((tm, D), lambda i: (jnp.minimum(i, nl - 1), 0)),
                    pl.BlockSpec((tm, D), lambda i: (jnp.maximum(i - nl, 0), 0))]
    else:
        in_specs = [pl.BlockSpec((tm, D), row)]
    in_specs += [pl.BlockSpec((tm, D), row), pl.BlockSpec((1, D), vec),
                 pl.BlockSpec((8, D), lambda i: (0, k_gate))]
    args = list(xs) + [y, g_post.reshape(1, D), mod_gate]
    out_specs = [pl.BlockSpec((tm, D), row)]
    out_shape = [jax.ShapeDtypeStruct((n_rows, D), F32)]
    if pre is not None:
        g_pre, mod_pre, k_shift = pre
        in_specs += [pl.BlockSpec((1, D), vec),
                     pl.BlockSpec((8, D), lambda i: (0, k_shift)),
                     pl.BlockSpec((8, D), lambda i: (0, k_shift + 1))]
        args += [g_pre.reshape(1, D), mod_pre, mod_pre]
        out_specs.append(pl.BlockSpec((tm, D), row))
        out_shape.append(jax.ShapeDtypeStruct((n_rows, D), BF16))
    return pl.pallas_call(
        functools.partial(_resid_kernel, nl, split_x, pre is not None),
        grid=(nt,), in_specs=in_specs, out_specs=out_specs, out_shape=out_shape,
        compiler_params=_params(("arbitrary",)),
    )(*args)


def _mm_kernel(n_in, *refs):
    w_ref, o_ref = refs[n_in], refs[n_in + 1]
    kc = w_ref.shape[0] // n_in
    acc = None
    for k in range(n_in):
        d = jnp.dot(refs[k][...].astype(BF16), w_ref[kc * k:kc * (k + 1), :].astype(BF16),
                    preferred_element_type=F32)
        acc = d if acc is None else acc + d
    o_ref[...] = acc.astype(o_ref.dtype)


def _matmul(a_list, w3, layer, tm, tn, out_dtype, single_buffer_rows=False, m_rows=None):
    M = a_list[0].shape[0] if m_rows is None else m_rows
    _, K, Nn = w3.shape
    kc = K // len(a_list)
    mode = dict(pipeline_mode=pl.Buffered(1)) if single_buffer_rows else {}
    return pl.pallas_call(
        functools.partial(_mm_kernel, len(a_list)),
        grid=(M // tm, Nn // tn),
        in_specs=[pl.BlockSpec((tm, kc), lambda i, j: (i, 0), **mode) for _ in a_list]
        + [pl.BlockSpec((None, K, tn), lambda i, j: (layer, 0, j))],
        out_specs=pl.BlockSpec((tm, tn), lambda i, j: (i, j)),
        out_shape=jax.ShapeDtypeStruct((M, Nn), out_dtype),
        compiler_params=_params(("arbitrary", "arbitrary"), 56),
    )(*a_list, w3)


def _rope_tables(n_lat, n_ctx, n_real, half):
    t = jnp.arange(n_lat, dtype=jnp.int32)
    rows = (t // GRID_W).astype(F32)[:, None]
    cols = (t % GRID_W).astype(F32)[:, None]
    lane = np.arange(LANES)
    grp = (lane // (2 * half)) % 2
    jj = lane % (2 * half)
    first = jj < half
    real = lane < n_real
    inv = ROPE_THETA ** (-jnp.asarray(jj % half, F32) / half)
    ang = jnp.where(jnp.asarray(grp == 0)[None, :], rows, cols) * inv[None, :]
    cos, sin = jnp.cos(ang), jnp.sin(ang)
    real_j, first_j = jnp.asarray(real)[None, :], jnp.asarray(first)[None, :]
    cos_t = jnp.where(real_j, cos, 1.0)
    sa_t = jnp.where(real_j & first_j, -sin, 0.0)
    sb_t = jnp.where(real_j & (~first_j), sin, 0.0)
    pad = lambda a, v: jnp.concatenate([a, jnp.full((n_ctx, LANES), v, F32)], axis=0)
    return pad(cos_t, 1.0), pad(sa_t, 0.0), pad(sb_t, 0.0)


def _rope(v, cos, sa, sb, half):
    return v * cos + pltpu.roll(v, LANES - half, 1) * sa + pltpu.roll(v, half, 1) * sb


def _mla_q_kernel(cq_ref, g_ref, w_ref, cos_ref, sa_ref, sb_ref, o_ref):
    n = _rms(cq_ref[...], g_ref[...]).astype(BF16)
    a = jnp.dot(n, w_ref[...], preferred_element_type=F32)
    cos, sa, sb = cos_ref[...], sa_ref[...], sb_ref[...]
    for h in range(MLA_HEADS):
        lo = 2 * LANES * h
        o_ref[:, lo:lo + LANES] = (a[:, lo:lo + LANES] * MLA_SCALE).astype(BF16)
        r = _rope(a[:, lo + LANES:lo + 2 * LANES], cos, sa, sb, MLA_ROPE // 4)
        o_ref[:, lo + LANES:lo + 2 * LANES] = (r * MLA_SCALE).astype(BF16)


def _prep_tile(T):
    return 3 * ROW_TILE if T % (3 * ROW_TILE) == 0 else ROW_TILE


def _mla_q(p, g_qa, w_uq_ext, tabs):
    T = p.shape[0]
    tm = _prep_tile(T)
    tab = pl.BlockSpec((tm, LANES), lambda i: (i, 0))
    return pl.pallas_call(
        _mla_q_kernel,
        grid=(T // tm,),
        in_specs=[pl.BlockSpec((tm, MLA_Q_RANK), lambda i: (i, P_CQ // MLA_Q_RANK)),
                  pl.BlockSpec((1, MLA_Q_RANK), lambda i: (0, 0)),
                  pl.BlockSpec((MLA_Q_RANK, 2 * LANES * MLA_HEADS), lambda i: (0, 0)),
                  tab, tab, tab],
        out_specs=pl.BlockSpec((tm, 2 * LANES * MLA_HEADS), lambda i: (i, 0)),
        out_shape=jax.ShapeDtypeStruct((T, 2 * LANES * MLA_HEADS), BF16),
        compiler_params=_params(("arbitrary",)),
    )(p, g_qa.reshape(1, -1), w_uq_ext, *tabs)


def _mla_kv_kernel(ckv_ref, kr_ref, g_ref, w_ref, cos_ref, sa_ref, sb_ref, k_ref, v_ref):
    n = _rms(ckv_ref[...], g_ref[...]).astype(BF16)
    a = jnp.dot(n, w_ref[...], preferred_element_type=F32)
    kr = _rope(kr_ref[...], cos_ref[...], sa_ref[...], sb_ref[...], MLA_ROPE // 4).astype(BF16)
    ones = jnp.ones(kr.shape, BF16)
    for h in range(MLA_HEADS):
        lo = 2 * LANES * h
        k_ref[:, lo:lo + LANES] = a[:, lo:lo + LANES].astype(BF16)
        k_ref[:, lo + LANES:lo + 2 * LANES] = kr
        v_ref[:, lo:lo + LANES] = a[:, lo + LANES:lo + 2 * LANES].astype(BF16)
        v_ref[:, lo + LANES:lo + 2 * LANES] = ones


def _mla_kv(p, g_kva, w_ukv, tabs):
    T = p.shape[0]
    tm = _prep_tile(T)
    tab = pl.BlockSpec((tm, LANES), lambda i: (i, 0))
    wide = 2 * LANES * MLA_HEADS
    return pl.pallas_call(
        _mla_kv_kernel,
        grid=(T // tm,),
        in_specs=[pl.BlockSpec((tm, MLA_KV_RANK), lambda i: (i, P_CKV // MLA_KV_RANK)),
                  pl.BlockSpec((tm, LANES), lambda i: (i, P_KR // LANES)),
                  pl.BlockSpec((1, MLA_KV_RANK), lambda i: (0, 0)),
                  pl.BlockSpec((MLA_KV_RANK, wide), lambda i: (0, 0)),
                  tab, tab, tab],
        out_specs=[pl.BlockSpec((tm, wide), lambda i: (i, 0)),
                   pl.BlockSpec((tm, wide), lambda i: (i, 0))],
        out_shape=[jax.ShapeDtypeStruct((T, wide), BF16),
                   jax.ShapeDtypeStruct((T, wide), BF16)],
        compiler_params=_params(("arbitrary",)),
    )(p, p, g_kva.reshape(1, -1), w_ukv, *tabs)


def _flash_kernel(q_ref, k_ref, v_ref, *rest):
    o_ref, m_sc, acc_sc = rest[-3:]
    kk = pl.program_id(2)

    @pl.when(kk == 0)
    def _():
        m_sc[...] = jnp.full(m_sc.shape, -1e30, F32)
        acc_sc[...] = jnp.zeros(acc_sc.shape, F32)

    rc = q_ref.shape[0] // FLASH_SUBTILES
    subs = [slice(r * rc, (r + 1) * rc) for r in range(FLASH_SUBTILES)]
    scores = [lax.dot_general(q_ref[rows, :], k_ref[...], _NT, preferred_element_type=F32) for rows in subs]
    for rows, s in zip(subs, scores):
        m_prev = m_sc[rows, :]
        m_new = jnp.maximum(m_prev, jnp.max(s, axis=1, keepdims=True))
        alpha = jnp.exp(m_prev - m_new)
        p = jnp.exp((s - m_new).astype(BF16))
        acc_sc[rows, :] = alpha * acc_sc[rows, :] + jnp.dot(p, v_ref[...], preferred_element_type=F32)
        m_sc[rows, :] = m_new

    @pl.when(kk == pl.num_programs(2) - 1)
    def _():
        acc = acc_sc[...]
        o_ref[...] = (acc[:, :LANES] / acc[:, LANES:LANES + 1]).astype(o_ref.dtype)


def _flash(q, k, v, out_prev, n_rows, q_blk0, n_qblk, k_blk0, n_kblk, tq, tk):
    dq = 2 * LANES
    in_specs = [pl.BlockSpec((tq, dq), lambda h, i, kk: (i + q_blk0, h)),
                pl.BlockSpec((tk, dq), lambda h, i, kk: (kk + k_blk0, h)),
                pl.BlockSpec((tk, dq), lambda h, i, kk: (kk + k_blk0, h))]
    args = [q, k, v]
    aliases = {}
    if out_prev is not None:
        in_specs.append(pl.BlockSpec(memory_space=pl.ANY))
        args.append(out_prev)
        aliases = {3: 0}
    return pl.pallas_call(
        _flash_kernel,
        grid=(MLA_HEADS, n_qblk, n_kblk),
        in_specs=in_specs,
        out_specs=pl.BlockSpec((tq, LANES), lambda h, i, kk: (i + q_blk0, h)),
        out_shape=jax.ShapeDtypeStruct((n_rows, MLA_HEADS * LANES), BF16),
        scratch_shapes=[pltpu.VMEM((tq, 1), F32), pltpu.VMEM((tq, dq), F32)],
        input_output_aliases=aliases,
        compiler_params=_params(("arbitrary", "arbitrary", "arbitrary"), 48),
    )(*args)


def _gla_decays(q, k, glr, wg, bg, tri, last_row):
    z = jnp.dot(glr.astype(BF16), wg, preferred_element_type=F32) + bg
    la = (jnp.minimum(z, 0.0) - jnp.log(1.0 + jnp.exp(-jnp.abs(z)))) * (1.0 / GLA_GATE_TAU)
    ones = jnp.where(tri, 1.0, 0.0).astype(BF16)
    la_hi = la.astype(BF16)
    la_lo = (la - la_hi.astype(F32)).astype(BF16)
    b = (jnp.dot(ones, la_hi, preferred_element_type=F32)
         + jnp.dot(ones, la_lo, preferred_element_type=F32))
    b_tot = b[last_row:last_row + 1, :]
    qt = q * (GLA_DK ** -0.5) * jnp.exp(b)
    kt = k * jnp.exp(-b)
    kh = k * jnp.exp(b_tot - b)
    dec = jnp.exp(b_tot)
    return qt, kt, kh, dec


def _gla_head(h, decays, v, tri, st_ref, o_ref):
    qt, kt, kh, dec = decays
    ks = slice(GLA_DK * h, GLA_DK * (h + 1))
    vs = slice(GLA_DV * h, GLA_DV * (h + 1))
    qh = qt[:, ks].astype(BF16)
    vh = v[:, vs]
    st = st_ref[h]
    inter = lax.dot_general(qh, st.astype(BF16), _NT, preferred_element_type=F32)
    sc = lax.dot_general(qh, kt[:, ks].astype(BF16), _NT, preferred_element_type=F32)
    sc = jnp.where(tri, sc, 0.0)
    intra = jnp.dot(sc.astype(BF16), vh.astype(BF16), preferred_element_type=F32)
    o_ref[:, vs] = inter + intra
    st_ref[h] = st * dec[:, ks] + jnp.dot(vh.T.astype(BF16), kh[:, ks].astype(BF16),
                                           preferred_element_type=F32)


def _gla_kernel(qf, kf, vf, gf, qb, kb, vb, gb, wf, bf, wb, bb, of_ref, ob_ref, stf, stb):
    @pl.when(pl.program_id(0) == 0)
    def _():
        stf[...] = jnp.zeros(stf.shape, F32)
        stb[...] = jnp.zeros(stb.shape, F32)

    cs = qf.shape[0]
    t_i = lax.broadcasted_iota(jnp.int32, (cs, cs), 0)
    s_i = lax.broadcasted_iota(jnp.int32, (cs, cs), 1)
    tri_f, tri_b = s_i <= t_i, s_i >= t_i
    dec_f = _gla_decays(qf[...], kf[...], gf[...], wf[...], bf[...], tri_f, cs - 1)
    dec_b = _gla_decays(qb[...], kb[...], gb[...], wb[...], bb[...], tri_b, 0)
    v_f, v_b = vf[...], vb[...]
    for h in range(GLA_HEADS):
        _gla_head(h, dec_f, v_f, tri_f, stf, of_ref)
        _gla_head(h, dec_b, v_b, tri_b, stb, ob_ref)


def _gla(p, n_lat, wf_ext, bf, wb_ext, bb):
    T = p.shape[0]
    cs = GLA_CHUNK
    nch = T // cs
    nlc = n_lat // cs
    fwd = lambda c: (c + nlc) % nch
    bwd = lambda c: nch - 1 - c
    dk, dv = GLA_HEADS * GLA_DK, GLA_HEADS * GLA_DV

    def specs(idx):
        return [pl.BlockSpec((cs, dk), lambda c: (idx(c), P_GQ // dk)),
                pl.BlockSpec((cs, dk), lambda c: (idx(c), P_GK // dk)),
                pl.BlockSpec((cs, dv), lambda c: (idx(c), P_GV // dv)),
                pl.BlockSpec((cs, LANES), lambda c: (idx(c), P_GLR // LANES))]

    wspec = pl.BlockSpec((LANES, dk), lambda c: (0, 0))
    bspec = pl.BlockSpec((1, dk), lambda c: (0, 0))
    return pl.pallas_call(
        _gla_kernel,
        grid=(nch,),
        in_specs=specs(fwd) + specs(bwd) + [wspec, bspec, wspec, bspec],
        out_specs=[pl.BlockSpec((cs, dv), lambda c: (fwd(c), 0)),
                   pl.BlockSpec((cs, dv), lambda c: (bwd(c), 0))],
        out_shape=[jax.ShapeDtypeStruct((T, dv), F32), jax.ShapeDtypeStruct((T, dv), F32)],
        scratch_shapes=[pltpu.VMEM((GLA_HEADS, GLA_DV, GLA_DK), F32),
                        pltpu.VMEM((GLA_HEADS, GLA_DV, GLA_DK), F32)],
        compiler_params=_params(("arbitrary",)),
    )(p, p, p, p, p, p, p, p, wf_ext, bf.reshape(1, dk), wb_ext, bb.reshape(1, dk))


def _gla_out_kernel(of_ref, ob_ref, gr_ref, g_ref, o_ref):
    o = of_ref[...] + ob_ref[...]
    gr = gr_ref[...]
    for h in range(GLA_HEADS):
        vs = slice(GLA_DV * h, GLA_DV * (h + 1))
        o_ref[:, vs] = (_rms(o[:, vs], g_ref[...]) * _silu(gr[:, vs])).astype(BF16)


def _gla_out(o_f, o_b, p, g_gla):
    T, dv = o_f.shape
    tm = _prep_tile(T)
    row = pl.BlockSpec((tm, dv), lambda i: (i, 0))
    return pl.pallas_call(
        _gla_out_kernel,
        grid=(T // tm,),
        in_specs=[row, row, pl.BlockSpec((tm, dv), lambda i: (i, P_GR // dv)),
                  pl.BlockSpec((1, GLA_DV), lambda i: (0, 0))],
        out_specs=row,
        out_shape=jax.ShapeDtypeStruct((T, dv), BF16),
        compiler_params=_params(("arbitrary",)),
    )(o_f, o_b, p, g_gla.reshape(1, GLA_DV))


def _dft_cos_sin(n):
    k = np.arange(n)
    ang = 2.0 * np.pi * ((k[:, None] * k[None, :]) % n) / n
    return np.cos(ang), np.sin(ang)


def _channel_dft(xb, wd):
    zr, zi = [], []
    for g in range(FNET_GROUPS):
        z = jnp.dot(xb[:, LANES * g:LANES * (g + 1)], wd, preferred_element_type=F32)
        zr.append(z[:, :LANES])
        zi.append(z[:, LANES:])
    return jnp.concatenate(zr, axis=1), jnp.concatenate(zi, axis=1)


def _fft_a_kernel(n1, x_ref, wd_ref, m_ref, bre_ref, bim_ref):
    for n2 in range(FFT_N2):
        xs = x_ref[pl.ds(n2, n1, stride=FFT_N2), :].astype(BF16)
        z = jnp.dot(xs, wd_ref[...], preferred_element_type=F32)
        zs = jnp.concatenate([z[:, :LANES], z[:, LANES:]], axis=0).astype(BF16)
        b = jnp.dot(m_ref[n2], zs, preferred_element_type=F32)
        bre_ref[n2 * n1:(n2 + 1) * n1, :] = b[:n1]
        bim_ref[n2 * n1:(n2 + 1) * n1, :] = b[n1:]


def _fft_b_kernel(n1, scale, c_ref, s_ref, bre_ref, bim_ref, o_ref):
    grp = 8
    for t in range(n1 // grp):
        rows = [pl.ds(grp * t + r, FFT_N2, stride=n1) for r in range(grp)]
        br = jnp.concatenate([bre_ref[rw, :] for rw in rows], axis=1).astype(BF16)
        bi = jnp.concatenate([bim_ref[rw, :] for rw in rows], axis=1).astype(BF16)
        o = (jnp.dot(c_ref[...], br, preferred_element_type=F32)
             + jnp.dot(s_ref[...], bi, preferred_element_type=F32)) * scale
        for r in range(grp):
            o_ref[rows[r], :] = o[:, LANES * r:LANES * (r + 1)]


def _fft_ctx_kernel(scale, x_ref, wd_ref, cs_ref, prev_ref, o_ref):
    del prev_ref
    zr, zi = _channel_dft(x_ref[...].astype(BF16), wd_ref[...])
    z = jnp.concatenate([zr, zi], axis=0).astype(BF16)
    o_ref[...] = jnp.dot(cs_ref[...], z, preferred_element_type=F32) * scale


def _fourier(p, n_lat, n_ctx, with_ctx):
    T = p.shape[0]
    gw = FNET_GROUPS * LANES
    n1 = n_lat // FFT_N2
    cd, sd = _dft_cos_sin(LANES)
    as_bf16 = lambda a: jnp.asarray(a, F32).astype(BF16)
    wd = as_bf16(np.concatenate([cd, -sd], axis=1))
    k1 = jnp.arange(n1, dtype=jnp.int32)[None, :, None]
    pos = (FFT_N2 * jnp.arange(n1, dtype=jnp.int32)[None, None, :]
           + jnp.arange(FFT_N2, dtype=jnp.int32)[:, None, None])
    ang = (2.0 * np.pi / n_lat) * ((k1 * pos) % n_lat).astype(F32)
    gc, gs = jnp.cos(ang), jnp.sin(ang)
    m = jnp.concatenate([jnp.concatenate([gc, gs], axis=2),
                         jnp.concatenate([-gs, gc], axis=2)], axis=1).astype(BF16)
    col = lambda g: (0, g)
    bre, bim = pl.pallas_call(
        functools.partial(_fft_a_kernel, n1),
        grid=(FNET_GROUPS,),
        in_specs=[pl.BlockSpec((n_lat, LANES), lambda g: (0, P_FU // LANES + g)),
                  pl.BlockSpec((LANES, 2 * LANES), lambda g: (0, 0)),
                  pl.BlockSpec((FFT_N2, 2 * n1, 2 * n1), lambda g: (0, 0, 0))],
        out_specs=[pl.BlockSpec((n_lat, LANES), col)] * 2,
        out_shape=[jax.ShapeDtypeStruct((n_lat, gw), F32)] * 2,
        compiler_params=_params(("arbitrary",), 48),
    )(p, wd, m)

    c2, s2 = _dft_cos_sin(FFT_N2)
    mspec = pl.BlockSpec((FFT_N2, FFT_N2), lambda g: (0, 0))
    y = pl.pallas_call(
        functools.partial(_fft_b_kernel, n1, float((n_lat * LANES) ** -0.5)),
        grid=(FNET_GROUPS,),
        in_specs=[mspec, mspec, pl.BlockSpec((n_lat, LANES), col), pl.BlockSpec((n_lat, LANES), col)],
        out_specs=pl.BlockSpec((n_lat, LANES), col),
        out_shape=jax.ShapeDtypeStruct((T, gw), F32),
        compiler_params=_params(("arbitrary",)),
    )(as_bf16(c2), as_bf16(s2), bre, bim)

    if not with_ctx:
        return y
    cc, sc = _dft_cos_sin(n_ctx)
    return pl.pallas_call(
        functools.partial(_fft_ctx_kernel, float((n_ctx * LANES) ** -0.5)),
        grid=(1,),
        in_specs=[pl.BlockSpec((n_ctx, gw), lambda j: (n_lat // n_ctx, P_FU // gw)),
                  pl.BlockSpec((LANES, 2 * LANES), lambda j: (0, 0)),
                  pl.BlockSpec((n_ctx, 2 * n_ctx), lambda j: (0, 0)),
                  pl.BlockSpec(memory_space=pl.ANY)],
        out_specs=pl.BlockSpec((n_ctx, gw), lambda j: (n_lat // n_ctx, 0)),
        out_shape=jax.ShapeDtypeStruct((T, gw), F32),
        input_output_aliases={3: 0},
        compiler_params=_params(("arbitrary",)),
    )(p, wd, as_bf16(np.concatenate([cc, sc], axis=1)), y)


def _swa_prep_kernel(q_ref, k_ref, v_ref, cos_ref, sa_ref, sb_ref, qo_ref, ko_ref, vo_ref):
    cos, sa, sb = cos_ref[...], sa_ref[...], sb_ref[...]
    for h in range(SWA_HEADS):
        sl = slice(LANES * h, LANES * (h + 1))
        qo_ref[:, sl] = (_rope(q_ref[:, sl], cos, sa, sb, 32) * SWA_SCALE).astype(BF16)
    for h in range(SWA_KV_HEADS):
        sl = slice(LANES * h, LANES * (h + 1))
        ko_ref[:, sl] = _rope(k_ref[:, sl], cos, sa, sb, 32).astype(BF16)
    vo_ref[...] = v_ref[...].astype(BF16)


def _swa_prep(p, tabs):
    T = p.shape[0]
    tm = _prep_tile(T)
    qw, kw = SWA_HEADS * LANES, SWA_KV_HEADS * LANES
    tab = pl.BlockSpec((tm, LANES), lambda i: (i, 0))
    return pl.pallas_call(
        _swa_prep_kernel,
        grid=(T // tm,),
        in_specs=[pl.BlockSpec((tm, qw), lambda i: (i, P_SQ // qw)),
                  pl.BlockSpec((tm, kw), lambda i: (i, P_SK // kw)),
                  pl.BlockSpec((tm, kw), lambda i: (i, P_SV // kw)),
                  tab, tab, tab],
        out_specs=[pl.BlockSpec((tm, qw), lambda i: (i, 0)),
                   pl.BlockSpec((tm, kw), lambda i: (i, 0)),
                   pl.BlockSpec((tm, kw), lambda i: (i, 0))],
        out_shape=[jax.ShapeDtypeStruct((T, qw), BF16),
                   jax.ShapeDtypeStruct((T, kw), BF16),
                   jax.ShapeDtypeStruct((T, kw), BF16)],
        compiler_params=_params(("arbitrary",)),
    )(p, p, p, *tabs)


def _sink_attend(q, kcat, vcat, mask, sink):
    s = lax.dot_general(q, kcat, _NT, preferred_element_type=F32)
    if mask is not None:
        s = jnp.where(mask, s, -1e30)
    m = jnp.maximum(jnp.max(s, axis=1, keepdims=True), sink)
    pr = jnp.exp((s - m).astype(BF16))
    den = jnp.sum(pr.astype(F32), axis=1, keepdims=True) + jnp.exp(sink - m)
    return jnp.dot(pr, vcat, preferred_element_type=F32) / den


def _swa_lat_kernel(q_ref, kc_ref, kp_ref, k0_ref, kn_ref, vc_ref, vp_ref, v0_ref, vn_ref,
                    sink_ref, o_ref):
    kvh, nb = pl.program_id(0), pl.program_id(1)
    n_ctx = kc_ref.shape[0]
    qb = q_ref.shape[0]
    win = SWA_BLOCK
    kcat = jnp.concatenate([kc_ref[...], kp_ref[...], k0_ref[...], kn_ref[...]], axis=0)
    vcat = jnp.concatenate([vc_ref[...], vp_ref[...], v0_ref[...], vn_ref[...]], axis=0)
    n_keys = n_ctx + qb + 2 * win
    i = lax.broadcasted_iota(jnp.int32, (qb, n_keys), 0)
    j = lax.broadcasted_iota(jnp.int32, (qb, n_keys), 1)
    off = j - (n_ctx + win)
    in_band = jnp.abs(i - off) <= win
    exists = ((off >= 0) | (nb > 0)) & ((off < qb) | (nb < pl.num_programs(1) - 1))
    mask = (j < n_ctx) | (in_band & exists)
    for g in range(SWA_GROUP):
        sl = slice(LANES * g, LANES * (g + 1))
        sink = sink_ref[pl.ds(kvh * SWA_GROUP + g, 1), 0:1]
        o_ref[:, sl] = _sink_attend(q_ref[:, sl], kcat, vcat, mask, sink).astype(BF16)


def _swa_ctx_kernel(q_ref, k_ref, v_ref, sink_ref, prev_ref, o_ref):
    del prev_ref
    kvh = pl.program_id(0)
    for g in range(SWA_GROUP):
        sl = slice(LANES * g, LANES * (g + 1))
        sink = sink_ref[pl.ds(kvh * SWA_GROUP + g, 1), 0:1]
        o_ref[:, sl] = _sink_attend(q_ref[:, sl], k_ref[...], v_ref[...], None, sink).astype(BF16)


def _swa(qs, ks, vs, sink, n_lat, n_ctx, with_ctx):
    T = qs.shape[0]
    win = SWA_BLOCK
    qb = SWA_QBLOCK
    nb = n_lat // qb
    per = qb // win
    gq = SWA_GROUP * LANES
    sink2d = jnp.broadcast_to(sink.astype(F32)[:, None], (SWA_HEADS, LANES))
    cblk = n_lat // n_ctx
    ctx_spec = pl.BlockSpec((n_ctx, LANES), lambda h, b: (cblk, h))
    prev_spec = pl.BlockSpec((win, LANES), lambda h, b: (jnp.maximum(b * per - 1, 0), h))
    cur_spec = pl.BlockSpec((qb, LANES), lambda h, b: (b, h))
    next_spec = pl.BlockSpec((win, LANES), lambda h, b: (jnp.minimum((b + 1) * per, nb * per - 1), h))
    sink_spec = pl.BlockSpec((SWA_HEADS, LANES), lambda h, b: (0, 0))
    o_lat = pl.pallas_call(
        _swa_lat_kernel,
        grid=(SWA_KV_HEADS, nb),
        in_specs=[pl.BlockSpec((qb, gq), lambda h, b: (b, h)),
                  ctx_spec, prev_spec, cur_spec, next_spec,
                  ctx_spec, prev_spec, cur_spec, next_spec, sink_spec],
        out_specs=pl.BlockSpec((qb, gq), lambda h, b: (b, h)),
        out_shape=jax.ShapeDtypeStruct((T, SWA_HEADS * LANES), BF16),
        compiler_params=_params(("arbitrary", "arbitrary")),
    )(qs, ks, ks, ks, ks, vs, vs, vs, vs, sink2d)
    if not with_ctx:
        return o_lat
    return pl.pallas_call(
        _swa_ctx_kernel,
        grid=(SWA_KV_HEADS,),
        in_specs=[pl.BlockSpec((n_ctx, gq), lambda h: (cblk, h)),
                  pl.BlockSpec((n_ctx, LANES), lambda h: (cblk, h)),
                  pl.BlockSpec((n_ctx, LANES), lambda h: (cblk, h)),
                  pl.BlockSpec((SWA_HEADS, LANES), lambda h: (0, 0)),
                  pl.BlockSpec(memory_space=pl.ANY)],
        out_specs=pl.BlockSpec((n_ctx, gq), lambda h: (cblk, h)),
        out_shape=jax.ShapeDtypeStruct((T, SWA_HEADS * LANES), BF16),
        input_output_aliases={4: 0},
        compiler_params=_params(("arbitrary",)),
    )(qs, ks, vs, sink2d, o_lat)


GATE_TILE = 128
GATE_HALO = 16


def _gate_kernel(tm, n_lat, n_tot, ug_ref, ua_ref, gp_ref, gn_ref, ap_ref, an_ref, cwg_ref, cwa_ref,
                 cbg_ref, cba_ref, o_ref, ubuf):
    i = pl.program_id(0)
    first = (i == 0) | (i * tm == n_lat)
    last = ((i + 1) * tm == n_lat) | ((i + 1) * tm == n_tot)
    rows = tm + 2 * GATE_HALO
    out_r = lax.broadcasted_iota(jnp.int32, (2 * tm, rows), 0)
    src_r = lax.broadcasted_iota(jnp.int32, (2 * tm, rows), 1)
    want = jnp.where(out_r < tm, out_r + (GATE_HALO - 1), out_r - tm + (GATE_HALO + 1))
    pick = jnp.where(src_r == want, 1.0, 0.0).astype(BF16)

    def conv(u_ref, p_ref, n_ref, cw_ref, cb_ref):
        ubuf[0:GATE_HALO, :] = jnp.where(first, jnp.zeros_like(p_ref[...]), p_ref[...])
        ubuf[GATE_HALO:GATE_HALO + tm, :] = u_ref[...]
        ubuf[GATE_HALO + tm:, :] = jnp.where(last, jnp.zeros_like(n_ref[...]), n_ref[...])
        shifted = jnp.dot(pick, ubuf[...], preferred_element_type=F32)
        um, up = shifted[:tm], shifted[tm:]
        return (um * cw_ref[0:1, :] + u_ref[...].astype(F32) * cw_ref[1:2, :] + up * cw_ref[2:3, :]
                + cb_ref[...])

    g = conv(ug_ref, gp_ref, gn_ref, cwg_ref, cbg_ref)
    a = conv(ua_ref, ap_ref, an_ref, cwa_ref, cba_ref)
    o_ref[...] = (_silu(g) * a).astype(BF16)


def _gate(u, n_lat, conv_w, conv_b):
    T = u.shape[0]
    dff = u.shape[1] // 2
    tm = GATE_TILE
    hb = tm // GATE_HALO
    last_hb = T // GATE_HALO - 1
    cb2 = conv_b.reshape(1, 2 * dff)
    prev = lambda i: jnp.maximum(i * hb - 1, 0)
    nxt = lambda i: jnp.minimum((i + 1) * hb, last_hb)
    return pl.pallas_call(
        functools.partial(_gate_kernel, tm, n_lat, T),
        grid=(T // tm,),
        in_specs=[pl.BlockSpec((tm, dff), lambda i: (i, 0)),
                  pl.BlockSpec((tm, dff), lambda i: (i, 1)),
                  pl.BlockSpec((GATE_HALO, dff), lambda i: (prev(i), 0)),
                  pl.BlockSpec((GATE_HALO, dff), lambda i: (nxt(i), 0)),
                  pl.BlockSpec((GATE_HALO, dff), lambda i: (prev(i), 1)),
                  pl.BlockSpec((GATE_HALO, dff), lambda i: (nxt(i), 1)),
                  pl.BlockSpec((3, dff), lambda i: (0, 0)),
                  pl.BlockSpec((3, dff), lambda i: (0, 1)),
                  pl.BlockSpec((1, dff), lambda i: (0, 0)),
                  pl.BlockSpec((1, dff), lambda i: (0, 1))],
        out_specs=pl.BlockSpec((tm, dff), lambda i: (i, 0)),
        out_shape=jax.ShapeDtypeStruct((T, dff), BF16),
        scratch_shapes=[pltpu.VMEM((tm + 2 * GATE_HALO, dff), BF16)],
        compiler_params=_params(("arbitrary",), 48),
    )(u, u, u, u, u, u, conv_w, conv_w, cb2, cb2)


def _w_in_ext(w_in):
    cq, ckv, kr, gq, gk, gv, gr, glr, fu, sq, sk, sv = jnp.split(
        w_in, np.cumsum([768, 512, 64, 512, 512, 1024, 1024, 32, 1024, 1024, 256]).tolist(), axis=2)
    pad = lambda w: jnp.pad(w, ((0, 0), (0, 0), (0, LANES - w.shape[2])))
    parts = [fu, gv, gr, cq, sk, sq, ckv, gq, gk, sv, pad(kr), pad(glr)]
    return jnp.concatenate(parts, axis=2).astype(BF16)


def _tile_for(total, parts, mult=16):
    t = total // parts
    assert t * parts == total and t % mult == 0, (total, parts)
    return t


def _key_tile(total):
    best = LANES
    for t in range(LANES, 3072 + 1, LANES):
        if total % t == 0:
            best = t
    return best


def kernel(x, c, ctx, c_ctx, w_ada, b_ada, g_pre_mix, g_post_mix, g_pre_ffn, g_post_ffn, w_in, g_qa, w_uq,
           g_kva, w_ukv, w_gate_f, b_gate_f, w_gate_b, b_gate_b, g_gla, swa_sink, w_out, w_up, conv_w, conv_b,
           w_down):
    B, N, D = x.shape
    C = ctx.shape[1]
    T = N + C
    L = w_ada.shape[0]
    dff = w_down.shape[1]
    assert B == 1 and D == 4096 and N % (GRID_W * 16) == 0 and C % ROW_TILE == 0 and N % C == 0
    x2d, ctx2d = x[0], ctx[0]

    cc = jnp.zeros((8, D), F32).at[0].set(c[0]).at[1].set(c_ctx)
    mod = _ada(cc, w_ada, b_ada)
    mla_tabs = _rope_tables(N, C, MLA_ROPE, MLA_ROPE // 4)
    swa_tabs = _rope_tables(N, C, LANES, 32)


    w_in_x = _w_in_ext(w_in)
    w_down_b = w_down.astype(BF16)
    tq, tk = _tile_for(N, 4), _key_tile(T)

    h = _rms_mod(x2d, ctx2d, g_pre_mix[0], mod[0], 0)
    xs = (x2d, ctx2d)
    for l in range(L):
        last = l == L - 1
        p = _matmul([h], w_in_x, l, _tile_for(T, 8), 1024, F32)

        w_uq_ext = jnp.pad(w_uq[l].reshape(MLA_Q_RANK, MLA_HEADS, MLA_NOPE + MLA_ROPE),
                           ((0, 0), (0, 0), (0, 2 * LANES - MLA_NOPE - MLA_ROPE))
                           ).reshape(MLA_Q_RANK, 2 * LANES * MLA_HEADS).astype(BF16)
        q_a = _mla_q(p, g_qa[l], w_uq_ext, mla_tabs)
        k_a, v_a = _mla_kv(p, g_kva[l], w_ukv[l].astype(BF16), mla_tabs)
        o_a = _flash(q_a, k_a, v_a, None, T, 0, N // tq, 0, T // tk, tq, tk)
        if not last:
            o_a = _flash(q_a, k_a, v_a, o_a, T, N // C, 1, N // C, 1, C, C)

        wf_ext = jnp.zeros((LANES, GLA_HEADS * GLA_DK), F32).at[:GLA_GATE_RANK].set(w_gate_f[l]).astype(BF16)
        wb_ext = jnp.zeros((LANES, GLA_HEADS * GLA_DK), F32).at[GLA_GATE_RANK:2 * GLA_GATE_RANK].set(
            w_gate_b[l]).astype(BF16)
        o_f, o_bk = _gla(p, N, wf_ext, b_gate_f[l], wb_ext, b_gate_b[l])
        o_b = _gla_out(o_f, o_bk, p, g_gla[l])

        o_c = _fourier(p, N, C, not last)

        qs, ks, vs = _swa_prep(p, swa_tabs)
        o_d = _swa(qs, ks, vs, swa_sink[l], N, C, not last)

        rows = N if last else T
        y = _matmul([o_a, o_b, o_c, o_d], w_out, l, _tile_for(rows, 4), 512, F32, True, m_rows=rows)
        x1, h2 = _resid(xs, y, g_post_mix[l], mod[l], 2, N, rows, pre=(g_pre_ffn[l], mod[l], 3))
        u = _matmul([h2], w_up, l, _tile_for(rows, 4), 512, BF16, True)
        act = _gate(u, N, conv_w[l], conv_b[l])
        f = _matmul([act], w_down_b, l, _tile_for(rows, 16), 512, F32)
        if last:
            (x_out,) = _resid((x1,), f, g_post_ffn[l], mod[l], 5, N, N)
            return x_out.reshape(1, N, D)
        x2, h = _resid((x1,), f, g_post_ffn[l], mod[l], 5, N, T, pre=(g_pre_mix[l + 1], mod[l + 1], 0))
        xs = (x2,)
```

```python
import functools

import numpy as np
import jax
import jax.numpy as jnp
from jax import lax
from jax.experimental import pallas as pl
from jax.experimental.pallas import tpu as pltpu

F32 = jnp.float32
BF16 = jnp.bfloat16

GRID_W = 64
EPS = 1e-6
ROPE_THETA = 10000.0
MLA_HEADS = 8
MLA_Q_RANK = 768
MLA_KV_RANK = 512
MLA_NOPE = 128
MLA_ROPE = 64
MLA_SCALE = (MLA_NOPE + MLA_ROPE) ** -0.5
GLA_HEADS = 4
GLA_DK = 128
GLA_DV = 256
GLA_GATE_RANK = 16
GLA_GATE_TAU = 16.0
GLA_CHUNK = 128
FNET_GROUPS = 8
FFT_N2 = 64
SWA_HEADS = 8
SWA_KV_HEADS = 2
SWA_GROUP = 4
SWA_BLOCK = 128
SWA_QBLOCK = 512
FLASH_SUBTILES = 4
SWA_SCALE = 128 ** -0.5
LANES = 128
ROW_TILE = 256

P_FU, P_GV, P_GR, P_CQ, P_SK, P_SQ = 0, 1024, 2048, 3072, 3840, 4096
P_CKV, P_GQ, P_GK, P_SV, P_KR, P_GLR = 5120, 5632, 6144, 6656, 6912, 7040
P_WIDTH = 7168

_NT = (((1,), (1,)), ((), ()))
_MIB = 1024 * 1024


def _params(sem, vmem_mib=40):
    return pltpu.CompilerParams(dimension_semantics=sem, vmem_limit_bytes=vmem_mib * _MIB)


def _silu(v):
    return v / (1.0 + jnp.exp(-v))


def _rms(v, g):
    return v * lax.rsqrt(jnp.mean(v * v, axis=-1, keepdims=True) + EPS) * g


def _pick(is_ctx, ref):
    return jnp.where(is_ctx, ref[1:2, :], ref[0:1, :])


def _ada_kernel(c_ref, w_ref, b_ref, o_ref):
    a = _silu(c_ref[...]).astype(BF16)
    o_ref[...] = jnp.dot(a, w_ref[...].astype(BF16), preferred_element_type=F32) + b_ref[...]


def _ada(cc, w_ada, b_ada):
    L, D, N6 = w_ada.shape
    tn = 512
    return pl.pallas_call(
        _ada_kernel,
        grid=(L, N6 // tn),
        in_specs=[pl.BlockSpec((8, D), lambda l, j: (0, 0)),
                  pl.BlockSpec((None, D, tn), lambda l, j: (l, 0, j)),
                  pl.BlockSpec((None, 1, tn), lambda l, j: (l, 0, j))],
        out_specs=pl.BlockSpec((None, 8, tn), lambda l, j: (l, 0, j)),
        out_shape=jax.ShapeDtypeStruct((L, 8, N6), F32),
        compiler_params=_params(("arbitrary", "arbitrary")),
    )(cc, w_ada, b_ada.reshape(L, 1, N6))


def _rms_mod_kernel(n_lat_tiles, x_ref, c_ref, g_ref, sh_ref, sc_ref, h_ref):
    is_ctx = pl.program_id(0) >= n_lat_tiles
    xv = jnp.where(is_ctx, c_ref[...], x_ref[...])
    y = _rms(xv, g_ref[...])
    h_ref[...] = (y * (1.0 + _pick(is_ctx, sc_ref)) + _pick(is_ctx, sh_ref)).astype(BF16)


def _rms_mod(x2d, ctx2d, g, mod, k_shift):
    N, D = x2d.shape
    C = ctx2d.shape[0]
    tm = ROW_TILE
    nl, nc = N // tm, C // tm
    return pl.pallas_call(
        functools.partial(_rms_mod_kernel, nl),
        grid=(nl + nc,),
        in_specs=[pl.BlockSpec((tm, D), lambda i: (jnp.minimum(i, nl - 1), 0)),
                  pl.BlockSpec((tm, D), lambda i: (jnp.maximum(i - nl, 0), 0)),
                  pl.BlockSpec((1, D), lambda i: (0, 0)),
                  pl.BlockSpec((8, D), lambda i: (0, k_shift)),
                  pl.BlockSpec((8, D), lambda i: (0, k_shift + 1))],
        out_specs=pl.BlockSpec((tm, D), lambda i: (i, 0)),
        out_shape=jax.ShapeDtypeStruct((N + C, D), BF16),
        compiler_params=_params(("arbitrary",)),
    )(x2d, ctx2d, g.reshape(1, D), mod, mod)


def _resid_kernel(n_lat_tiles, split_x, with_h, *refs):
    refs = list(refs)
    x_ref = refs.pop(0)
    c_ref = refs.pop(0) if split_x else None
    y_ref, gpost_ref, gate_ref = refs.pop(0), refs.pop(0), refs.pop(0)
    if with_h:
        gpre_ref, sh_ref, sc_ref = refs.pop(0), refs.pop(0), refs.pop(0)
    x1_ref = refs.pop(0)
    is_ctx = pl.program_id(0) >= n_lat_tiles
    xv = jnp.where(is_ctx, c_ref[...], x_ref[...]) if split_x else x_ref[...]
    x1 = xv + _pick(is_ctx, gate_ref) * _rms(y_ref[...], gpost_ref[...])
    x1_ref[...] = x1
    if with_h:
        h_ref = refs.pop(0)
        n = _rms(x1, gpre_ref[...])
        h_ref[...] = (n * (1.0 + _pick(is_ctx, sc_ref)) + _pick(is_ctx, sh_ref)).astype(BF16)


def _resid(xs, y, g_post, mod_gate, k_gate, n_lat, n_rows, pre=None):
    D = y.shape[1]
    tm = ROW_TILE
    nl = n_lat // tm
    nt = n_rows // tm
    split_x = len(xs) == 2
    row = lambda i: (i, 0)
    vec = lambda i: (0, 0)
    if split_x:
        in_specs = [pl.BlockSpec((tm, D), lambda i: (jnp.minimum(i, nl - 1), 0)),
                    pl.BlockSpec((tm, D), lambda i: (jnp.maximum(i - nl, 0), 0))]
    else:
        in_specs = [pl.BlockSpec((tm, D), row)]
    in_specs += [pl.BlockSpec((tm, D), row), pl.BlockSpec((1, D), vec),
                 pl.BlockSpec((8, D), lambda i: (0, k_gate))]
    args = list(xs) + [y, g_post.reshape(1, D), mod_gate]
    out_specs = [pl.BlockSpec((tm, D), row)]
    out_shape = [jax.ShapeDtypeStruct((n_rows, D), F32)]
    if pre is not None:
        g_pre, mod_pre, k_shift = pre
        in_specs += [pl.BlockSpec((1, D), vec),
                     pl.BlockSpec((8, D), lambda i: (0, k_shift)),
                     pl.BlockSpec((8, D), lambda i: (0, k_shift + 1))]
        args += [g_pre.reshape(1, D), mod_pre, mod_pre]
        out_specs.append(pl.BlockSpec((tm, D), row))
        out_shape.append(jax.ShapeDtypeStruct((n_rows, D), BF16))
    return pl.pallas_call(
        functools.partial(_resid_kernel, nl, split_x, pre is not None),
        grid=(nt,), in_specs=in_specs, out_specs=out_specs, out_shape=out_shape,
        compiler_params=_params(("arbitrary",)),
    )(*args)


def _mm_kernel(n_in, *refs):
    w_ref, o_ref = refs[n_in], refs[n_in + 1]
    kc = w_ref.shape[0] // n_in
    acc = None
    for k in range(n_in):
        d = jnp.dot(refs[k][...].astype(BF16), w_ref[kc * k:kc * (k + 1), :].astype(BF16),
                    preferred_element_type=F32)
        acc = d if acc is None else acc + d
    o_ref[...] = acc.astype(o_ref.dtype)


def _matmul(a_list, w3, layer, tm, tn, out_dtype, single_buffer_rows=False, m_rows=None):
    M = a_list[0].shape[0] if m_rows is None else m_rows
    _, K, Nn = w3.shape
    kc = K // len(a_list)
    mode = dict(pipeline_mode=pl.Buffered(1)) if single_buffer_rows else {}
    return pl.pallas_call(
        functools.partial(_mm_kernel, len(a_list)),
        grid=(M // tm, Nn // tn),
        in_specs=[pl.BlockSpec((tm, kc), lambda i, j: (i, 0), **mode) for _ in a_list]
        + [pl.BlockSpec((None, K, tn), lambda i, j: (layer, 0, j))],
        out_specs=pl.BlockSpec((tm, tn), lambda i, j: (i, j)),
        out_shape=jax.ShapeDtypeStruct((M, Nn), out_dtype),
        compiler_params=_params(("arbitrary", "arbitrary"), 56),
    )(*a_list, w3)


def _rope_tables(n_lat, n_ctx, n_real, half):
    t = jnp.arange(n_lat, dtype=jnp.int32)
    rows = (t // GRID_W).astype(F32)[:, None]
    cols = (t % GRID_W).astype(F32)[:, None]
    lane = np.arange(LANES)
    grp = (lane // (2 * half)) % 2
    jj = lane % (2 * half)
    first = jj < half
    real = lane < n_real
    inv = ROPE_THETA ** (-jnp.asarray(jj % half, F32) / half)
    ang = jnp.where(jnp.asarray(grp == 0)[None, :], rows, cols) * inv[None, :]
    cos, sin = jnp.cos(ang), jnp.sin(ang)
    real_j, first_j = jnp.asarray(real)[None, :], jnp.asarray(first)[None, :]
    cos_t = jnp.where(real_j, cos, 1.0)
    sa_t = jnp.where(real_j & first_j, -sin, 0.0)
    sb_t = jnp.where(real_j & (~first_j), sin, 0.0)
    pad = lambda a, v: jnp.concatenate([a, jnp.full((n_ctx, LANES), v, F32)], axis=0)
    return pad(cos_t, 1.0), pad(sa_t, 0.0), pad(sb_t, 0.0)


def _rope(v, cos, sa, sb, half):
    return v * cos + pltpu.roll(v, LANES - half, 1) * sa + pltpu.roll(v, half, 1) * sb


def _mla_q_kernel(cq_ref, g_ref, w_ref, cos_ref, sa_ref, sb_ref, o_ref):
    n = _rms(cq_ref[...], g_ref[...]).astype(BF16)
    a = jnp.dot(n, w_ref[...], preferred_element_type=F32)
    cos, sa, sb = cos_ref[...], sa_ref[...], sb_ref[...]
    for h in range(MLA_HEADS):
        lo = 2 * LANES * h
        o_ref[:, lo:lo + LANES] = (a[:, lo:lo + LANES] * MLA_SCALE).astype(BF16)
        r = _rope(a[:, lo + LANES:lo + 2 * LANES], cos, sa, sb, MLA_ROPE // 4)
        o_ref[:, lo + LANES:lo + 2 * LANES] = (r * MLA_SCALE).astype(BF16)


def _prep_tile(T):
    return 3 * ROW_TILE if T % (3 * ROW_TILE) == 0 else ROW_TILE


def _mla_kv_kernel(ckv_ref, kr_ref, g_ref, w_ref, cos_ref, sa_ref, sb_ref, k_ref, v_ref):
    n = _rms(ckv_ref[...], g_ref[...]).astype(BF16)
    a = jnp.dot(n, w_ref[...], preferred_element_type=F32)
    kr = _rope(kr_ref[...], cos_ref[...], sa_ref[...], sb_ref[...], MLA_ROPE // 4).astype(BF16)
    ones = jnp.ones(kr.shape, BF16)
    for h in range(MLA_HEADS):
        lo = 2 * LANES * h
        k_ref[:, lo:lo + LANES] = a[:, lo:lo + LANES].astype(BF16)
        k_ref[:, lo + LANES:lo + 2 * LANES] = kr
        v_ref[:, lo:lo + LANES] = a[:, lo + LANES:lo + 2 * LANES].astype(BF16)
        v_ref[:, lo + LANES:lo + 2 * LANES] = ones


def _mla_qkv_kernel(cq_ref, gq_ref, wq_ref, ckv_ref, kr_ref, gkv_ref, wkv_ref, cos_ref, sa_ref, sb_ref,
                    q_ref, k_ref, v_ref):
    _mla_q_kernel(cq_ref, gq_ref, wq_ref, cos_ref, sa_ref, sb_ref, q_ref)
    _mla_kv_kernel(ckv_ref, kr_ref, gkv_ref, wkv_ref, cos_ref, sa_ref, sb_ref, k_ref, v_ref)


def _mla_qkv(p, g_qa, w_uq_ext, g_kva, w_ukv, tabs):
    T = p.shape[0]
    tm = _prep_tile(T)
    tab = pl.BlockSpec((tm, LANES), lambda i: (i, 0))
    wide = 2 * LANES * MLA_HEADS
    row = pl.BlockSpec((tm, wide), lambda i: (i, 0))
    return pl.pallas_call(
        _mla_qkv_kernel,
        grid=(T // tm,),
        in_specs=[pl.BlockSpec((tm, MLA_Q_RANK), lambda i: (i, P_CQ // MLA_Q_RANK)),
                  pl.BlockSpec((1, MLA_Q_RANK), lambda i: (0, 0)),
                  pl.BlockSpec((MLA_Q_RANK, wide), lambda i: (0, 0)),
                  pl.BlockSpec((tm, MLA_KV_RANK), lambda i: (i, P_CKV // MLA_KV_RANK)),
                  pl.BlockSpec((tm, LANES), lambda i: (i, P_KR // LANES)),
                  pl.BlockSpec((1, MLA_KV_RANK), lambda i: (0, 0)),
                  pl.BlockSpec((MLA_KV_RANK, wide), lambda i: (0, 0)),
                  tab, tab, tab],
        out_specs=[row, row, row],
        out_shape=[jax.ShapeDtypeStruct((T, wide), BF16)] * 3,
        compiler_params=_params(("arbitrary",), 56),
    )(p, g_qa.reshape(1, -1), w_uq_ext, p, p, g_kva.reshape(1, -1), w_ukv, *tabs)


def _flash_kernel(q_ref, k_ref, v_ref, *rest):
    o_ref, m_sc, acc_sc = rest[-3:]
    kk = pl.program_id(2)

    @pl.when(kk == 0)
    def _():
        m_sc[...] = jnp.full(m_sc.shape, -1e30, F32)
        acc_sc[...] = jnp.zeros(acc_sc.shape, F32)

    rc = q_ref.shape[0] // FLASH_SUBTILES
    subs = [slice(r * rc, (r + 1) * rc) for r in range(FLASH_SUBTILES)]
    scores = [lax.dot_general(q_ref[rows, :], k_ref[...], _NT, preferred_element_type=F32) for rows in subs]
    for rows, s in zip(subs, scores):
        m_prev = m_sc[rows, :]
        m_new = jnp.maximum(m_prev, jnp.max(s, axis=1, keepdims=True))
        alpha = jnp.exp(m_prev - m_new)
        p = jnp.exp((s - m_new).astype(BF16))
        acc_sc[rows, :] = alpha * acc_sc[rows, :] + jnp.dot(p, v_ref[...], preferred_element_type=F32)
        m_sc[rows, :] = m_new

    @pl.when(kk == pl.num_programs(2) - 1)
    def _():
        acc = acc_sc[...]
        o_ref[...] = (acc[:, :LANES] / acc[:, LANES:LANES + 1]).astype(o_ref.dtype)


def _flash(q, k, v, out_prev, n_rows, q_blk0, n_qblk, k_blk0, n_kblk, tq, tk):
    dq = 2 * LANES
    in_specs = [pl.BlockSpec((tq, dq), lambda h, i, kk: (i + q_blk0, h)),
                pl.BlockSpec((tk, dq), lambda h, i, kk: (kk + k_blk0, h)),
                pl.BlockSpec((tk, dq), lambda h, i, kk: (kk + k_blk0, h))]
    args = [q, k, v]
    aliases = {}
    if out_prev is not None:
        in_specs.append(pl.BlockSpec(memory_space=pl.ANY))
        args.append(out_prev)
        aliases = {3: 0}
    return pl.pallas_call(
        _flash_kernel,
        grid=(MLA_HEADS, n_qblk, n_kblk),
        in_specs=in_specs,
        out_specs=pl.BlockSpec((tq, LANES), lambda h, i, kk: (i + q_blk0, h)),
        out_shape=jax.ShapeDtypeStruct((n_rows, MLA_HEADS * LANES), BF16),
        scratch_shapes=[pltpu.VMEM((tq, 1), F32), pltpu.VMEM((tq, dq), F32)],
        input_output_aliases=aliases,
        compiler_params=_params(("arbitrary", "arbitrary", "arbitrary"), 48),
    )(*args)


def _gla_decays(q, k, glr, wg, bg, tri, last_row):
    z = jnp.dot(glr.astype(BF16), wg, preferred_element_type=F32) + bg
    la = (jnp.minimum(z, 0.0) - jnp.log(1.0 + jnp.exp(-jnp.abs(z)))) * (1.0 / GLA_GATE_TAU)
    ones = jnp.where(tri, 1.0, 0.0).astype(BF16)
    la_hi = la.astype(BF16)
    la_lo = (la - la_hi.astype(F32)).astype(BF16)
    b = (jnp.dot(ones, la_hi, preferred_element_type=F32)
         + jnp.dot(ones, la_lo, preferred_element_type=F32))
    b_tot = b[last_row:last_row + 1, :]
    qt = q * (GLA_DK ** -0.5) * jnp.exp(b)
    kt = k * jnp.exp(-b)
    kh = k * jnp.exp(b_tot - b)
    dec = jnp.exp(b_tot)
    return qt, kt, kh, dec


def _gla_head(h, decays, v, tri, st_ref, o_ref):
    qt, kt, kh, dec = decays
    ks = slice(GLA_DK * h, GLA_DK * (h + 1))
    vs = slice(GLA_DV * h, GLA_DV * (h + 1))
    qh = qt[:, ks].astype(BF16)
    vh = v[:, vs]
    st = st_ref[h]
    inter = lax.dot_general(qh, st.astype(BF16), _NT, preferred_element_type=F32)
    sc = lax.dot_general(qh, kt[:, ks].astype(BF16), _NT, preferred_element_type=F32)
    sc = jnp.where(tri, sc, 0.0)
    intra = jnp.dot(sc.astype(BF16), vh.astype(BF16), preferred_element_type=F32)
    o_ref[:, vs] = inter + intra
    st_ref[h] = st * dec[:, ks] + jnp.dot(vh.T.astype(BF16), kh[:, ks].astype(BF16),
                                           preferred_element_type=F32)


def _gla_kernel(qf, kf, vf, gf, qb, kb, vb, gb, wf, bf, wb, bb, of_ref, ob_ref, stf, stb):
    @pl.when(pl.program_id(0) == 0)
    def _():
        stf[...] = jnp.zeros(stf.shape, F32)
        stb[...] = jnp.zeros(stb.shape, F32)

    cs = qf.shape[0]
    t_i = lax.broadcasted_iota(jnp.int32, (cs, cs), 0)
    s_i = lax.broadcasted_iota(jnp.int32, (cs, cs), 1)
    tri_f, tri_b = s_i <= t_i, s_i >= t_i
    dec_f = _gla_decays(qf[...], kf[...], gf[...], wf[...], bf[...], tri_f, cs - 1)
    dec_b = _gla_decays(qb[...], kb[...], gb[...], wb[...], bb[...], tri_b, 0)
    v_f, v_b = vf[...], vb[...]
    for h in range(GLA_HEADS):
        _gla_head(h, dec_f, v_f, tri_f, stf, of_ref)
        _gla_head(h, dec_b, v_b, tri_b, stb, ob_ref)


def _gla(p, n_lat, wf_ext, bf, wb_ext, bb):
    T = p.shape[0]
    cs = GLA_CHUNK
    nch = T // cs
    nlc = n_lat // cs
    fwd = lambda c: (c + nlc) % nch
    bwd = lambda c: nch - 1 - c
    dk, dv = GLA_HEADS * GLA_DK, GLA_HEADS * GLA_DV

    def specs(idx):
        return [pl.BlockSpec((cs, dk), lambda c: (idx(c), P_GQ // dk)),
                pl.BlockSpec((cs, dk), lambda c: (idx(c), P_GK // dk)),
                pl.BlockSpec((cs, dv), lambda c: (idx(c), P_GV // dv)),
                pl.BlockSpec((cs, LANES), lambda c: (idx(c), P_GLR // LANES))]

    wspec = pl.BlockSpec((LANES, dk), lambda c: (0, 0))
    bspec = pl.BlockSpec((1, dk), lambda c: (0, 0))
    return pl.pallas_call(
        _gla_kernel,
        grid=(nch,),
        in_specs=specs(fwd) + specs(bwd) + [wspec, bspec, wspec, bspec],
        out_specs=[pl.BlockSpec((cs, dv), lambda c: (fwd(c), 0)),
                   pl.BlockSpec((cs, dv), lambda c: (bwd(c), 0))],
        out_shape=[jax.ShapeDtypeStruct((T, dv), F32), jax.ShapeDtypeStruct((T, dv), F32)],
        scratch_shapes=[pltpu.VMEM((GLA_HEADS, GLA_DV, GLA_DK), F32),
                        pltpu.VMEM((GLA_HEADS, GLA_DV, GLA_DK), F32)],
        compiler_params=_params(("arbitrary",)),
    )(p, p, p, p, p, p, p, p, wf_ext, bf.reshape(1, dk), wb_ext, bb.reshape(1, dk))


def _gla_out_kernel(of_ref, ob_ref, gr_ref, g_ref, o_ref):
    o = of_ref[...] + ob_ref[...]
    gr = gr_ref[...]
    for h in range(GLA_HEADS):
        vs = slice(GLA_DV * h, GLA_DV * (h + 1))
        o_ref[:, vs] = (_rms(o[:, vs], g_ref[...]) * _silu(gr[:, vs])).astype(BF16)


def _gla_out(o_f, o_b, p, g_gla):
    T, dv = o_f.shape
    tm = _prep_tile(T)
    row = pl.BlockSpec((tm, dv), lambda i: (i, 0))
    return pl.pallas_call(
        _gla_out_kernel,
        grid=(T // tm,),
        in_specs=[row, row, pl.BlockSpec((tm, dv), lambda i: (i, P_GR // dv)),
                  pl.BlockSpec((1, GLA_DV), lambda i: (0, 0))],
        out_specs=row,
        out_shape=jax.ShapeDtypeStruct((T, dv), BF16),
        compiler_params=_params(("arbitrary",)),
    )(o_f, o_b, p, g_gla.reshape(1, GLA_DV))


def _dft_cos_sin(n):
    k = np.arange(n)
    ang = 2.0 * np.pi * ((k[:, None] * k[None, :]) % n) / n
    return np.cos(ang), np.sin(ang)


def _channel_dft(xb, wd):
    zr, zi = [], []
    for g in range(FNET_GROUPS):
        z = jnp.dot(xb[:, LANES * g:LANES * (g + 1)], wd, preferred_element_type=F32)
        zr.append(z[:, :LANES])
        zi.append(z[:, LANES:])
    return jnp.concatenate(zr, axis=1), jnp.concatenate(zi, axis=1)


def _fft_a_kernel(n1, x_ref, wd_ref, m_ref, bre_ref, bim_ref):
    for n2 in range(FFT_N2):
        xs = x_ref[pl.ds(n2, n1, stride=FFT_N2), :].astype(BF16)
        z = jnp.dot(xs, wd_ref[...], preferred_element_type=F32)
        zs = jnp.concatenate([z[:, :LANES], z[:, LANES:]], axis=0).astype(BF16)
        b = jnp.dot(m_ref[n2], zs, preferred_element_type=F32)
        bre_ref[n2 * n1:(n2 + 1) * n1, :] = b[:n1]
        bim_ref[n2 * n1:(n2 + 1) * n1, :] = b[n1:]


def _fft_b_kernel(n1, scale, c_ref, s_ref, bre_ref, bim_ref, o_ref):
    grp = 8
    for t in range(n1 // grp):
        rows = [pl.ds(grp * t + r, FFT_N2, stride=n1) for r in range(grp)]
        br = jnp.concatenate([bre_ref[rw, :] for rw in rows], axis=1).astype(BF16)
        bi = jnp.concatenate([bim_ref[rw, :] for rw in rows], axis=1).astype(BF16)
        o = (jnp.dot(c_ref[...], br, preferred_element_type=F32)
             + jnp.dot(s_ref[...], bi, preferred_element_type=F32)) * scale
        for r in range(grp):
            o_ref[rows[r], :] = o[:, LANES * r:LANES * (r + 1)]


def _fft_ctx_kernel(scale, x_ref, wd_ref, cs_ref, prev_ref, o_ref):
    del prev_ref
    zr, zi = _channel_dft(x_ref[...].astype(BF16), wd_ref[...])
    z = jnp.concatenate([zr, zi], axis=0).astype(BF16)
    o_ref[...] = jnp.dot(cs_ref[...], z, preferred_element_type=F32) * scale


def _fourier(p, n_lat, n_ctx, with_ctx):
    T = p.shape[0]
    gw = FNET_GROUPS * LANES
    n1 = n_lat // FFT_N2
    cd, sd = _dft_cos_sin(LANES)
    as_bf16 = lambda a: jnp.asarray(a, F32).astype(BF16)
    wd = as_bf16(np.concatenate([cd, -sd], axis=1))
    k1 = jnp.arange(n1, dtype=jnp.int32)[None, :, None]
    pos = (FFT_N2 * jnp.arange(n1, dtype=jnp.int32)[None, None, :]
           + jnp.arange(FFT_N2, dtype=jnp.int32)[:, None, None])
    ang = (2.0 * np.pi / n_lat) * ((k1 * pos) % n_lat).astype(F32)
    gc, gs = jnp.cos(ang), jnp.sin(ang)
    m = jnp.concatenate([jnp.concatenate([gc, gs], axis=2),
                         jnp.concatenate([-gs, gc], axis=2)], axis=1).astype(BF16)
    col = lambda g: (0, g)
    bre, bim = pl.pallas_call(
        functools.partial(_fft_a_kernel, n1),
        grid=(FNET_GROUPS,),
        in_specs=[pl.BlockSpec((n_lat, LANES), lambda g: (0, P_FU // LANES + g)),
                  pl.BlockSpec((LANES, 2 * LANES), lambda g: (0, 0)),
                  pl.BlockSpec((FFT_N2, 2 * n1, 2 * n1), lambda g: (0, 0, 0))],
        out_specs=[pl.BlockSpec((n_lat, LANES), col)] * 2,
        out_shape=[jax.ShapeDtypeStruct((n_lat, gw), F32)] * 2,
        compiler_params=_params(("arbitrary",), 48),
    )(p, wd, m)

    c2, s2 = _dft_cos_sin(FFT_N2)
    mspec = pl.BlockSpec((FFT_N2, FFT_N2), lambda g: (0, 0))
    y = pl.pallas_call(
        functools.partial(_fft_b_kernel, n1, float((n_lat * LANES) ** -0.5)),
        grid=(FNET_GROUPS,),
        in_specs=[mspec, mspec, pl.BlockSpec((n_lat, LANES), col), pl.BlockSpec((n_lat, LANES), col)],
        out_specs=pl.BlockSpec((n_lat, LANES), col),
        out_shape=jax.ShapeDtypeStruct((T, gw), F32),
        compiler_params=_params(("arbitrary",)),
    )(as_bf16(c2), as_bf16(s2), bre, bim)

    if not with_ctx:
        return y
    cc, sc = _dft_cos_sin(n_ctx)
    return pl.pallas_call(
        functools.partial(_fft_ctx_kernel, float((n_ctx * LANES) ** -0.5)),
        grid=(1,),
        in_specs=[pl.BlockSpec((n_ctx, gw), lambda j: (n_lat // n_ctx, P_FU // gw)),
                  pl.BlockSpec((LANES, 2 * LANES), lambda j: (0, 0)),
                  pl.BlockSpec((n_ctx, 2 * n_ctx), lambda j: (0, 0)),
                  pl.BlockSpec(memory_space=pl.ANY)],
        out_specs=pl.BlockSpec((n_ctx, gw), lambda j: (n_lat // n_ctx, 0)),
        out_shape=jax.ShapeDtypeStruct((T, gw), F32),
        input_output_aliases={3: 0},
        compiler_params=_params(("arbitrary",)),
    )(p, wd, as_bf16(np.concatenate([cc, sc], axis=1)), y)


def _swa_prep_kernel(q_ref, k_ref, v_ref, cos_ref, sa_ref, sb_ref, qo_ref, ko_ref, vo_ref):
    cos, sa, sb = cos_ref[...], sa_ref[...], sb_ref[...]
    for h in range(SWA_HEADS):
        sl = slice(LANES * h, LANES * (h + 1))
        qo_ref[:, sl] = (_rope(q_ref[:, sl], cos, sa, sb, 32) * SWA_SCALE).astype(BF16)
    for h in range(SWA_KV_HEADS):
        sl = slice(LANES * h, LANES * (h + 1))
        ko_ref[:, sl] = _rope(k_ref[:, sl], cos, sa, sb, 32).astype(BF16)
    vo_ref[...] = v_ref[...].astype(BF16)


def _swa_prep(p, tabs):
    T = p.shape[0]
    tm = _prep_tile(T)
    qw, kw = SWA_HEADS * LANES, SWA_KV_HEADS * LANES
    tab = pl.BlockSpec((tm, LANES), lambda i: (i, 0))
    return pl.pallas_call(
        _swa_prep_kernel,
        grid=(T // tm,),
        in_specs=[pl.BlockSpec((tm, qw), lambda i: (i, P_SQ // qw)),
                  pl.BlockSpec((tm, kw), lambda i: (i, P_SK // kw)),
                  pl.BlockSpec((tm, kw), lambda i: (i, P_SV // kw)),
                  tab, tab, tab],
        out_specs=[pl.BlockSpec((tm, qw), lambda i: (i, 0)),
                   pl.BlockSpec((tm, kw), lambda i: (i, 0)),
                   pl.BlockSpec((tm, kw), lambda i: (i, 0))],
        out_shape=[jax.ShapeDtypeStruct((T, qw), BF16),
                   jax.ShapeDtypeStruct((T, kw), BF16),
                   jax.ShapeDtypeStruct((T, kw), BF16)],
        compiler_params=_params(("arbitrary",)),
    )(p, p, p, *tabs)


def _sink_attend(q, kcat, vcat, mask, sink):
    s = lax.dot_general(q, kcat, _NT, preferred_element_type=F32)
    if mask is not None:
        s = jnp.where(mask, s, -1e30)
    m = jnp.maximum(jnp.max(s, axis=1, keepdims=True), sink)
    pr = jnp.exp((s - m).astype(BF16))
    den = jnp.sum(pr.astype(F32), axis=1, keepdims=True) + jnp.exp(sink - m)
    return jnp.dot(pr, vcat, preferred_element_type=F32) / den


def _swa_lat_kernel(q_ref, kc_ref, kp_ref, k0_ref, kn_ref, vc_ref, vp_ref, v0_ref, vn_ref,
                    sink_ref, o_ref):
    kvh, nb = pl.program_id(0), pl.program_id(1)
    n_ctx = kc_ref.shape[0]
    qb = q_ref.shape[0]
    win = SWA_BLOCK
    kcat = jnp.concatenate([kc_ref[...], kp_ref[...], k0_ref[...], kn_ref[...]], axis=0)
    vcat = jnp.concatenate([vc_ref[...], vp_ref[...], v0_ref[...], vn_ref[...]], axis=0)
    n_keys = n_ctx + qb + 2 * win
    i = lax.broadcasted_iota(jnp.int32, (qb, n_keys), 0)
    j = lax.broadcasted_iota(jnp.int32, (qb, n_keys), 1)
    off = j - (n_ctx + win)
    in_band = jnp.abs(i - off) <= win
    exists = ((off >= 0) | (nb > 0)) & ((off < qb) | (nb < pl.num_programs(1) - 1))
    mask = (j < n_ctx) | (in_band & exists)
    for g in range(SWA_GROUP):
        sl = slice(LANES * g, LANES * (g + 1))
        sink = sink_ref[pl.ds(kvh * SWA_GROUP + g, 1), 0:1]
        o_ref[:, sl] = _sink_attend(q_ref[:, sl], kcat, vcat, mask, sink).astype(BF16)


def _swa_ctx_kernel(q_ref, k_ref, v_ref, sink_ref, prev_ref, o_ref):
    del prev_ref
    kvh = pl.program_id(0)
    for g in range(SWA_GROUP):
        sl = slice(LANES * g, LANES * (g + 1))
        sink = sink_ref[pl.ds(kvh * SWA_GROUP + g, 1), 0:1]
        o_ref[:, sl] = _sink_attend(q_ref[:, sl], k_ref[...], v_ref[...], None, sink).astype(BF16)


def _swa(qs, ks, vs, sink, n_lat, n_ctx, with_ctx):
    T = qs.shape[0]
    win = SWA_BLOCK
    qb = SWA_QBLOCK
    nb = n_lat // qb
    per = qb // win
    gq = SWA_GROUP * LANES
    sink2d = jnp.broadcast_to(sink.astype(F32)[:, None], (SWA_HEADS, LANES))
    cblk = n_lat // n_ctx
    ctx_spec = pl.BlockSpec((n_ctx, LANES), lambda h, b: (cblk, h))
    prev_spec = pl.BlockSpec((win, LANES), lambda h, b: (jnp.maximum(b * per - 1, 0), h))
    cur_spec = pl.BlockSpec((qb, LANES), lambda h, b: (b, h))
    next_spec = pl.BlockSpec((win, LANES), lambda h, b: (jnp.minimum((b + 1) * per, nb * per - 1), h))
    sink_spec = pl.BlockSpec((SWA_HEADS, LANES), lambda h, b: (0, 0))
    o_lat = pl.pallas_call(
        _swa_lat_kernel,
        grid=(SWA_KV_HEADS, nb),
        in_specs=[pl.BlockSpec((qb, gq), lambda h, b: (b, h)),
                  ctx_spec, prev_spec, cur_spec, next_spec,
                  ctx_spec, prev_spec, cur_spec, next_spec, sink_spec],
        out_specs=pl.BlockSpec((qb, gq), lambda h, b: (b, h)),
        out_shape=jax.ShapeDtypeStruct((T, SWA_HEADS * LANES), BF16),
        compiler_params=_params(("arbitrary", "arbitrary")),
    )(qs, ks, ks, ks, ks, vs, vs, vs, vs, sink2d)
    if not with_ctx:
        return o_lat
    return pl.pallas_call(
        _swa_ctx_kernel,
        grid=(SWA_KV_HEADS,),
        in_specs=[pl.BlockSpec((n_ctx, gq), lambda h: (cblk, h)),
                  pl.BlockSpec((n_ctx, LANES), lambda h: (cblk, h)),
                  pl.BlockSpec((n_ctx, LANES), lambda h: (cblk, h)),
                  pl.BlockSpec((SWA_HEADS, LANES), lambda h: (0, 0)),
                  pl.BlockSpec(memory_space=pl.ANY)],
        out_specs=pl.BlockSpec((n_ctx, gq), lambda h: (cblk, h)),
        out_shape=jax.ShapeDtypeStruct((T, SWA_HEADS * LANES), BF16),
        input_output_aliases={4: 0},
        compiler_params=_params(("arbitrary",)),
    )(qs, ks, vs, sink2d, o_lat)


GATE_TILE = 128
GATE_HALO = 16


def _gate_kernel(tm, n_lat, n_tot, ug_ref, ua_ref, gp_ref, gn_ref, ap_ref, an_ref, cwg_ref, cwa_ref,
                 cbg_ref, cba_ref, o_ref, ubuf):
    i = pl.program_id(0)
    first = (i == 0) | (i * tm == n_lat)
    last = ((i + 1) * tm == n_lat) | ((i + 1) * tm == n_tot)
    rows = tm + 2 * GATE_HALO
    out_r = lax.broadcasted_iota(jnp.int32, (2 * tm, rows), 0)
    src_r = lax.broadcasted_iota(jnp.int32, (2 * tm, rows), 1)
    want = jnp.where(out_r < tm, out_r + (GATE_HALO - 1), out_r - tm + (GATE_HALO + 1))
    pick = jnp.where(src_r == want, 1.0, 0.0).astype(BF16)

    def conv(u_ref, p_ref, n_ref, cw_ref, cb_ref):
        ubuf[0:GATE_HALO, :] = jnp.where(first, jnp.zeros_like(p_ref[...]), p_ref[...])
        ubuf[GATE_HALO:GATE_HALO + tm, :] = u_ref[...]
        ubuf[GATE_HALO + tm:, :] = jnp.where(last, jnp.zeros_like(n_ref[...]), n_ref[...])
        shifted = jnp.dot(pick, ubuf[...], preferred_element_type=F32)
        um, up = shifted[:tm], shifted[tm:]
        return (um * cw_ref[0:1, :] + u_ref[...].astype(F32) * cw_ref[1:2, :] + up * cw_ref[2:3, :]
                + cb_ref[...])

    g = conv(ug_ref, gp_ref, gn_ref, cwg_ref, cbg_ref)
    a = conv(ua_ref, ap_ref, an_ref, cwa_ref, cba_ref)
    o_ref[...] = (_silu(g) * a).astype(BF16)


def _gate(u, n_lat, conv_w, conv_b):
    T = u.shape[0]
    dff = u.shape[1] // 2
    tm = GATE_TILE
    hb = tm // GATE_HALO
    last_hb = T // GATE_HALO - 1
    cb2 = conv_b.reshape(1, 2 * dff)
    prev = lambda i: jnp.maximum(i * hb - 1, 0)
    nxt = lambda i: jnp.minimum((i + 1) * hb, last_hb)
    return pl.pallas_call(
        functools.partial(_gate_kernel, tm, n_lat, T),
        grid=(T // tm,),
        in_specs=[pl.BlockSpec((tm, dff), lambda i: (i, 0)),
                  pl.BlockSpec((tm, dff), lambda i: (i, 1)),
                  pl.BlockSpec((GATE_HALO, dff), lambda i: (prev(i), 0)),
                  pl.BlockSpec((GATE_HALO, dff), lambda i: (nxt(i), 0)),
                  pl.BlockSpec((GATE_HALO, dff), lambda i: (prev(i), 1)),
                  pl.BlockSpec((GATE_HALO, dff), lambda i: (nxt(i), 1)),
                  pl.BlockSpec((3, dff), lambda i: (0, 0)),
                  pl.BlockSpec((3, dff), lambda i: (0, 1)),
                  pl.BlockSpec((1, dff), lambda i: (0, 0)),
                  pl.BlockSpec((1, dff), lambda i: (0, 1))],
        out_specs=pl.BlockSpec((tm, dff), lambda i: (i, 0)),
        out_shape=jax.ShapeDtypeStruct((T, dff), BF16),
        scratch_shapes=[pltpu.VMEM((tm + 2 * GATE_HALO, dff), BF16)],
        compiler_params=_params(("arbitrary",), 48),
    )(u, u, u, u, u, u, conv_w, conv_w, cb2, cb2)


def _w_in_ext(w_in):
    cq, ckv, kr, gq, gk, gv, gr, glr, fu, sq, sk, sv = jnp.split(
        w_in, np.cumsum([768, 512, 64, 512, 512, 1024, 1024, 32, 1024, 1024, 256]).tolist(), axis=2)
    pad = lambda w: jnp.pad(w, ((0, 0), (0, 0), (0, LANES - w.shape[2])))
    parts = [fu, gv, gr, cq, sk, sq, ckv, gq, gk, sv, pad(kr), pad(glr)]
    return jnp.concatenate(parts, axis=2).astype(BF16)


def _tile_for(total, parts, mult=16):
    t = total // parts
    assert t * parts == total and t % mult == 0, (total, parts)
    return t


def _key_tile(total):
    best = LANES
    for t in range(LANES, 3072 + 1, LANES):
        if total % t == 0:
            best = t
    return best


def kernel(x, c, ctx, c_ctx, w_ada, b_ada, g_pre_mix, g_post_mix, g_pre_ffn, g_post_ffn, w_in, g_qa, w_uq,
           g_kva, w_ukv, w_gate_f, b_gate_f, w_gate_b, b_gate_b, g_gla, swa_sink, w_out, w_up, conv_w, conv_b,
           w_down):
    B, N, D = x.shape
    C = ctx.shape[1]
    T = N + C
    L = w_ada.shape[0]
    dff = w_down.shape[1]
    assert B == 1 and D == 4096 and N % (GRID_W * 16) == 0 and C % ROW_TILE == 0 and N % C == 0
    x2d, ctx2d = x[0], ctx[0]

    cc = jnp.zeros((8, D), F32).at[0].set(c[0]).at[1].set(c_ctx)
    mod = _ada(cc, w_ada, b_ada)
    mla_tabs = _rope_tables(N, C, MLA_ROPE, MLA_ROPE // 4)
    swa_tabs = _rope_tables(N, C, LANES, 32)


    w_in_x = _w_in_ext(w_in)
    w_down_b = w_down.astype(BF16)
    tq, tk = _tile_for(N, 4), _key_tile(T)

    h = _rms_mod(x2d, ctx2d, g_pre_mix[0], mod[0], 0)
    xs = (x2d, ctx2d)
    for l in range(L):
        last = l == L - 1
        p = _matmul([h], w_in_x, l, _tile_for(T, 8), 1024, F32)

        w_uq_ext = jnp.pad(w_uq[l].reshape(MLA_Q_RANK, MLA_HEADS, MLA_NOPE + MLA_ROPE),
                           ((0, 0), (0, 0), (0, 2 * LANES - MLA_NOPE - MLA_ROPE))
                           ).reshape(MLA_Q_RANK, 2 * LANES * MLA_HEADS).astype(BF16)
        q_a, k_a, v_a = _mla_qkv(p, g_qa[l], w_uq_ext, g_kva[l], w_ukv[l].astype(BF16), mla_tabs)
        o_a = _flash(q_a, k_a, v_a, None, T, 0, N // tq, 0, T // tk, tq, tk)
        if not last:
            o_a = _flash(q_a, k_a, v_a, o_a, T, N // C, 1, N // C, 1, C, C)

        wf_ext = jnp.zeros((LANES, GLA_HEADS * GLA_DK), F32).at[:GLA_GATE_RANK].set(w_gate_f[l]).astype(BF16)
        wb_ext = jnp.zeros((LANES, GLA_HEADS * GLA_DK), F32).at[GLA_GATE_RANK:2 * GLA_GATE_RANK].set(
            w_gate_b[l]).astype(BF16)
        o_f, o_bk = _gla(p, N, wf_ext, b_gate_f[l], wb_ext, b_gate_b[l])
        o_b = _gla_out(o_f, o_bk, p, g_gla[l])

        o_c = _fourier(p, N, C, not last)

        qs, ks, vs = _swa_prep(p, swa_tabs)
        o_d = _swa(qs, ks, vs, swa_sink[l], N, C, not last)

        rows = N if last else T
        y = _matmul([o_a, o_b, o_c, o_d], w_out, l, _tile_for(rows, 4), 512, F32, True, m_rows=rows)
        x1, h2 = _resid(xs, y, g_post_mix[l], mod[l], 2, N, rows, pre=(g_pre_ffn[l], mod[l], 3))
        u = _matmul([h2], w_up, l, _tile_for(rows, 4), 512, BF16, True)
        act = _gate(u, N, conv_w[l], conv_b[l])
        f = _matmul([act], w_down_b, l, _tile_for(rows, 16), 512, F32)
        if last:
            (x_out,) = _resid((x1,), f, g_post_ffn[l], mod[l], 5, N, N)
            return x_out.reshape(1, N, D)
        x2, h = _resid((x1,), f, g_post_ffn[l], mod[l], 5, N, T, pre=(g_pre_mix[l + 1], mod[l + 1], 0))
        xs = (x2,)
```
